```python
import jax
import jax.numpy as jnp
from jax import lax
import numpy as np

D_MODEL = 2048
BATCH = 8
SEQ = 8192
DEPTH = 4

GRID_W = 64
CTX_LEN = 256
N_MIXERS = 3
HEAD_DIM = 128
N_HEADS = D_MODEL // HEAD_DIM
N_KV_HEADS = N_HEADS // 4
GQA_GROUP = N_HEADS // N_KV_HEADS
WINDOW = 128
BLOCK = 128
ROPE_THETA = 10000.0
ROPE_FREQS = HEAD_DIM // 4
LRU_WIDTH = D_MODEL
LRU_BLOCK_SIZE = 128
LRU_BLOCKS = LRU_WIDTH // LRU_BLOCK_SIZE
LRU_C = 8.0
LRU_CONV = 4
CONF_WIDTH = 31
D_FF = ((8 * D_MODEL // 3 + 255) // 256) * 256
FFN_CONV = 3
NORM_EPS = 1e-6
NEG_INF = -1e30
N_ATTN_LAYERS = (DEPTH + 2) // 3
N_LRU_LAYERS = (DEPTH + 1) // 3
N_CONV_LAYERS = DEPTH // 3

kernel_name = "hybrid_interleaved_diffusion_block"


def rms_norm(x, g):
    x32 = x.astype(jnp.float32)
    y = x32 * lax.rsqrt(jnp.mean(x32 * x32, axis=-1, keepdims=True) + NORM_EPS)
    return y.astype(x.dtype) * g


def layer_norm(x, g, b):
    x32 = x.astype(jnp.float32)
    mu = jnp.mean(x32, axis=-1, keepdims=True)
    var = jnp.mean(jnp.square(x32 - mu), axis=-1, keepdims=True)
    return ((x32 - mu) * lax.rsqrt(var + NORM_EPS)).astype(x.dtype) * g + b


def modulate(x, g, shift, scale):
    return rms_norm(x, g) * (1.0 + scale) + shift


def ada_mod(cvec, w, b):
    m = jax.nn.silu(cvec) @ w + b
    return [t[:, None, :] for t in jnp.split(m, 6, axis=-1)]


def depthwise_conv(x, w, b, pad_left, pad_right):
    y = lax.conv_general_dilated(
        x, w[:, None, :].astype(x.dtype), window_strides=(1,),
        padding=[(pad_left, pad_right)], dimension_numbers=("NWC", "WIO", "NWC"),
        feature_group_count=x.shape[-1])
    return y + b


def axial_rope_tables(n_tokens):
    rows = n_tokens // GRID_W
    row = jnp.repeat(jnp.arange(rows), GRID_W)
    col = jnp.arange(rows * GRID_W) % GRID_W
    pos = jnp.stack([row, col], axis=-1).astype(jnp.float32)
    freq = ROPE_THETA ** (-jnp.arange(ROPE_FREQS, dtype=jnp.float32) / ROPE_FREQS)
    ang = pos[:, :, None] * freq
    return jnp.cos(ang), jnp.sin(ang)


def apply_axial_rope(x, cos, sin):
    b, s, n, _ = x.shape
    xr = x.reshape(b, s, n, 2, 2, ROPE_FREQS)
    x1, x2 = xr[..., 0, :], xr[..., 1, :]
    cs = cos[None, :, None].astype(x.dtype)
    sn = sin[None, :, None].astype(x.dtype)
    return jnp.stack([x1 * cs - x2 * sn, x2 * cs + x1 * sn], axis=-2).reshape(x.shape)


def attention_mixer(h, hc, w_qkv, w_o, sink, cos, sin, need_ctx_out):
    bsz, n_tok, _ = h.shape
    n_ctx = hc.shape[1]
    q_cols = N_HEADS * HEAD_DIM
    kv_cols = N_KV_HEADS * HEAD_DIM
    scale = HEAD_DIM ** -0.5
    qkv = h @ w_qkv
    q = apply_axial_rope(qkv[..., :q_cols].reshape(bsz, n_tok, N_HEADS, HEAD_DIM), cos, sin)
    k = apply_axial_rope(qkv[..., q_cols:q_cols + kv_cols].reshape(bsz, n_tok, N_KV_HEADS, HEAD_DIM), cos, sin)
    v = qkv[..., q_cols + kv_cols:].reshape(bsz, n_tok, N_KV_HEADS, HEAD_DIM)
    kv_c = hc @ w_qkv[:, q_cols:]
    k_c = kv_c[..., :kv_cols].reshape(bsz, n_ctx, N_KV_HEADS, HEAD_DIM)
    v_c = kv_c[..., kv_cols:].reshape(bsz, n_ctx, N_KV_HEADS, HEAD_DIM)
    sink_kg = sink.astype(jnp.float32).reshape(N_KV_HEADS, GQA_GROUP)

    n_blk = n_tok // BLOCK
    qb = q.reshape(bsz, n_blk, BLOCK, N_KV_HEADS, GQA_GROUP, HEAD_DIM)

    def band(t):
        tp = jnp.pad(t, ((0, 0), (BLOCK, BLOCK), (0, 0), (0, 0)))
        tp = tp.reshape(bsz, n_blk + 2, BLOCK, N_KV_HEADS, HEAD_DIM)
        return jnp.concatenate([tp[:, :-2], tp[:, 1:-1], tp[:, 2:]], axis=2)

    kb, vb = band(k), band(v)
    s_loc = jnp.einsum('bnqkgd,bnskd->bnkgqs', qb, kb).astype(jnp.float32) * scale
    s_ctx = jnp.einsum('bnqkgd,bckd->bnkgqc', qb, k_c).astype(jnp.float32) * scale
    blk = jnp.arange(n_blk)[:, None, None]
    qpos = blk * BLOCK + jnp.arange(BLOCK)[None, :, None]
    kpos = (blk - 1) * BLOCK + jnp.arange(3 * BLOCK)[None, None, :]
    valid = (jnp.abs(qpos - kpos) <= WINDOW) & (kpos >= 0) & (kpos < n_tok)
    s_loc = jnp.where(valid[None, :, None, None], s_loc, NEG_INF)
    sink_col = jnp.broadcast_to(sink_kg[None, None, :, :, None, None], s_loc.shape[:-1] + (1,))
    p = jax.nn.softmax(jnp.concatenate([s_loc, s_ctx, sink_col], axis=-1), axis=-1).astype(v.dtype)
    n_loc = 3 * BLOCK
    o = (jnp.einsum('bnkgqs,bnskd->bnqkgd', p[..., :n_loc], vb)
         + jnp.einsum('bnkgqc,bckd->bnqkgd', p[..., n_loc:n_loc + n_ctx], v_c))
    y = o.reshape(bsz, n_tok, q_cols) @ w_o
    if not need_ctx_out:
        return y, None
    q_c = (hc @ w_qkv[:, :q_cols]).reshape(bsz, n_ctx, N_KV_HEADS, GQA_GROUP, HEAD_DIM)
    s_c = jnp.einsum('bqkgd,bckd->bkgqc', q_c, k_c).astype(jnp.float32) * scale
    sink_c = jnp.broadcast_to(sink_kg[None, :, :, None, None], s_c.shape[:-1] + (1,))
    p_c = jax.nn.softmax(jnp.concatenate([s_c, sink_c], axis=-1), axis=-1).astype(v_c.dtype)
    o_c = jnp.einsum('bkgqc,bckd->bqkgd', p_c[..., :n_ctx], v_c)
    yc = o_c.reshape(bsz, n_ctx, q_cols) @ w_o
    return y, yc


def block_diag_linear(u, w, b):
    t = u.reshape(*u.shape[:-1], LRU_BLOCKS, LRU_BLOCK_SIZE)
    return jnp.einsum('btnd,nde->btne', t, w).reshape(u.shape) + b


def linear_scan(a, b, h0, reverse):
    if h0 is not None:
        idx = -1 if reverse else 0
        b = b.at[:, idx].add(a[:, idx] * h0)

    def combine(e1, e2):
        a1, b1 = e1
        a2, b2 = e2
        return a1 * a2, a2 * b1 + b2

    _, hs = lax.associative_scan(combine, (a, b), reverse=reverse, axis=1)
    return hs


def rglru_mixer(h, hc, w_in, conv_w, conv_b, wa, ba, wx, bx, lam, w_out, need_ctx_out):
    gate_l, xb_l = jnp.split(h @ w_in, 2, axis=-1)
    if need_ctx_out:
        gate_c, xb_c = jnp.split(hc @ w_in, 2, axis=-1)
    else:
        gate_c, xb_c = None, hc @ w_in[:, LRU_WIDTH:]
    sum_l = None
    sum_c = None
    for d in range(2):
        rev = d == 1
        pad = (0, LRU_CONV - 1) if rev else (LRU_CONV - 1, 0)

        def gates(u):
            uc = depthwise_conv(u, conv_w[d], conv_b[d], pad[0], pad[1])
            r = jax.nn.sigmoid(block_diag_linear(uc, wa[d], ba[d])).astype(jnp.float32)
            ig = jax.nn.sigmoid(block_diag_linear(uc, wx[d], bx[d])).astype(jnp.float32)
            log_a = -LRU_C * r * jax.nn.softplus(-lam[d].astype(jnp.float32))
            a = jnp.exp(log_a)
            bterm = jnp.sqrt(-jnp.expm1(2.0 * log_a)) * (ig * uc.astype(jnp.float32))
            return a, bterm

        a_c, b_c = gates(xb_c)
        hs_c = linear_scan(a_c, b_c, None, rev)
        h_end = hs_c[:, 0] if rev else hs_c[:, -1]
        a_l, b_l = gates(xb_l)
        hs_l = linear_scan(a_l, b_l, h_end, rev)
        sum_l = hs_l if sum_l is None else sum_l + hs_l
        sum_c = hs_c if sum_c is None else sum_c + hs_c
    y = (jax.nn.gelu(gate_l) * sum_l.astype(h.dtype)) @ w_out
    if not need_ctx_out:
        return y, None
    yc = (jax.nn.gelu(gate_c) * sum_c.astype(hc.dtype)) @ w_out
    return y, yc


def conformer_conv_mixer(u, w_in, b_in, dw_w, dw_b, ln_g, ln_b, w_out, b_out):
    z = u @ w_in + b_in
    z = z[..., :D_MODEL] * jax.nn.sigmoid(z[..., D_MODEL:])
    half = CONF_WIDTH // 2
    z = depthwise_conv(z, dw_w, dw_b, half, half)
    z = jax.nn.silu(layer_norm(z, ln_g, ln_b))
    return z @ w_out + b_out


def conv_ffn(u, w_up, conv_w, conv_b, w_down):
    g, v = jnp.split(u @ w_up, 2, axis=-1)
    g = depthwise_conv(g, conv_w, conv_b, FFN_CONV // 2, FFN_CONV // 2)
    return (jax.nn.silu(g) * v) @ w_down


def _fwd_setup_inputs(seed: int = 0) -> dict:
    key = jax.random.key(seed)
    ks = jax.random.split(key, 40)
    f32 = jnp.float32
    D, F, R = D_MODEL, D_FF, LRU_WIDTH

    def nrm(i, shape, scale):
        return jax.random.normal(ks[i], shape, f32) * scale

    qkv_cols = (N_HEADS + 2 * N_KV_HEADS) * HEAD_DIM
    u = jax.random.uniform(ks[0], (N_LRU_LAYERS, 2, R), f32, 0.9, 0.999)
    s = u ** (1.0 / LRU_C)
    lam = jnp.log(s) - jnp.log1p(-s)
    return {
        "x": nrm(1, (BATCH, SEQ, D), 1.0),
        "c": nrm(2, (BATCH, D), 1.0),
        "ctx": nrm(3, (BATCH, CTX_LEN, D), 1.0),
        "c_ctx": nrm(4, (D,), 1.0),
        "ada_w": nrm(5, (DEPTH, D, 6 * D), 0.5 * D ** -0.5),
        "ada_b": nrm(6, (DEPTH, 6 * D), 0.02),
        "norm_mix_g": 1.0 + nrm(7, (DEPTH, D), 0.02),
        "norm_ffn_g": 1.0 + nrm(8, (DEPTH, D), 0.02),
        "attn_w_qkv": nrm(9, (N_ATTN_LAYERS, D, qkv_cols), D ** -0.5),
        "attn_w_o": nrm(10, (N_ATTN_LAYERS, N_HEADS * HEAD_DIM, D), (N_HEADS * HEAD_DIM) ** -0.5),
        "attn_sink": nrm(11, (N_ATTN_LAYERS, N_HEADS), 0.5),
        "lru_w_in": nrm(12, (N_LRU_LAYERS, D, 2 * R), D ** -0.5),
        "lru_conv_w": nrm(13, (N_LRU_LAYERS, 2, LRU_CONV, R), LRU_CONV ** -0.5),
        "lru_conv_b": nrm(14, (N_LRU_LAYERS, 2, R), 0.02),
        "lru_wa": nrm(15, (N_LRU_LAYERS, 2, LRU_BLOCKS, LRU_BLOCK_SIZE, LRU_BLOCK_SIZE), LRU_BLOCK_SIZE ** -0.5),
        "lru_ba": nrm(16, (N_LRU_LAYERS, 2, R), 0.02),
        "lru_wx": nrm(17, (N_LRU_LAYERS, 2, LRU_BLOCKS, LRU_BLOCK_SIZE, LRU_BLOCK_SIZE), LRU_BLOCK_SIZE ** -0.5),
        "lru_bx": nrm(18, (N_LRU_LAYERS, 2, R), 0.02),
        "lru_lambda": lam,
        "lru_w_out": nrm(19, (N_LRU_LAYERS, R, D), R ** -0.5),
        "conf_w_in": nrm(20, (N_CONV_LAYERS, D, 2 * D), D ** -0.5),
        "conf_b_in": nrm(21, (N_CONV_LAYERS, 2 * D), 0.02),
        "conf_dw_w": nrm(22, (N_CONV_LAYERS, CONF_WIDTH, D), CONF_WIDTH ** -0.5),
        "conf_dw_b": nrm(23, (N_CONV_LAYERS, D), 0.02),
        "conf_ln_g": 1.0 + nrm(24, (N_CONV_LAYERS, D), 0.02),
        "conf_ln_b": nrm(25, (N_CONV_LAYERS, D), 0.02),
        "conf_w_out": nrm(26, (N_CONV_LAYERS, D, D), D ** -0.5),
        "conf_b_out": nrm(27, (N_CONV_LAYERS, D), 0.02),
        "ffn_w_up": nrm(28, (DEPTH, D, 2 * F), D ** -0.5),
        "ffn_conv_w": nrm(29, (DEPTH, FFN_CONV, F), FFN_CONV ** -0.5),
        "ffn_conv_b": nrm(30, (DEPTH, F), 0.02),
        "ffn_w_down": nrm(31, (DEPTH, F, D), F ** -0.5),
        "final_norm_g": 1.0 + nrm(32, (D,), 0.02),
    }


def _fwd_reference(x, c, ctx, c_ctx, ada_w, ada_b, norm_mix_g, norm_ffn_g, attn_w_qkv, attn_w_o, attn_sink,
              lru_w_in, lru_conv_w, lru_conv_b, lru_wa, lru_ba, lru_wx, lru_bx, lru_lambda, lru_w_out,
              conf_w_in, conf_b_in, conf_dw_w, conf_dw_b, conf_ln_g, conf_ln_b, conf_w_out, conf_b_out,
              ffn_w_up, ffn_conv_w, ffn_conv_b, ffn_w_down, final_norm_g):
    cos, sin = axial_rope_tables(x.shape[1])
    xc = ctx
    c_ctx_row = c_ctx[None, :]
    for i in range(DEPTH):
        last = i == DEPTH - 1
        kind, j = i % N_MIXERS, i // N_MIXERS
        sh1, sc1, g1, sh2, sc2, g2 = ada_mod(c, ada_w[i], ada_b[i])
        h = modulate(x, norm_mix_g[i], sh1, sc1)
        ctx_used = (not last) or kind != 2
        if ctx_used:
            csh1, csc1, cg1, csh2, csc2, cg2 = ada_mod(c_ctx_row, ada_w[i], ada_b[i])
            hc = modulate(xc, norm_mix_g[i], csh1, csc1)
        if kind == 0:
            y, yc = attention_mixer(h, hc, attn_w_qkv[j], attn_w_o[j], attn_sink[j], cos, sin, not last)
        elif kind == 1:
            y, yc = rglru_mixer(h, hc, lru_w_in[j], lru_conv_w[j], lru_conv_b[j], lru_wa[j], lru_ba[j],
                                lru_wx[j], lru_bx[j], lru_lambda[j], lru_w_out[j], not last)
        else:
            conf_args = (conf_w_in[j], conf_b_in[j], conf_dw_w[j], conf_dw_b[j], conf_ln_g[j], conf_ln_b[j],
                         conf_w_out[j], conf_b_out[j])
            y = conformer_conv_mixer(h, *conf_args)
            yc = None if last else conformer_conv_mixer(hc, *conf_args)
        x = x + g1 * y
        x = x + g2 * conv_ffn(modulate(x, norm_ffn_g[i], sh2, sc2), ffn_w_up[i], ffn_conv_w[i], ffn_conv_b[i],
                              ffn_w_down[i])
        if not last:
            xc = xc + cg1 * yc
            xc = xc + cg2 * conv_ffn(modulate(xc, norm_ffn_g[i], csh2, csc2), ffn_w_up[i], ffn_conv_w[i],
                                     ffn_conv_b[i], ffn_w_down[i])
    return rms_norm(x, final_norm_g)


import jax as _jax
import jax.numpy as _jnp

TWIN_FORMAT = 'train_step'
FWD_PARAMS = ['x', 'c', 'ctx', 'c_ctx', 'ada_w', 'ada_b', 'norm_mix_g', 'norm_ffn_g', 'attn_w_qkv', 'attn_w_o', 'attn_sink', 'lru_w_in', 'lru_conv_w', 'lru_conv_b', 'lru_wa', 'lru_ba', 'lru_wx', 'lru_bx', 'lru_lambda', 'lru_w_out', 'conf_w_in', 'conf_b_in', 'conf_dw_w', 'conf_dw_b', 'conf_ln_g', 'conf_ln_b', 'conf_w_out', 'conf_b_out', 'ffn_w_up', 'ffn_conv_w', 'ffn_conv_b', 'ffn_w_down', 'final_norm_g']
TWIN_WEIGHTS = ['c_ctx', 'ada_w', 'ada_b', 'norm_mix_g', 'norm_ffn_g', 'attn_w_qkv', 'attn_w_o', 'attn_sink', 'lru_w_in', 'lru_conv_w', 'lru_conv_b', 'lru_wa', 'lru_ba', 'lru_wx', 'lru_bx', 'lru_lambda', 'lru_w_out', 'conf_w_in', 'conf_b_in', 'conf_dw_w', 'conf_dw_b', 'conf_ln_g', 'conf_ln_b', 'conf_w_out', 'conf_b_out', 'ffn_w_up', 'ffn_conv_w', 'ffn_conv_b', 'ffn_w_down', 'final_norm_g']
TWIN_DIFF_INPUT = 'x'
TWIN_INPUTS = ['x', 'c', 'ctx', 'c_ctx', 'ada_w', 'ada_b', 'norm_mix_g', 'norm_ffn_g', 'attn_w_qkv', 'attn_w_o', 'attn_sink', 'lru_w_in', 'lru_conv_w', 'lru_conv_b', 'lru_wa', 'lru_ba', 'lru_wx', 'lru_bx', 'lru_lambda', 'lru_w_out', 'conf_w_in', 'conf_b_in', 'conf_dw_w', 'conf_dw_b', 'conf_ln_g', 'conf_ln_b', 'conf_w_out', 'conf_b_out', 'ffn_w_up', 'ffn_conv_w', 'ffn_conv_b', 'ffn_w_down', 'final_norm_g', 'loss_target', 'm_c_ctx', 'm_ada_w', 'm_ada_b', 'm_norm_mix_g', 'm_norm_ffn_g', 'm_attn_w_qkv', 'm_attn_w_o', 'm_attn_sink', 'm_lru_w_in', 'm_lru_conv_w', 'm_lru_conv_b', 'm_lru_wa', 'm_lru_ba', 'm_lru_wx', 'm_lru_bx', 'm_lru_lambda', 'm_lru_w_out', 'm_conf_w_in', 'm_conf_b_in', 'm_conf_dw_w', 'm_conf_dw_b', 'm_conf_ln_g', 'm_conf_ln_b', 'm_conf_w_out', 'm_conf_b_out', 'm_ffn_w_up', 'm_ffn_conv_w', 'm_ffn_conv_b', 'm_ffn_w_down', 'm_final_norm_g', 'v_c_ctx', 'v_ada_w', 'v_ada_b', 'v_norm_mix_g', 'v_norm_ffn_g', 'v_attn_w_qkv', 'v_attn_w_o', 'v_attn_sink', 'v_lru_w_in', 'v_lru_conv_w', 'v_lru_conv_b', 'v_lru_wa', 'v_lru_ba', 'v_lru_wx', 'v_lru_bx', 'v_lru_lambda', 'v_lru_w_out', 'v_conf_w_in', 'v_conf_b_in', 'v_conf_dw_w', 'v_conf_dw_b', 'v_conf_ln_g', 'v_conf_ln_b', 'v_conf_w_out', 'v_conf_b_out', 'v_ffn_w_up', 'v_ffn_conv_w', 'v_ffn_conv_b', 'v_ffn_w_down', 'v_final_norm_g']
TWIN_OUTPUTS = ['loss', 'grad_x', 'grad_c_ctx', 'grad_ada_w', 'grad_ada_b', 'grad_norm_mix_g', 'grad_norm_ffn_g', 'grad_attn_w_qkv', 'grad_attn_w_o', 'grad_attn_sink', 'grad_lru_w_in', 'grad_lru_conv_w', 'grad_lru_conv_b', 'grad_lru_wa', 'grad_lru_ba', 'grad_lru_wx', 'grad_lru_bx', 'grad_lru_lambda', 'grad_lru_w_out', 'grad_conf_w_in', 'grad_conf_b_in', 'grad_conf_dw_w', 'grad_conf_dw_b', 'grad_conf_ln_g', 'grad_conf_ln_b', 'grad_conf_w_out', 'grad_conf_b_out', 'grad_ffn_w_up', 'grad_ffn_conv_w', 'grad_ffn_conv_b', 'grad_ffn_w_down', 'grad_final_norm_g', 'delta_c_ctx', 'delta_ada_w', 'delta_ada_b', 'delta_norm_mix_g', 'delta_norm_ffn_g', 'delta_attn_w_qkv', 'delta_attn_w_o', 'delta_attn_sink', 'delta_lru_w_in', 'delta_lru_conv_w', 'delta_lru_conv_b', 'delta_lru_wa', 'delta_lru_ba', 'delta_lru_wx', 'delta_lru_bx', 'delta_lru_lambda', 'delta_lru_w_out', 'delta_conf_w_in', 'delta_conf_b_in', 'delta_conf_dw_w', 'delta_conf_dw_b', 'delta_conf_ln_g', 'delta_conf_ln_b', 'delta_conf_w_out', 'delta_conf_b_out', 'delta_ffn_w_up', 'delta_ffn_conv_w', 'delta_ffn_conv_b', 'delta_ffn_w_down', 'delta_final_norm_g', 'new_m_c_ctx', 'new_m_ada_w', 'new_m_ada_b', 'new_m_norm_mix_g', 'new_m_norm_ffn_g', 'new_m_attn_w_qkv', 'new_m_attn_w_o', 'new_m_attn_sink', 'new_m_lru_w_in', 'new_m_lru_conv_w', 'new_m_lru_conv_b', 'new_m_lru_wa', 'new_m_lru_ba', 'new_m_lru_wx', 'new_m_lru_bx', 'new_m_lru_lambda', 'new_m_lru_w_out', 'new_m_conf_w_in', 'new_m_conf_b_in', 'new_m_conf_dw_w', 'new_m_conf_dw_b', 'new_m_conf_ln_g', 'new_m_conf_ln_b', 'new_m_conf_w_out', 'new_m_conf_b_out', 'new_m_ffn_w_up', 'new_m_ffn_conv_w', 'new_m_ffn_conv_b', 'new_m_ffn_w_down', 'new_m_final_norm_g', 'new_v_c_ctx', 'new_v_ada_w', 'new_v_ada_b', 'new_v_norm_mix_g', 'new_v_norm_ffn_g', 'new_v_attn_w_qkv', 'new_v_attn_w_o', 'new_v_attn_sink', 'new_v_lru_w_in', 'new_v_lru_conv_w', 'new_v_lru_conv_b', 'new_v_lru_wa', 'new_v_lru_ba', 'new_v_lru_wx', 'new_v_lru_bx', 'new_v_lru_lambda', 'new_v_lru_w_out', 'new_v_conf_w_in', 'new_v_conf_b_in', 'new_v_conf_dw_w', 'new_v_conf_dw_b', 'new_v_conf_ln_g', 'new_v_conf_ln_b', 'new_v_conf_w_out', 'new_v_conf_b_out', 'new_v_ffn_w_up', 'new_v_ffn_conv_w', 'new_v_ffn_conv_b', 'new_v_ffn_w_down', 'new_v_final_norm_g']
TWIN_LEAF_KINDS = {'loss': 'loss', 'grad_x': 'grad_x', 'grad_c_ctx': 'grad_w', 'grad_ada_w': 'grad_w', 'grad_ada_b': 'grad_w', 'grad_norm_mix_g': 'grad_w', 'grad_norm_ffn_g': 'grad_w', 'grad_attn_w_qkv': 'grad_w', 'grad_attn_w_o': 'grad_w', 'grad_attn_sink': 'grad_w', 'grad_lru_w_in': 'grad_w', 'grad_lru_conv_w': 'grad_w', 'grad_lru_conv_b': 'grad_w', 'grad_lru_wa': 'grad_w', 'grad_lru_ba': 'grad_w', 'grad_lru_wx': 'grad_w', 'grad_lru_bx': 'grad_w', 'grad_lru_lambda': 'grad_w', 'grad_lru_w_out': 'grad_w', 'grad_conf_w_in': 'grad_w', 'grad_conf_b_in': 'grad_w', 'grad_conf_dw_w': 'grad_w', 'grad_conf_dw_b': 'grad_w', 'grad_conf_ln_g': 'grad_w', 'grad_conf_ln_b': 'grad_w', 'grad_conf_w_out': 'grad_w', 'grad_conf_b_out': 'grad_w', 'grad_ffn_w_up': 'grad_w', 'grad_ffn_conv_w': 'grad_w', 'grad_ffn_conv_b': 'grad_w', 'grad_ffn_w_down': 'grad_w', 'grad_final_norm_g': 'grad_w', 'delta_c_ctx': 'delta_w', 'delta_ada_w': 'delta_w', 'delta_ada_b': 'delta_w', 'delta_norm_mix_g': 'delta_w', 'delta_norm_ffn_g': 'delta_w', 'delta_attn_w_qkv': 'delta_w', 'delta_attn_w_o': 'delta_w', 'delta_attn_sink': 'delta_w', 'delta_lru_w_in': 'delta_w', 'delta_lru_conv_w': 'delta_w', 'delta_lru_conv_b': 'delta_w', 'delta_lru_wa': 'delta_w', 'delta_lru_ba': 'delta_w', 'delta_lru_wx': 'delta_w', 'delta_lru_bx': 'delta_w', 'delta_lru_lambda': 'delta_w', 'delta_lru_w_out': 'delta_w', 'delta_conf_w_in': 'delta_w', 'delta_conf_b_in': 'delta_w', 'delta_conf_dw_w': 'delta_w', 'delta_conf_dw_b': 'delta_w', 'delta_conf_ln_g': 'delta_w', 'delta_conf_ln_b': 'delta_w', 'delta_conf_w_out': 'delta_w', 'delta_conf_b_out': 'delta_w', 'delta_ffn_w_up': 'delta_w', 'delta_ffn_conv_w': 'delta_w', 'delta_ffn_conv_b': 'delta_w', 'delta_ffn_w_down': 'delta_w', 'delta_final_norm_g': 'delta_w', 'new_m_c_ctx': 'new_m', 'new_m_ada_w': 'new_m', 'new_m_ada_b': 'new_m', 'new_m_norm_mix_g': 'new_m', 'new_m_norm_ffn_g': 'new_m', 'new_m_attn_w_qkv': 'new_m', 'new_m_attn_w_o': 'new_m', 'new_m_attn_sink': 'new_m', 'new_m_lru_w_in': 'new_m', 'new_m_lru_conv_w': 'new_m', 'new_m_lru_conv_b': 'new_m', 'new_m_lru_wa': 'new_m', 'new_m_lru_ba': 'new_m', 'new_m_lru_wx': 'new_m', 'new_m_lru_bx': 'new_m', 'new_m_lru_lambda': 'new_m', 'new_m_lru_w_out': 'new_m', 'new_m_conf_w_in': 'new_m', 'new_m_conf_b_in': 'new_m', 'new_m_conf_dw_w': 'new_m', 'new_m_conf_dw_b': 'new_m', 'new_m_conf_ln_g': 'new_m', 'new_m_conf_ln_b': 'new_m', 'new_m_conf_w_out': 'new_m', 'new_m_conf_b_out': 'new_m', 'new_m_ffn_w_up': 'new_m', 'new_m_ffn_conv_w': 'new_m', 'new_m_ffn_conv_b': 'new_m', 'new_m_ffn_w_down': 'new_m', 'new_m_final_norm_g': 'new_m', 'new_v_c_ctx': 'new_v', 'new_v_ada_w': 'new_v', 'new_v_ada_b': 'new_v', 'new_v_norm_mix_g': 'new_v', 'new_v_norm_ffn_g': 'new_v', 'new_v_attn_w_qkv': 'new_v', 'new_v_attn_w_o': 'new_v', 'new_v_attn_sink': 'new_v', 'new_v_lru_w_in': 'new_v', 'new_v_lru_conv_w': 'new_v', 'new_v_lru_conv_b': 'new_v', 'new_v_lru_wa': 'new_v', 'new_v_lru_ba': 'new_v', 'new_v_lru_wx': 'new_v', 'new_v_lru_bx': 'new_v', 'new_v_lru_lambda': 'new_v', 'new_v_lru_w_out': 'new_v', 'new_v_conf_w_in': 'new_v', 'new_v_conf_b_in': 'new_v', 'new_v_conf_dw_w': 'new_v', 'new_v_conf_dw_b': 'new_v', 'new_v_conf_ln_g': 'new_v', 'new_v_conf_ln_b': 'new_v', 'new_v_conf_w_out': 'new_v', 'new_v_conf_b_out': 'new_v', 'new_v_ffn_w_up': 'new_v', 'new_v_ffn_conv_w': 'new_v', 'new_v_ffn_conv_b': 'new_v', 'new_v_ffn_w_down': 'new_v', 'new_v_final_norm_g': 'new_v'}


def _forward(args):
    return _fwd_reference(*[args[k] for k in FWD_PARAMS])


def _output_shape():
    def fwd():
        inp = _fwd_setup_inputs(0)
        return _fwd_reference(*[inp[k] for k in FWD_PARAMS])
    out = _jax.eval_shape(fwd)
    return out.shape, out.dtype

N_MICROBATCH = 1
ADAM_LR = 0.001
ADAM_B1 = 0.9
ADAM_B2 = 0.999
ADAM_EPS = 1e-08
ADAM_WD = 0.01
ADAM_STEP = 10
PER_EXAMPLE_BATCH_AXIS = {'x': 0, 'c': 0, 'ctx': 0, 'loss_target': 0}
SHARED_INPUTS = []
_WEIGHT_DTYPES = {'c_ctx': _jnp.float32, 'ada_w': _jnp.float32, 'ada_b': _jnp.float32, 'norm_mix_g': _jnp.float32, 'norm_ffn_g': _jnp.float32, 'attn_w_qkv': _jnp.float32, 'attn_w_o': _jnp.float32, 'attn_sink': _jnp.float32, 'lru_w_in': _jnp.float32, 'lru_conv_w': _jnp.float32, 'lru_conv_b': _jnp.float32, 'lru_wa': _jnp.float32, 'lru_ba': _jnp.float32, 'lru_wx': _jnp.float32, 'lru_bx': _jnp.float32, 'lru_lambda': _jnp.float32, 'lru_w_out': _jnp.float32, 'conf_w_in': _jnp.float32, 'conf_b_in': _jnp.float32, 'conf_dw_w': _jnp.float32, 'conf_dw_b': _jnp.float32, 'conf_ln_g': _jnp.float32, 'conf_ln_b': _jnp.float32, 'conf_w_out': _jnp.float32, 'conf_b_out': _jnp.float32, 'ffn_w_up': _jnp.float32, 'ffn_conv_w': _jnp.float32, 'ffn_conv_b': _jnp.float32, 'ffn_w_down': _jnp.float32, 'final_norm_g': _jnp.float32}
MOMENT_SCALE = {'c_ctx': 2.896821e-02, 'ada_w': 5.615006e-02, 'ada_b': 1.033627e-01, 'norm_mix_g': 3.812267e-02, 'norm_ffn_g': 3.498655e-02, 'attn_w_qkv': 1.277417e-02, 'attn_w_o': 1.432010e-02, 'attn_sink': 3.685972e-04, 'lru_w_in': 7.559066e-02, 'lru_conv_w': 5.744151e-02, 'lru_conv_b': 1.653164e-01, 'lru_wa': 6.012273e-03, 'lru_ba': 9.356881e-03, 'lru_wx': 1.216253e-02, 'lru_bx': 2.282726e-02, 'lru_lambda': 2.458820e-02, 'lru_w_out': 7.633900e-02, 'conf_w_in': 1.563055e-02, 'conf_b_in': 1.505256e-02, 'conf_dw_w': 2.031518e-02, 'conf_dw_b': 3.357442e-02, 'conf_ln_g': 2.352676e-02, 'conf_ln_b': 2.018369e-02, 'conf_w_out': 1.975320e-02, 'conf_b_out': 3.567975e-02, 'ffn_w_up': 1.530478e-02, 'ffn_conv_w': 1.551439e-02, 'ffn_conv_b': 1.370334e-02, 'ffn_w_down': 2.504467e-02, 'final_norm_g': 3.215780e+01}


def _to_microbatches(a, axis):
    t = _jnp.moveaxis(a, axis, 0)
    t = t.reshape((N_MICROBATCH, t.shape[0] // N_MICROBATCH) + t.shape[1:])
    return _jnp.moveaxis(t, 1, axis + 1)


def setup_inputs(seed: int = 0) -> dict:
    inp = _fwd_setup_inputs(seed)
    key = _jax.random.fold_in(_jax.random.key(seed), 7919)
    shape, _ = _output_shape()
    out = dict(inp)
    out["loss_target"] = _jax.random.normal(_jax.random.fold_in(key, 0), shape, _jnp.float32)
    for i, name in enumerate(TWIN_WEIGHTS):
        w = inp[name].astype(_jnp.float32)
        if MOMENT_SCALE is None:
            s = _jnp.sqrt(_jnp.mean(_jnp.square(w)) + 1e-30)
        else:
            s = MOMENT_SCALE[name]
        km, kv = _jax.random.split(_jax.random.fold_in(key, i + 1))
        out[name] = w
        out["m_" + name] = s * _jax.random.normal(km, w.shape, _jnp.float32)
        out["v_" + name] = (s * s) * _jax.random.uniform(kv, w.shape, _jnp.float32, 0.5, 1.5)
    if N_MICROBATCH > 1:
        for name, axis in PER_EXAMPLE_BATCH_AXIS.items():
            out[name] = _to_microbatches(out[name], axis)
    return {'x': out['x'], 'c': out['c'], 'ctx': out['ctx'], 'c_ctx': out['c_ctx'], 'ada_w': out['ada_w'], 'ada_b': out['ada_b'], 'norm_mix_g': out['norm_mix_g'], 'norm_ffn_g': out['norm_ffn_g'], 'attn_w_qkv': out['attn_w_qkv'], 'attn_w_o': out['attn_w_o'], 'attn_sink': out['attn_sink'], 'lru_w_in': out['lru_w_in'], 'lru_conv_w': out['lru_conv_w'], 'lru_conv_b': out['lru_conv_b'], 'lru_wa': out['lru_wa'], 'lru_ba': out['lru_ba'], 'lru_wx': out['lru_wx'], 'lru_bx': out['lru_bx'], 'lru_lambda': out['lru_lambda'], 'lru_w_out': out['lru_w_out'], 'conf_w_in': out['conf_w_in'], 'conf_b_in': out['conf_b_in'], 'conf_dw_w': out['conf_dw_w'], 'conf_dw_b': out['conf_dw_b'], 'conf_ln_g': out['conf_ln_g'], 'conf_ln_b': out['conf_ln_b'], 'conf_w_out': out['conf_w_out'], 'conf_b_out': out['conf_b_out'], 'ffn_w_up': out['ffn_w_up'], 'ffn_conv_w': out['ffn_conv_w'], 'ffn_conv_b': out['ffn_conv_b'], 'ffn_w_down': out['ffn_w_down'], 'final_norm_g': out['final_norm_g'], 'loss_target': out['loss_target'], 'm_c_ctx': out['m_c_ctx'], 'm_ada_w': out['m_ada_w'], 'm_ada_b': out['m_ada_b'], 'm_norm_mix_g': out['m_norm_mix_g'], 'm_norm_ffn_g': out['m_norm_ffn_g'], 'm_attn_w_qkv': out['m_attn_w_qkv'], 'm_attn_w_o': out['m_attn_w_o'], 'm_attn_sink': out['m_attn_sink'], 'm_lru_w_in': out['m_lru_w_in'], 'm_lru_conv_w': out['m_lru_conv_w'], 'm_lru_conv_b': out['m_lru_conv_b'], 'm_lru_wa': out['m_lru_wa'], 'm_lru_ba': out['m_lru_ba'], 'm_lru_wx': out['m_lru_wx'], 'm_lru_bx': out['m_lru_bx'], 'm_lru_lambda': out['m_lru_lambda'], 'm_lru_w_out': out['m_lru_w_out'], 'm_conf_w_in': out['m_conf_w_in'], 'm_conf_b_in': out['m_conf_b_in'], 'm_conf_dw_w': out['m_conf_dw_w'], 'm_conf_dw_b': out['m_conf_dw_b'], 'm_conf_ln_g': out['m_conf_ln_g'], 'm_conf_ln_b': out['m_conf_ln_b'], 'm_conf_w_out': out['m_conf_w_out'], 'm_conf_b_out': out['m_conf_b_out'], 'm_ffn_w_up': out['m_ffn_w_up'], 'm_ffn_conv_w': out['m_ffn_conv_w'], 'm_ffn_conv_b': out['m_ffn_conv_b'], 'm_ffn_w_down': out['m_ffn_w_down'], 'm_final_norm_g': out['m_final_norm_g'], 'v_c_ctx': out['v_c_ctx'], 'v_ada_w': out['v_ada_w'], 'v_ada_b': out['v_ada_b'], 'v_norm_mix_g': out['v_norm_mix_g'], 'v_norm_ffn_g': out['v_norm_ffn_g'], 'v_attn_w_qkv': out['v_attn_w_qkv'], 'v_attn_w_o': out['v_attn_w_o'], 'v_attn_sink': out['v_attn_sink'], 'v_lru_w_in': out['v_lru_w_in'], 'v_lru_conv_w': out['v_lru_conv_w'], 'v_lru_conv_b': out['v_lru_conv_b'], 'v_lru_wa': out['v_lru_wa'], 'v_lru_ba': out['v_lru_ba'], 'v_lru_wx': out['v_lru_wx'], 'v_lru_bx': out['v_lru_bx'], 'v_lru_lambda': out['v_lru_lambda'], 'v_lru_w_out': out['v_lru_w_out'], 'v_conf_w_in': out['v_conf_w_in'], 'v_conf_b_in': out['v_conf_b_in'], 'v_conf_dw_w': out['v_conf_dw_w'], 'v_conf_dw_b': out['v_conf_dw_b'], 'v_conf_ln_g': out['v_conf_ln_g'], 'v_conf_ln_b': out['v_conf_ln_b'], 'v_conf_w_out': out['v_conf_w_out'], 'v_conf_b_out': out['v_conf_b_out'], 'v_ffn_w_up': out['v_ffn_w_up'], 'v_ffn_conv_w': out['v_ffn_conv_w'], 'v_ffn_conv_b': out['v_ffn_conv_b'], 'v_ffn_w_down': out['v_ffn_w_down'], 'v_final_norm_g': out['v_final_norm_g']}


def _loss(weights, diff, rest, loss_target):
    with _jax.named_scope("forward"):
        args = {**rest, TWIN_DIFF_INPUT: diff, **{k: w.astype(_WEIGHT_DTYPES[k]) for k, w in weights.items()}}
        y = _forward(args)
    with _jax.named_scope("loss_head"):
        err = _jnp.square(y.astype(_jnp.float32) - loss_target)
        return 0.5 * _jnp.sum(_jnp.mean(err, axis=-1)) if err.ndim else 0.5 * err


def _adamw(w, g, m, v):
    m = ADAM_B1 * m + (1.0 - ADAM_B1) * g
    v = ADAM_B2 * v + (1.0 - ADAM_B2) * _jnp.square(g)
    m_hat = m / (1.0 - ADAM_B1 ** ADAM_STEP)
    v_hat = v / (1.0 - ADAM_B2 ** ADAM_STEP)
    delta = -ADAM_LR * (m_hat / (_jnp.sqrt(v_hat) + ADAM_EPS) + ADAM_WD * w)
    return delta, m, v


def reference(x, c, ctx, c_ctx, ada_w, ada_b, norm_mix_g, norm_ffn_g, attn_w_qkv, attn_w_o, attn_sink, lru_w_in, lru_conv_w, lru_conv_b, lru_wa, lru_ba, lru_wx, lru_bx, lru_lambda, lru_w_out, conf_w_in, conf_b_in, conf_dw_w, conf_dw_b, conf_ln_g, conf_ln_b, conf_w_out, conf_b_out, ffn_w_up, ffn_conv_w, ffn_conv_b, ffn_w_down, final_norm_g, loss_target, m_c_ctx, m_ada_w, m_ada_b, m_norm_mix_g, m_norm_ffn_g, m_attn_w_qkv, m_attn_w_o, m_attn_sink, m_lru_w_in, m_lru_conv_w, m_lru_conv_b, m_lru_wa, m_lru_ba, m_lru_wx, m_lru_bx, m_lru_lambda, m_lru_w_out, m_conf_w_in, m_conf_b_in, m_conf_dw_w, m_conf_dw_b, m_conf_ln_g, m_conf_ln_b, m_conf_w_out, m_conf_b_out, m_ffn_w_up, m_ffn_conv_w, m_ffn_conv_b, m_ffn_w_down, m_final_norm_g, v_c_ctx, v_ada_w, v_ada_b, v_norm_mix_g, v_norm_ffn_g, v_attn_w_qkv, v_attn_w_o, v_attn_sink, v_lru_w_in, v_lru_conv_w, v_lru_conv_b, v_lru_wa, v_lru_ba, v_lru_wx, v_lru_bx, v_lru_lambda, v_lru_w_out, v_conf_w_in, v_conf_b_in, v_conf_dw_w, v_conf_dw_b, v_conf_ln_g, v_conf_ln_b, v_conf_w_out, v_conf_b_out, v_ffn_w_up, v_ffn_conv_w, v_ffn_conv_b, v_ffn_w_down, v_final_norm_g):
    given = dict(x=x, c=c, ctx=ctx, c_ctx=c_ctx, ada_w=ada_w, ada_b=ada_b, norm_mix_g=norm_mix_g, norm_ffn_g=norm_ffn_g, attn_w_qkv=attn_w_qkv, attn_w_o=attn_w_o, attn_sink=attn_sink, lru_w_in=lru_w_in, lru_conv_w=lru_conv_w, lru_conv_b=lru_conv_b, lru_wa=lru_wa, lru_ba=lru_ba, lru_wx=lru_wx, lru_bx=lru_bx, lru_lambda=lru_lambda, lru_w_out=lru_w_out, conf_w_in=conf_w_in, conf_b_in=conf_b_in, conf_dw_w=conf_dw_w, conf_dw_b=conf_dw_b, conf_ln_g=conf_ln_g, conf_ln_b=conf_ln_b, conf_w_out=conf_w_out, conf_b_out=conf_b_out, ffn_w_up=ffn_w_up, ffn_conv_w=ffn_conv_w, ffn_conv_b=ffn_conv_b, ffn_w_down=ffn_w_down, final_norm_g=final_norm_g, loss_target=loss_target, m_c_ctx=m_c_ctx, m_ada_w=m_ada_w, m_ada_b=m_ada_b, m_norm_mix_g=m_norm_mix_g, m_norm_ffn_g=m_norm_ffn_g, m_attn_w_qkv=m_attn_w_qkv, m_attn_w_o=m_attn_w_o, m_attn_sink=m_attn_sink, m_lru_w_in=m_lru_w_in, m_lru_conv_w=m_lru_conv_w, m_lru_conv_b=m_lru_conv_b, m_lru_wa=m_lru_wa, m_lru_ba=m_lru_ba, m_lru_wx=m_lru_wx, m_lru_bx=m_lru_bx, m_lru_lambda=m_lru_lambda, m_lru_w_out=m_lru_w_out, m_conf_w_in=m_conf_w_in, m_conf_b_in=m_conf_b_in, m_conf_dw_w=m_conf_dw_w, m_conf_dw_b=m_conf_dw_b, m_conf_ln_g=m_conf_ln_g, m_conf_ln_b=m_conf_ln_b, m_conf_w_out=m_conf_w_out, m_conf_b_out=m_conf_b_out, m_ffn_w_up=m_ffn_w_up, m_ffn_conv_w=m_ffn_conv_w, m_ffn_conv_b=m_ffn_conv_b, m_ffn_w_down=m_ffn_w_down, m_final_norm_g=m_final_norm_g, v_c_ctx=v_c_ctx, v_ada_w=v_ada_w, v_ada_b=v_ada_b, v_norm_mix_g=v_norm_mix_g, v_norm_ffn_g=v_norm_ffn_g, v_attn_w_qkv=v_attn_w_qkv, v_attn_w_o=v_attn_w_o, v_attn_sink=v_attn_sink, v_lru_w_in=v_lru_w_in, v_lru_conv_w=v_lru_conv_w, v_lru_conv_b=v_lru_conv_b, v_lru_wa=v_lru_wa, v_lru_ba=v_lru_ba, v_lru_wx=v_lru_wx, v_lru_bx=v_lru_bx, v_lru_lambda=v_lru_lambda, v_lru_w_out=v_lru_w_out, v_conf_w_in=v_conf_w_in, v_conf_b_in=v_conf_b_in, v_conf_dw_w=v_conf_dw_w, v_conf_dw_b=v_conf_dw_b, v_conf_ln_g=v_conf_ln_g, v_conf_ln_b=v_conf_ln_b, v_conf_w_out=v_conf_w_out, v_conf_b_out=v_conf_b_out, v_ffn_w_up=v_ffn_w_up, v_ffn_conv_w=v_ffn_conv_w, v_ffn_conv_b=v_ffn_conv_b, v_ffn_w_down=v_ffn_w_down, v_final_norm_g=v_final_norm_g)
    weights = {n: given[n] for n in TWIN_WEIGHTS}
    shared = {n: given[n] for n in SHARED_INPUTS}
    per_example = {n: given[n] for n in ['x', 'c', 'ctx']}
    grad_fn = _jax.value_and_grad(_loss, argnums=(0, 1))

    def one_microbatch(ex, loss_target):
        ex = dict(ex)
        diff = ex.pop(TWIN_DIFF_INPUT)
        return grad_fn(weights, diff, {**shared, **ex}, loss_target)

    if N_MICROBATCH == 1:
        loss, (grad_w, grad_x) = one_microbatch(per_example, given["loss_target"])
    else:
        def body(carry, xs):
            loss_sum, grad_sum = carry
            l_k, (gw_k, gx_k) = one_microbatch(xs[0], xs[1])
            with _jax.named_scope("update"):
                return (loss_sum + l_k, _jax.tree.map(_jnp.add, grad_sum, gw_k)), gx_k

        init = (_jnp.zeros((), _jnp.float32), _jax.tree.map(_jnp.zeros_like, weights))
        (loss, grad_w), grad_x = _jax.lax.scan(body, init, (per_example, given["loss_target"]))
    with _jax.named_scope("update"):
        delta_w, new_m, new_v = {}, {}, {}
        for n in TWIN_WEIGHTS:
            delta_w[n], new_m[n], new_v[n] = _adamw(weights[n], grad_w[n], given["m_" + n], given["v_" + n])
    return (loss, grad_x, *[grad_w[n] for n in TWIN_WEIGHTS], *[delta_w[n] for n in TWIN_WEIGHTS],
            *[new_m[n] for n in TWIN_WEIGHTS], *[new_v[n] for n in TWIN_WEIGHTS])
```

```python
import functools
import math

import jax
import jax.numpy as jnp
from jax import lax
from jax.experimental import pallas as pl
from jax.experimental.pallas import tpu as pltpu

F32 = jnp.float32
BF16 = jnp.bfloat16

VMEM_LIMIT_BYTES = 56 * 1024 * 1024
LANE = 128
SUBLANE = 8

HEAD_DIM = 128
GQA_GROUP = 4
WINDOW_BLOCK = 128
GRID_W = 64
ROPE_THETA = 10000.0
ROPE_FREQS = HEAD_DIM // 4
LRU_BLOCK = 128
LRU_C = 8.0
NORM_EPS = 1e-6
NEG_INF = -1e30
CONV_HALO = 32

ADAM_LR = 0.001
ADAM_B1 = 0.9
ADAM_B2 = 0.999
ADAM_EPS = 1e-08
ADAM_WD = 0.01
ADAM_STEP = 10

MESH = pl.DeviceIdType.MESH
ANY = pl.BlockSpec(memory_space=pl.ANY)


def _cparams(sem):
    return pltpu.CompilerParams(dimension_semantics=sem, vmem_limit_bytes=VMEM_LIMIT_BYTES)


def _div_tile(n, cap, unit):
    if n <= unit:
        return n
    best = None
    t = unit
    while t <= min(n, cap):
        if n % t == 0:
            best = t
        t += unit
    assert best is not None, (n, cap, unit)
    return best


def _dot(a, b, dims):
    return lax.dot_general(a.astype(BF16), b.astype(BF16), (dims, ((), ())), preferred_element_type=F32)


def matmul_nn(a, b, diag=False, name="mm_nn"):
    M = a.shape[0]
    G, K, n = b.shape
    tm = _div_tile(M, 768, SUBLANE)
    tn = _div_tile(n, 1408, LANE)
    tk = K if K <= 2048 else _div_tile(K, 512, LANE)
    nk, nn = K // tk, n // tn
    assert a.shape[1] == (G * K if diag else K)

    def body(a_ref, b_ref, o_ref, acc_ref):
        r = pl.program_id(3)
        part = _dot(a_ref[...], b_ref[...], ((1,), (0,)))
        if nk == 1:
            o_ref[...] = part
        else:
            @pl.when(r == 0)
            def _():
                acc_ref[...] = part

            @pl.when(r > 0)
            def _():
                acc_ref[...] += part

            @pl.when(r == nk - 1)
            def _():
                o_ref[...] = acc_ref[...]

    a_map = (lambda i, g, j, r: (i, g * nk + r)) if diag else (lambda i, g, j, r: (i, r))
    return pl.pallas_call(
        body, name=name,
        grid=(M // tm, G, nn, nk),
        in_specs=[pl.BlockSpec((tm, tk), a_map),
                  pl.BlockSpec((None, tk, tn), lambda i, g, j, r: (g, r, j))],
        out_specs=pl.BlockSpec((tm, tn), lambda i, g, j, r: (i, g * nn + j)),
        out_shape=jax.ShapeDtypeStruct((M, G * n), F32),
        scratch_shapes=[pltpu.VMEM((tm, tn) if nk > 1 else (SUBLANE, LANE), F32)],
        compiler_params=_cparams(("parallel", "parallel", "parallel", "arbitrary")),
    )(a, b)


def matmul_nt(dy, b, diag=False, name="mm_nt"):
    M = dy.shape[0]
    G, K, n = b.shape
    assert dy.shape[1] == G * n
    tm = _div_tile(M, 768, SUBLANE)
    tko = _div_tile(K, 1024, LANE)
    tr = _div_tile(n, 1408, LANE)
    nr, nko = n // tr, K // tko
    steps = nr if diag else G * nr

    def body(dy_ref, b_ref, o_ref, acc_ref):
        r = pl.program_id(3)
        part = _dot(dy_ref[...], b_ref[...], ((1,), (1,)))
        if steps == 1:
            o_ref[...] = part
        else:
            @pl.when(r == 0)
            def _():
                acc_ref[...] = part

            @pl.when(r > 0)
            def _():
                acc_ref[...] += part

            @pl.when(r == steps - 1)
            def _():
                o_ref[...] = acc_ref[...]

    if diag:
        grid = (M // tm, G, nko, nr)
        dy_map = lambda i, g, kk, r: (i, g * nr + r)
        b_map = lambda i, g, kk, r: (g, kk, r)
        o_map = lambda i, g, kk, r: (i, g * nko + kk)
        out_cols = G * K
    else:
        grid = (M // tm, 1, nko, G * nr)
        dy_map = lambda i, g, kk, r: (i, r)
        b_map = lambda i, g, kk, r: (r // nr, kk, r % nr)
        o_map = lambda i, g, kk, r: (i, kk)
        out_cols = K
    return pl.pallas_call(
        body, name=name,
        grid=grid,
        in_specs=[pl.BlockSpec((tm, tr), dy_map), pl.BlockSpec((None, tko, tr), b_map)],
        out_specs=pl.BlockSpec((tm, tko), o_map),
        out_shape=jax.ShapeDtypeStruct((M, out_cols), F32),
        scratch_shapes=[pltpu.VMEM((tm, tko) if steps > 1 else (SUBLANE, LANE), F32)],
        compiler_params=_cparams(("parallel", "parallel", "parallel", "arbitrary")),
    )(dy, b)


def matmul_tn(a, dy, G, diag=False, out_dtype=F32, name="mm_tn"):
    M = a.shape[0]
    n = dy.shape[1] // G
    K = a.shape[1] // G if diag else a.shape[1]
    tm = _div_tile(M, 768, SUBLANE)
    tk = _div_tile(K, 1024, LANE)
    tn = _div_tile(n, 1408, LANE)
    nkb, nn, nm = K // tk, n // tn, M // tm

    def body(a_ref, dy_ref, o_ref, acc_ref):
        r = pl.program_id(3)
        part = _dot(a_ref[...], dy_ref[...], ((0,), (0,)))
        if nm == 1:
            o_ref[...] = part.astype(out_dtype)
        else:
            @pl.when(r == 0)
            def _():
                acc_ref[...] = part

            @pl.when(r > 0)
            def _():
                acc_ref[...] += part

            @pl.when(r == nm - 1)
            def _():
                o_ref[...] = acc_ref[...].astype(out_dtype)

    a_map = (lambda g, kk, j, r: (r, g * nkb + kk)) if diag else (lambda g, kk, j, r: (r, kk))
    return pl.pallas_call(
        body, name=name,
        grid=(G, nkb, nn, nm),
        in_specs=[pl.BlockSpec((tm, tk), a_map),
                  pl.BlockSpec((tm, tn), lambda g, kk, j, r: (r, g * nn + j))],
        out_specs=pl.BlockSpec((None, tk, tn), lambda g, kk, j, r: (g, kk, j)),
        out_shape=jax.ShapeDtypeStruct((G, K, n), out_dtype),
        scratch_shapes=[pltpu.VMEM((tk, tn) if nm > 1 else (SUBLANE, LANE), F32)],
        compiler_params=_cparams(("parallel", "parallel", "parallel", "arbitrary")),
    )(a, dy)


def linear(a, w, w_grad_proxy, diag=False, grad_dtype=F32, name="lin"):
    G = w.shape[0]

    @jax.custom_vjp
    def op(a, w, proxy):
        return matmul_nn(a, w, diag, name + "_fwd")

    def fwd(a, w, proxy):
        return matmul_nn(a, w, diag, name + "_fwd"), (a, w)

    def bwd(res, dy):
        a, w = res
        da = matmul_nt(dy, w, diag, name + "_dx")
        dw = matmul_tn(a, dy, G, diag, grad_dtype, name + "_dw")
        return da, None, dw

    op.defvjp(fwd, bwd)
    return op(a, w, w_grad_proxy)


def rowwise(name, f, blocks, params, pkinds, n_out, tb, cb=None, nograd=()):
    rows = blocks[0].shape[0]
    tb = min(tb, rows)
    assert rows % tb == 0
    nrb = rows // tb
    nb, npar = len(blocks), len(params)
    widths = [b.shape[1] for b in blocks]
    if cb is None:
        ncb = 1
        bw = widths
    else:
        assert all(w == widths[0] for w in widths) and widths[0] % cb == 0
        ncb = widths[0] // cb
        bw = [cb] * nb
    pw = []
    for p, kind in zip(params, pkinds):
        full = p.shape[-1]
        pw.append(full if cb is None or full != widths[0] else cb)
    for p, kind in zip(params, pkinds):
        assert p.shape[:-1] == ((1,) if kind == "vec" else (2, 1)), (name, p.shape, kind)

    shapes = jax.eval_shape(
        f, *[jax.ShapeDtypeStruct((tb, w), b.dtype) for w, b in zip(bw, blocks)],
        *[jax.ShapeDtypeStruct((1, w), p.dtype) for w, p in zip(pw, params)])
    shapes = tuple(shapes)
    out_sh, acc_sh = shapes[:n_out], shapes[n_out:]
    n_acc = len(acc_sh)
    for s in out_sh:
        assert s.shape[0] == tb
    for s in acc_sh:
        assert s.shape[0] == 1

    def blk_spec(w):
        return pl.BlockSpec((tb, w), lambda c, r: (r, c))

    def par_spec(w, kind, full):
        col = (lambda c: c) if (cb is not None and full == widths[0]) else (lambda c: 0)
        if kind == "vec":
            return pl.BlockSpec((1, w), lambda c, r: (0, col(c)))
        return pl.BlockSpec((None, 1, w), lambda c, r: (jnp.minimum(r, 1), 0, col(c)))

    blk_specs = [blk_spec(w) for w in bw]
    par_specs = [par_spec(w, k, p.shape[-1]) for w, k, p in zip(pw, pkinds, params)]
    sem = _cparams(("parallel", "arbitrary"))

    def run_fwd(blocks, params):
        def body(*refs):
            ins = [r[...] for r in refs[:nb + npar]]
            outs = refs[nb + npar:]
            res = f(*ins)
            r_id = pl.program_id(1)
            for o_ref, val in zip(outs[:n_out], res[:n_out]):
                o_ref[...] = val
            for o_ref, val in zip(outs[n_out:], res[n_out:]):
                @pl.when(r_id == 0)
                def _(o_ref=o_ref, val=val):
                    o_ref[...] = val

                @pl.when(r_id > 0)
                def _(o_ref=o_ref, val=val):
                    o_ref[...] += val

        out_specs = [blk_spec(s.shape[1]) for s in out_sh] + \
                    [pl.BlockSpec((1, s.shape[1]), lambda c, r: (0, c)) for s in acc_sh]
        out_shape = [jax.ShapeDtypeStruct((rows, s.shape[1] * ncb), s.dtype) for s in out_sh] + \
                    [jax.ShapeDtypeStruct((1, s.shape[1] * ncb), s.dtype) for s in acc_sh]
        return tuple(pl.pallas_call(
            body, name=name + "_fwd", grid=(ncb, nrb),
            in_specs=blk_specs + par_specs, out_specs=out_specs, out_shape=out_shape,
            compiler_params=sem)(*blocks, *params))

    def run_bwd(blocks, params, cts):
        d_outs, d_accs = list(cts[:n_out]), list(cts[n_out:])
        want = [i for i in range(nb) if i not in nograd]

        def body(*refs):
            k = nb + npar
            ins = [r[...] for r in refs[:k]]
            ct = tuple(r[...] for r in refs[k:k + n_out + n_acc])
            outs = refs[k + n_out + n_acc:]
            _, vjp = jax.vjp(lambda *a: tuple(f(*a)), *ins)
            grads = vjp(ct)
            r_id = pl.program_id(1)
            for o_ref, i in zip(outs[:len(want)], want):
                o_ref[...] = grads[i].astype(o_ref.dtype)
            for o_ref, g, kind in zip(outs[len(want):], grads[nb:], pkinds):
                first = (r_id == 0) if kind == "vec" else (r_id <= 1)

                @pl.when(first)
                def _(o_ref=o_ref, g=g):
                    o_ref[...] = g.astype(o_ref.dtype)

                @pl.when(jnp.logical_not(first))
                def _(o_ref=o_ref, g=g):
                    o_ref[...] += g.astype(o_ref.dtype)

        ct_specs = [blk_spec(s.shape[1]) for s in out_sh] + \
                   [pl.BlockSpec((1, s.shape[1]), lambda c, r: (0, c)) for s in acc_sh]
        out_specs = [blk_specs[i] for i in want] + par_specs
        out_shape = [jax.ShapeDtypeStruct(blocks[i].shape, blocks[i].dtype) for i in want] + \
                    [jax.ShapeDtypeStruct(p.shape, p.dtype) for p in params]
        res = pl.pallas_call(
            body, name=name + "_bwd", grid=(ncb, nrb),
            in_specs=blk_specs + par_specs + ct_specs, out_specs=out_specs, out_shape=out_shape,
            compiler_params=sem)(*blocks, *params, *d_outs, *d_accs)
        d_blocks = [None] * nb
        for i, g in zip(want, res[:len(want)]):
            d_blocks[i] = g
        return tuple(d_blocks), tuple(res[len(want):])

    @jax.custom_vjp
    def op(blocks, params):
        return run_fwd(blocks, params)

    def op_fwd(blocks, params):
        return run_fwd(blocks, params), (blocks, params)

    def op_bwd(res, cts):
        return run_bwd(res[0], res[1], cts)

    op.defvjp(op_fwd, op_bwd)
    return op(tuple(blocks), tuple(params))


def _conv_window(prev_ref, cur_ref, next_ref, win_ref, tb):
    win_ref[pl.ds(0, CONV_HALO), :] = prev_ref[...]
    win_ref[pl.ds(CONV_HALO, tb), :] = cur_ref[...]
    win_ref[pl.ds(CONV_HALO + tb, CONV_HALO), :] = next_ref[...]


def _conv_tap_mask(i, tb, k, pad_left, rows, n_ctx):
    t = i * tb + lax.broadcasted_iota(jnp.int32, (tb, 1), 0)
    s = t + (k - pad_left)
    return (s >= 0) & (s < rows) & ((s >= n_ctx) == (t >= n_ctx))


def _conv_specs(rows, tb, cb):
    hb = tb // CONV_HALO
    last = rows // CONV_HALO - 1
    prev = pl.BlockSpec((CONV_HALO, cb), lambda c, i: (jnp.maximum(i * hb - 1, 0), c))
    cur = pl.BlockSpec((tb, cb), lambda c, i: (i, c))
    nxt = pl.BlockSpec((CONV_HALO, cb), lambda c, i: (jnp.minimum((i + 1) * hb, last), c))
    return prev, cur, nxt


def _dwconv_apply(x, w, b, pad_left, n_ctx, name):
    rows, C = x.shape
    K = w.shape[0]
    tb = 256
    cb = _div_tile(C, 512, LANE)
    assert rows % tb == 0 and K - 1 <= CONV_HALO

    def body(prev_ref, cur_ref, next_ref, w_ref, b_ref, o_ref, win_ref):
        i = pl.program_id(1)
        _conv_window(prev_ref, cur_ref, next_ref, win_ref, tb)
        acc = jnp.broadcast_to(b_ref[...], (tb, cb))
        for k in range(K):
            tap = win_ref[pl.ds(CONV_HALO + k - pad_left, tb), :]
            mask = _conv_tap_mask(i, tb, k, pad_left, rows, n_ctx)
            acc = acc + jnp.where(mask, tap, 0.0) * w_ref[pl.ds(k, 1), :]
        o_ref[...] = acc

    prev, cur, nxt = _conv_specs(rows, tb, cb)
    return pl.pallas_call(
        body, name=name, grid=(C // cb, rows // tb),
        in_specs=[prev, cur, nxt, pl.BlockSpec((K, cb), lambda c, i: (0, c)),
                  pl.BlockSpec((1, cb), lambda c, i: (0, c))],
        out_specs=pl.BlockSpec((tb, cb), lambda c, i: (i, c)),
        out_shape=jax.ShapeDtypeStruct((rows, C), F32),
        scratch_shapes=[pltpu.VMEM((tb + 2 * CONV_HALO, cb), F32)],
        compiler_params=_cparams(("parallel", "arbitrary")),
    )(x, x, x, w, b)


def _dwconv_wgrad(x, dy, K, pad_left, n_ctx, name):
    rows, C = x.shape
    tb = 256
    cb = _div_tile(C, 512, LANE)

    def body(prev_ref, cur_ref, next_ref, dy_ref, dw_ref, db_ref, win_ref):
        i = pl.program_id(1)
        _conv_window(prev_ref, cur_ref, next_ref, win_ref, tb)
        dy = dy_ref[...]

        @pl.when(i == 0)
        def _():
            dw_ref[...] = jnp.zeros_like(dw_ref)
            db_ref[...] = jnp.zeros_like(db_ref)

        db_ref[...] += jnp.sum(dy, axis=0, keepdims=True)
        for k in range(K):
            tap = win_ref[pl.ds(CONV_HALO + k - pad_left, tb), :]
            mask = _conv_tap_mask(i, tb, k, pad_left, rows, n_ctx)
            dw_ref[pl.ds(k, 1), :] += jnp.sum(jnp.where(mask, tap, 0.0) * dy, axis=0, keepdims=True)

    prev, cur, nxt = _conv_specs(rows, tb, cb)
    return pl.pallas_call(
        body, name=name, grid=(C // cb, rows // tb),
        in_specs=[prev, cur, nxt, pl.BlockSpec((tb, cb), lambda c, i: (i, c))],
        out_specs=[pl.BlockSpec((K, cb), lambda c, i: (0, c)), pl.BlockSpec((1, cb), lambda c, i: (0, c))],
        out_shape=[jax.ShapeDtypeStruct((K, C), F32), jax.ShapeDtypeStruct((1, C), F32)],
        scratch_shapes=[pltpu.VMEM((tb + 2 * CONV_HALO, cb), F32)],
        compiler_params=_cparams(("parallel", "arbitrary")),
    )(x, x, x, dy)


def dwconv(x, w, b, pad_left, n_ctx, name):
    K = w.shape[0]

    @jax.custom_vjp
    def op(x, w, b):
        return _dwconv_apply(x, w, b, pad_left, n_ctx, name + "_fwd")

    def fwd(x, w, b):
        return _dwconv_apply(x, w, b, pad_left, n_ctx, name + "_fwd"), (x, w)

    def bwd(res, dy):
        x, w = res
        dx = _dwconv_apply(dy, w[::-1], jnp.zeros((1, w.shape[1]), F32), K - 1 - pad_left, n_ctx, name + "_dx")
        dw, db = _dwconv_wgrad(x, dy, K, pad_left, n_ctx, name + "_dw")
        return dx, dw, db

    op.defvjp(fwd, bwd)
    return op(x, w, b)


def _block_scan(a, b, reverse):
    tb = a.shape[0]
    row = lax.broadcasted_iota(jnp.int32, (tb, 1), 0)
    s = 1
    while s < tb:
        if reverse:
            keep = row < tb - s
            a_sh = pltpu.roll(a, tb - s, 0)
            b_sh = pltpu.roll(b, tb - s, 0)
        else:
            keep = row >= s
            a_sh = pltpu.roll(a, s, 0)
            b_sh = pltpu.roll(b, s, 0)
        b = jnp.where(keep, a * b_sh + b, b)
        a = jnp.where(keep, a * a_sh, a)
        s *= 2
    return a, b


def _shift_in(h, carry, reverse):
    tb = h.shape[0]
    row = lax.broadcasted_iota(jnp.int32, (tb, 1), 0)
    if reverse:
        return jnp.where(row == tb - 1, carry, pltpu.roll(h, tb - 1, 0))
    return jnp.where(row == 0, carry, pltpu.roll(h, 1, 0))


def _scan_maps(nrb, rot, reverse):
    def phys(c, i):
        q = (nrb - 1 - i) if reverse else i
        return (lax.rem(q + rot, nrb), c)
    return phys


def _scan_fwd(a, b, rot, reverse, name):
    rows, C = a.shape
    tb = 256
    cb = _div_tile(C, 512, LANE)
    nrb = rows // tb
    last_row = 0 if reverse else tb - 1

    def body(a_ref, b_ref, h_ref, hp_ref, carry_ref):
        i = pl.program_id(1)

        @pl.when(i == 0)
        def _():
            carry_ref[...] = jnp.zeros_like(carry_ref)

        carry = carry_ref[pl.ds(0, 1), :]
        A, B = _block_scan(a_ref[...], b_ref[...], reverse)
        h = A * carry + B
        h_ref[...] = h
        hp_ref[...] = _shift_in(h, carry, reverse)
        carry_ref[pl.ds(0, 1), :] = h[last_row:last_row + 1, :]

    spec = pl.BlockSpec((tb, cb), _scan_maps(nrb, rot, reverse))
    return pl.pallas_call(
        body, name=name, grid=(C // cb, nrb),
        in_specs=[spec, spec], out_specs=[spec, spec],
        out_shape=[jax.ShapeDtypeStruct((rows, C), F32)] * 2,
        scratch_shapes=[pltpu.VMEM((SUBLANE, cb), F32)],
        compiler_params=_cparams(("parallel", "arbitrary")),
    )(a, b)


def _scan_bwd(a, dh, h_prev, rot, reverse, name):
    rows, C = a.shape
    tb = 256
    cb = _div_tile(C, 512, LANE)
    nrb = rows // tb
    adj = not reverse
    last_row = 0 if adj else tb - 1

    def body(a_ref, dh_ref, hp_ref, da_ref, db_ref, carry_ref):
        i = pl.program_id(1)

        @pl.when(i == 0)
        def _():
            carry_ref[...] = jnp.zeros_like(carry_ref)

        carry = carry_ref[pl.ds(0, 1), :]
        a = a_ref[...]
        dh = dh_ref[...]
        A, B = _block_scan(a, a * dh, adj)
        u = A * carry + B
        g = dh + _shift_in(u, carry, adj)
        db_ref[...] = g
        da_ref[...] = g * hp_ref[...]
        carry_ref[pl.ds(0, 1), :] = u[last_row:last_row + 1, :]

    spec = pl.BlockSpec((tb, cb), _scan_maps(nrb, rot, adj))
    return pl.pallas_call(
        body, name=name, grid=(C // cb, nrb),
        in_specs=[spec, spec, spec], out_specs=[spec, spec],
        out_shape=[jax.ShapeDtypeStruct((rows, C), F32)] * 2,
        scratch_shapes=[pltpu.VMEM((SUBLANE, cb), F32)],
        compiler_params=_cparams(("parallel", "arbitrary")),
    )(a, dh, h_prev)


def linear_scan(a, b, rot, reverse, name):
    @jax.custom_vjp
    def op(a, b):
        return _scan_fwd(a, b, rot, reverse, name + "_fwd")[0]

    def fwd(a, b):
        h, hp = _scan_fwd(a, b, rot, reverse, name + "_fwd")
        return h, (a, hp)

    def bwd(res, dh):
        a, hp = res
        da, db = _scan_bwd(a, dh, hp, rot, reverse, name + "_bwd")
        return da, db

    op.defvjp(fwd, bwd)
    return op(a, b)


def rope_tables(n_ctx, n_lat):
    t = jnp.arange(n_lat)
    pos = jnp.stack([t // GRID_W, t % GRID_W], axis=-1).astype(F32)
    freq = ROPE_THETA ** (-jnp.arange(ROPE_FREQS, dtype=F32) / ROPE_FREQS)
    ang = pos[:, :, None] * freq
    cos, sin = jnp.cos(ang), jnp.sin(ang)
    c = jnp.concatenate([cos[:, 0], cos[:, 0], cos[:, 1], cos[:, 1]], axis=-1)
    s = jnp.concatenate([-sin[:, 0], sin[:, 0], -sin[:, 1], sin[:, 1]], axis=-1)
    c = jnp.concatenate([jnp.ones((n_ctx, HEAD_DIM), F32), c], axis=0)
    s = jnp.concatenate([jnp.zeros((n_ctx, HEAD_DIM), F32), s], axis=0)
    return c, s


def _rope_apply(qkv, c_tab, s_tab, n_rot_heads, name):
    rows, cols = qkv.shape
    tb = _div_tile(rows, 768, SUBLANE)

    def body(x_ref, c_ref, s_ref, o_ref):
        x = x_ref[...]
        lane = lax.broadcasted_iota(jnp.int32, x.shape, 1)
        swapped = jnp.where((lane & 63) < 32, pltpu.roll(x, HEAD_DIM - 32, 1), pltpu.roll(x, 32, 1))
        roped = x * c_ref[...] + swapped * s_ref[...]
        o_ref[...] = jnp.where(pl.program_id(1) < n_rot_heads, roped, x)

    tab = pl.BlockSpec((tb, HEAD_DIM), lambda i, j: (i, 0))
    blk = pl.BlockSpec((tb, HEAD_DIM), lambda i, j: (i, j))
    return pl.pallas_call(
        body, name=name, grid=(rows // tb, cols // HEAD_DIM),
        in_specs=[blk, tab, tab], out_specs=blk,
        out_shape=jax.ShapeDtypeStruct((rows, cols), F32),
        compiler_params=_cparams(("parallel", "arbitrary")),
    )(qkv, c_tab, s_tab)


def rope(qkv, c_tab, s_tab, n_rot_heads, name):
    @jax.custom_vjp
    def op(qkv):
        return _rope_apply(qkv, c_tab, s_tab, n_rot_heads, name + "_fwd")

    def fwd(qkv):
        return _rope_apply(qkv, c_tab, s_tab, n_rot_heads, name + "_fwd"), None

    def bwd(_, d):
        return (_rope_apply(d, c_tab, -s_tab, n_rot_heads, name + "_bwd"),)

    op.defvjp(fwd, bwd)
    return op(qkv)


def _attn_in_specs(H, KV, n_ctx, nqb):
    ncb = n_ctx // WINDOW_BLOCK
    G = GQA_GROUP

    def loc(delta, col0):
        return pl.BlockSpec((WINDOW_BLOCK, HEAD_DIM),
                            lambda g, i: (jnp.clip(i + delta, ncb, nqb - 1), col0 + g))

    q = pl.BlockSpec((WINDOW_BLOCK, G * HEAD_DIM), lambda g, i: (i, g))
    kc = pl.BlockSpec((n_ctx, HEAD_DIM), lambda g, i: (0, H + g))
    vc = pl.BlockSpec((n_ctx, HEAD_DIM), lambda g, i: (0, H + KV + g))
    sink = pl.BlockSpec((None, G * WINDOW_BLOCK, 1), lambda g, i: (g, 0, 0))
    return [q, kc, loc(-1, H), loc(0, H), loc(1, H), vc, loc(-1, H + KV), loc(0, H + KV), loc(1, H + KV), sink]


def _stack_heads(x):
    return jnp.concatenate([x[:, h * HEAD_DIM:(h + 1) * HEAD_DIM] for h in range(GQA_GROUP)], axis=0)


def _attn_probs(i, q_ref, kc_ref, kp_ref, kcur_ref, kn_ref, sink_ref, n_ctx, n_lat_blocks):
    ncb = n_ctx // WINDOW_BLOCK
    nq = GQA_GROUP * WINDOW_BLOCK
    qs = _stack_heads(q_ref[...])
    k = jnp.concatenate([kc_ref[...], kp_ref[...], kcur_ref[...], kn_ref[...]], axis=0)
    nk = k.shape[0]
    s = _dot(qs, k, ((1,), (1,))) * (HEAD_DIM ** -0.5)
    r = lax.broadcasted_iota(jnp.int32, (nq, nk), 0) & (WINDOW_BLOCK - 1)
    col = lax.broadcasted_iota(jnp.int32, (nq, nk), 1)
    c2 = col - n_ctx
    blk = c2 >> 7
    rk = c2 & (WINDOW_BLOCK - 1)
    n = i - ncb
    local = ((blk == 1)
             | ((blk == 0) & (rk >= r) & (n >= 1))
             | ((blk == 2) & (rk <= r) & (n + 1 < n_lat_blocks)))
    valid = (col < n_ctx) | ((col >= n_ctx) & local & (i >= ncb))
    s = jnp.where(valid, s, NEG_INF)
    sk = sink_ref[...]
    m = jnp.maximum(jnp.max(s, axis=1, keepdims=True), sk)
    e = jnp.exp(s - m)
    es = jnp.exp(sk - m)
    inv = 1.0 / (jnp.sum(e, axis=1, keepdims=True) + es)
    return qs, k, e * inv, es * inv


def _attn_fwd(qkv, sink_col, H, KV, n_ctx, name):
    rows = qkv.shape[0]
    nqb = rows // WINDOW_BLOCK
    nlb = (rows - n_ctx) // WINDOW_BLOCK
    G = GQA_GROUP

    def body(q_ref, kc_ref, kp_ref, kcur_ref, kn_ref, vc_ref, vp_ref, vcur_ref, vn_ref, sink_ref, o_ref):
        i = pl.program_id(1)
        _, _, p, _ = _attn_probs(i, q_ref, kc_ref, kp_ref, kcur_ref, kn_ref, sink_ref, n_ctx, nlb)
        v = jnp.concatenate([vc_ref[...], vp_ref[...], vcur_ref[...], vn_ref[...]], axis=0)
        o = _dot(p, v, ((1,), (0,)))
        for h in range(G):
            o_ref[:, h * HEAD_DIM:(h + 1) * HEAD_DIM] = o[h * WINDOW_BLOCK:(h + 1) * WINDOW_BLOCK, :]

    return pl.pallas_call(
        body, name=name, grid=(KV, nqb),
        in_specs=_attn_in_specs(H, KV, n_ctx, nqb),
        out_specs=pl.BlockSpec((WINDOW_BLOCK, G * HEAD_DIM), lambda g, i: (i, g)),
        out_shape=jax.ShapeDtypeStruct((rows, H * HEAD_DIM), F32),
        compiler_params=_cparams(("parallel", "arbitrary")),
    )(*([qkv] * 9), sink_col)


def _attn_bwd(qkv, sink_col, o, do, H, KV, n_ctx, name):
    rows = qkv.shape[0]
    nqb = rows // WINDOW_BLOCK
    nlb = (rows - n_ctx) // WINDOW_BLOCK
    G = GQA_GROUP
    WB = WINDOW_BLOCK

    def body(q_ref, kc_ref, kp_ref, kcur_ref, kn_ref, vc_ref, vp_ref, vcur_ref, vn_ref, sink_ref, o_ref, do_ref,
             dq_ref, dkc_ref, dvc_ref, dkp_ref, dkcur_ref, dkn_ref, dvp_ref, dvcur_ref, dvn_ref, dsink_ref):
        i = pl.program_id(1)
        qs, k, p, ps = _attn_probs(i, q_ref, kc_ref, kp_ref, kcur_ref, kn_ref, sink_ref, n_ctx, nlb)
        v = jnp.concatenate([vc_ref[...], vp_ref[...], vcur_ref[...], vn_ref[...]], axis=0)
        do_s = _stack_heads(do_ref[...])
        o_s = _stack_heads(o_ref[...])
        delta = jnp.sum(do_s * o_s, axis=1, keepdims=True)
        dp = _dot(do_s, v, ((1,), (1,)))
        ds = p * (dp - delta) * (HEAD_DIM ** -0.5)
        dq = _dot(ds, k, ((1,), (0,)))
        dk = _dot(ds, qs, ((0,), (0,)))
        dv = _dot(p, do_s, ((0,), (0,)))
        for h in range(G):
            dq_ref[:, h * HEAD_DIM:(h + 1) * HEAD_DIM] = dq[h * WB:(h + 1) * WB, :]

        @pl.when(i == 0)
        def _():
            dkc_ref[...] = jnp.zeros_like(dkc_ref)
            dvc_ref[...] = jnp.zeros_like(dvc_ref)
            dsink_ref[...] = jnp.zeros_like(dsink_ref)

        dkc_ref[...] += dk[:n_ctx]
        dvc_ref[...] += dv[:n_ctx]
        dsink_ref[...] += -ps * delta
        for j, (dk_ref, dv_ref) in enumerate(((dkp_ref, dvp_ref), (dkcur_ref, dvcur_ref), (dkn_ref, dvn_ref))):
            dk_ref[...] = dk[n_ctx + j * WB:n_ctx + (j + 1) * WB]
            dv_ref[...] = dv[n_ctx + j * WB:n_ctx + (j + 1) * WB]

    qblk = pl.BlockSpec((WB, G * HEAD_DIM), lambda g, i: (i, g))
    ctx = pl.BlockSpec((n_ctx, HEAD_DIM), lambda g, i: (0, g))
    piece = pl.BlockSpec((WB, HEAD_DIM), lambda g, i: (i, g))
    sink = pl.BlockSpec((None, G * WB, 1), lambda g, i: (g, 0, 0))
    kv_shape = jax.ShapeDtypeStruct((rows, KV * HEAD_DIM), F32)
    ctx_shape = jax.ShapeDtypeStruct((n_ctx, KV * HEAD_DIM), F32)
    return pl.pallas_call(
        body, name=name, grid=(KV, nqb),
        in_specs=_attn_in_specs(H, KV, n_ctx, nqb) + [qblk, qblk],
        out_specs=[qblk, ctx, ctx] + [piece] * 6 + [sink],
        out_shape=[jax.ShapeDtypeStruct((rows, H * HEAD_DIM), F32), ctx_shape, ctx_shape] + [kv_shape] * 6 +
                  [jax.ShapeDtypeStruct(sink_col.shape, F32)],
        compiler_params=_cparams(("parallel", "arbitrary")),
    )(*([qkv] * 9), sink_col, o, do)


def _shift_blocks(x, n_ctx, delta):
    lat = x[n_ctx:]
    z = jnp.zeros((WINDOW_BLOCK, x.shape[1]), x.dtype)
    if delta == 1:
        lat = jnp.concatenate([z, lat[:-WINDOW_BLOCK]], axis=0)
    elif delta == -1:
        lat = jnp.concatenate([lat[WINDOW_BLOCK:], z], axis=0)
    return jnp.concatenate([jnp.zeros((n_ctx, x.shape[1]), x.dtype), lat], axis=0)


def attention(qkv, sink_col, H, KV, n_ctx, name):
    @jax.custom_vjp
    def op(qkv, sink_col):
        return _attn_fwd(qkv, sink_col, H, KV, n_ctx, name + "_fwd")

    def fwd(qkv, sink_col):
        o = _attn_fwd(qkv, sink_col, H, KV, n_ctx, name + "_fwd")
        return o, (qkv, sink_col, o)

    def bwd(res, do):
        qkv, sink_col, o = res
        dq, dkc, dvc, dkp, dkcur, dkn, dvp, dvcur, dvn, dsink = _attn_bwd(qkv, sink_col, o, do, H, KV, n_ctx,
                                                                          name + "_bwd")

        def gather_pieces(prev, cur, nxt, ctx):
            pad = jnp.concatenate([ctx, jnp.zeros((qkv.shape[0] - n_ctx, ctx.shape[1]), F32)], axis=0)
            return rowwise(name + "_kvsum", lambda a, b, c, d: (a + b + c + d,),
                           [cur, _shift_blocks(prev, n_ctx, -1), _shift_blocks(nxt, n_ctx, 1), pad], [], [], 1, 256)[0]

        dk = gather_pieces(dkp, dkcur, dkn, dkc)
        dv = gather_pieces(dvp, dvcur, dvn, dvc)
        return jnp.concatenate([dq, dk, dv], axis=1), dsink

    op.defvjp(fwd, bwd)
    return op(qkv, sink_col)


ROW_BLOCK = 256


def _rms_norm(x, g):
    return (x * lax.rsqrt(jnp.mean(x * x, axis=-1, keepdims=True) + NORM_EPS)) * g


def _modulate_f(x, g, shift, scale):
    return (_rms_norm(x, g) * (1.0 + scale) + shift,)


def _expm1(x):
    series = x * (1 + x / 2 * (1 + x / 3 * (1 + x / 4 * (1 + x / 5 * (1 + x / 6)))))
    return jnp.where(jnp.abs(x) < 0.1, series, jnp.exp(x) - 1.0)


def _lru_gates_f(ra, rx, uc, ba, bx, sp):
    r = jax.nn.sigmoid(ra + ba)
    ig = jax.nn.sigmoid(rx + bx)
    log_a = -LRU_C * r * sp
    return jnp.exp(log_a), jnp.sqrt(-_expm1(2.0 * log_a)) * (ig * uc)


def _ln_silu_f(z, g, b):
    mu = jnp.mean(z, axis=-1, keepdims=True)
    var = jnp.mean(jnp.square(z - mu), axis=-1, keepdims=True)
    return (jax.nn.silu((z - mu) * lax.rsqrt(var + NORM_EPS) * g + b),)


def _loss_f(x, target, g):
    err = _rms_norm(x, g) - target
    return (jnp.sum(0.5 * err * err, axis=0, keepdims=True) / x.shape[1],)


def _modulate(x, g, shift, scale):
    return rowwise("modulate", _modulate_f, [x], [g, shift, scale], ["vec", "seg", "seg"], 1, ROW_BLOCK)[0]


def _residual(x, y, gate):
    return rowwise("residual", lambda x, y, g: (x + g * y,), [x, y], [gate], ["seg"], 1, ROW_BLOCK)[0]


def _split_linear(h, w, px, name):
    return (linear(h, w[0:2], px[0:2], grad_dtype=BF16, name=name + "_a"),
            linear(h, w[2:4], px[2:4], grad_dtype=BF16, name=name + "_b"))


def _attention_mixer(h, P, PX, j, n_ctx, tabs):
    H = P["attn_w_o"][j].shape[1] // HEAD_DIM
    KV = H // GQA_GROUP
    qkv = linear(h, P["attn_w_qkv"][j], PX["attn_w_qkv"][j], grad_dtype=BF16, name="attn_qkv")
    qkv = rope(qkv, tabs[0], tabs[1], H + KV, "rope")
    sink_col = jnp.repeat(P["attn_sink"][j].reshape(KV, GQA_GROUP), WINDOW_BLOCK, axis=1)[..., None]
    o = attention(qkv, sink_col, H, KV, n_ctx, "attn")
    return linear(o, P["attn_w_o"][j], PX["attn_w_o"][j], grad_dtype=BF16, name="attn_o"), None


def _rglru_mixer(h, P, PX, j, n_ctx):
    gate, xb = _split_linear(h, P["lru_w_in"][j], PX["lru_w_in"][j], "lru_in")
    R = xb.shape[1]
    cb = _div_tile(R, 512, LANE)
    sp = jax.nn.softplus(-P["lru_lambda"][j])
    hs = []
    for d in range(2):
        K = P["lru_conv_w"][j].shape[1]
        uc = dwconv(xb, P["lru_conv_w"][j][d], P["lru_conv_b"][j][d][None], 0 if d == 1 else K - 1, n_ctx,
                    "lru_conv")
        ra = linear(uc, P["lru_wa"][j][d], P["lru_wa"][j][d], diag=True, name="lru_wa")
        rx = linear(uc, P["lru_wx"][j][d], P["lru_wx"][j][d], diag=True, name="lru_wx")
        a, bt = rowwise("lru_gates", _lru_gates_f, [ra, rx, uc],
                        [P["lru_ba"][j][d][None], P["lru_bx"][j][d][None], sp[d][None]], ["vec"] * 3, 2,
                        ROW_BLOCK, cb=cb)
        hs.append(linear_scan(a, bt, d, d == 1, "lru_scan"))
    y_in = rowwise("lru_gelu", lambda g, h0, h1: (jax.nn.gelu(g) * (h0 + h1),), [gate, hs[0], hs[1]], [], [], 1,
                   ROW_BLOCK, cb=cb)[0]
    return linear(y_in, P["lru_w_out"][j], PX["lru_w_out"][j], grad_dtype=BF16, name="lru_out"), None


def _conformer_mixer(h, P, PX, j, n_ctx):
    z1, z2 = _split_linear(h, P["conf_w_in"][j], PX["conf_w_in"][j], "conf_in")
    Dm = z1.shape[1]
    b_in = P["conf_b_in"][j]
    z = rowwise("conf_glu", lambda a, b, ba, bb: ((a + ba) * jax.nn.sigmoid(b + bb),), [z1, z2],
                [b_in[None, :Dm], b_in[None, Dm:]], ["vec", "vec"], 1, ROW_BLOCK, cb=_div_tile(Dm, 512, LANE))[0]
    K = P["conf_dw_w"][j].shape[0]
    zc = dwconv(z, P["conf_dw_w"][j], P["conf_dw_b"][j][None], K // 2, n_ctx, "conf_conv")
    zs = rowwise("conf_ln_silu", _ln_silu_f, [zc], [P["conf_ln_g"][j][None], P["conf_ln_b"][j][None]],
                 ["vec", "vec"], 1, ROW_BLOCK)[0]
    y = linear(zs, P["conf_w_out"][j], PX["conf_w_out"][j], grad_dtype=BF16, name="conf_out")
    return y, P["conf_b_out"][j][None]


def _conv_ffn(u, P, PX, i, n_ctx):
    g, v = _split_linear(u, P["ffn_w_up"][i], PX["ffn_w_up"][i], "ffn_up")
    K = P["ffn_conv_w"][i].shape[0]
    gc = dwconv(g, P["ffn_conv_w"][i], P["ffn_conv_b"][i][None], K // 2, n_ctx, "ffn_conv")
    a = rowwise("ffn_swiglu", lambda g, v: (jax.nn.silu(g) * v,), [gc, v], [], [], 1, ROW_BLOCK,
                cb=_div_tile(g.shape[1], 1408, LANE))[0]
    return linear(a, P["ffn_w_down"][i], PX["ffn_w_down"][i], grad_dtype=BF16, name="ffn_down")


def local_loss(x_all, mods, P, PX, target, n_ctx):
    assert n_ctx == ROW_BLOCK
    depth = len(mods)
    tabs = rope_tables(n_ctx, x_all.shape[0] - n_ctx)
    x = x_all
    for i in range(depth):
        kind, j = i % 3, i // 3
        sh1, sc1, g1, sh2, sc2, g2 = mods[i]
        h = _modulate(x, P["norm_mix_g"][i][None], sh1, sc1)
        if kind == 0:
            y, bias = _attention_mixer(h, P, PX, j, n_ctx, tabs)
        elif kind == 1:
            y, bias = _rglru_mixer(h, P, PX, j, n_ctx)
        else:
            y, bias = _conformer_mixer(h, P, PX, j, n_ctx)
        if bias is None:
            x = _residual(x, y, g1)
        else:
            x = rowwise("residual_bias", lambda x, y, b, g: (x + g * (y + b),), [x, y], [bias, g1],
                        ["vec", "seg"], 1, ROW_BLOCK)[0]
        u = _modulate(x, P["norm_ffn_g"][i][None], sh2, sc2)
        x = _residual(x, _conv_ffn(u, P, PX, i, n_ctx), g2)
    per_feature = rowwise("loss_head", _loss_f, [x[n_ctx:], target], [P["final_norm_g"][None]], ["vec"], 0,
                          ROW_BLOCK, nograd=(1,))[0]
    return jnp.sum(per_feature)


def _plane_peers():
    x, y, c = lax.axis_index("x"), lax.axis_index("y"), lax.axis_index("c")
    me = 2 * x + y
    peers = [((1 - x, y, c), 2 * (1 - x) + y),
             ((x, 1 - y, c), 2 * x + (1 - y)),
             ((1 - x, 1 - y, c), 2 * (1 - x) + (1 - y))]
    return me, peers


def plane_allgather(arrays, layers, name):
    flat = []
    for k, L in enumerate(layers):
        flat += [(k, None)] if L is None else [(k, l) for l in range(L)]
    n_in, n = len(arrays), len(flat)

    def body(*refs):
        ins, outs = refs[:n_in], refs[n_in:n_in + n]
        lsem, ssem, rsem = refs[n_in + n:]
        me, peers = _plane_peers()

        def src(t):
            k, l = flat[t]
            return ins[k] if l is None else ins[k].at[l]

        def remote(t, p, slot):
            return pltpu.make_async_remote_copy(src(t), outs[t].at[slot], ssem.at[3 * t + p], rsem.at[3 * t + p],
                                                device_id=peers[p][0], device_id_type=MESH)

        local = [pltpu.make_async_copy(src(t), outs[t].at[me], lsem.at[t]) for t in range(n)]
        for t in range(n):
            local[t].start()
            for p in range(3):
                remote(t, p, me).start()
        for t in range(n):
            local[t].wait()
            for p in range(3):
                remote(t, p, peers[p][1]).wait()

    out_shape = []
    for k, l in flat:
        shp = arrays[k].shape if l is None else arrays[k].shape[1:]
        out_shape.append(jax.ShapeDtypeStruct((4,) + tuple(shp), arrays[k].dtype))
    res = pl.pallas_call(
        body, name=name, in_specs=[ANY] * n_in, out_specs=[ANY] * n, out_shape=out_shape,
        scratch_shapes=[pltpu.SemaphoreType.DMA((n,)), pltpu.SemaphoreType.DMA((3 * n,)),
                        pltpu.SemaphoreType.DMA((3 * n,))],
    )(*arrays)
    out, t = [], 0
    for L in layers:
        if L is None:
            out.append(res[t])
            t += 1
        else:
            out.append(list(res[t:t + L]))
            t += L
    return out


def plane_alltoall(groups, name):
    flat = [(k, l) for k, grp in enumerate(groups) for l in range(len(grp))]
    arrays = [a for grp in groups for a in grp]
    n, ng = len(flat), len(groups)

    def body(*refs):
        ins, outs = refs[:n], refs[n:n + ng]
        lsem, ssem, rsem = refs[n + ng:]
        me, peers = _plane_peers()

        def remote(t, p, src_slot, dst_slot):
            k, l = flat[t]
            return pltpu.make_async_remote_copy(ins[t].at[src_slot], outs[k].at[dst_slot, l], ssem.at[3 * t + p],
                                                rsem.at[3 * t + p], device_id=peers[p][0], device_id_type=MESH)

        local = [pltpu.make_async_copy(ins[t].at[me], outs[flat[t][0]].at[me, flat[t][1]], lsem.at[t])
                 for t in range(n)]
        for t in range(n):
            local[t].start()
            for p in range(3):
                remote(t, p, peers[p][1], me).start()
        for t in range(n):
            local[t].wait()
            for p in range(3):
                remote(t, p, peers[p][1], peers[p][1]).wait()

    out_shape = [jax.ShapeDtypeStruct((4, len(grp)) + tuple(grp[0].shape[1:]), grp[0].dtype) for grp in groups]
    return pl.pallas_call(
        body, name=name, in_specs=[ANY] * n, out_specs=[ANY] * ng, out_shape=out_shape,
        scratch_shapes=[pltpu.SemaphoreType.DMA((n,)), pltpu.SemaphoreType.DMA((3 * n,)),
                        pltpu.SemaphoreType.DMA((3 * n,))],
    )(*arrays)


def sibling_exchange(arrays, name):
    n = len(arrays)

    def body(*refs):
        ins, outs = refs[:n], refs[n:2 * n]
        ssem, rsem = refs[2 * n:]
        sibling = (lax.axis_index("x"), lax.axis_index("y"), 1 - lax.axis_index("c"))
        copies = [pltpu.make_async_remote_copy(ins[t], outs[t], ssem.at[t], rsem.at[t], device_id=sibling,
                                               device_id_type=MESH) for t in range(n)]
        for cp in copies:
            cp.start()
        for cp in copies:
            cp.wait()

    return pl.pallas_call(
        body, name=name, in_specs=[ANY] * n, out_specs=[ANY] * n,
        out_shape=[jax.ShapeDtypeStruct(a.shape, a.dtype) for a in arrays],
        scratch_shapes=[pltpu.SemaphoreType.DMA((n,)), pltpu.SemaphoreType.DMA((n,))],
    )(*arrays)


def gather_all_devices(a, name):
    own = plane_allgather([a], [None], name + "_plane")[0]
    sib = sibling_exchange([own], name + "_sibling")[0]
    c = lax.axis_index("c")
    c0 = jnp.where(c == 0, own, sib)
    c1 = jnp.where(c == 0, sib, own)
    return jnp.stack([c0, c1], axis=1).reshape((8,) + a.shape)


def _shape2d(shape):
    if len(shape) >= 2 and shape[-1] % LANE == 0:
        return (math.prod(shape[:-1]), shape[-1])
    return tuple(shape) if len(shape) == 2 else (1, math.prod(shape))


def _as2d(a):
    return a.reshape(_shape2d(a.shape))


def _row_tile(R, C, budget_bytes):
    if R * C * 4 <= budget_bytes or R % SUBLANE:
        return R
    cap = max(SUBLANE, (budget_bytes // (C * 4)) // SUBLANE * SUBLANE)
    return _div_tile(R, cap, SUBLANE)


def sum_slots(x4, name):
    shp = x4.shape[1:]
    R, C = _shape2d(shp)
    v = x4.reshape(4, R, C)
    tr = _row_tile(R, C, 1 << 20)

    def body(x_ref, o_ref):
        f = lambda s: x_ref[s].astype(F32)
        o_ref[...] = ((f(0) + f(1)) + f(2)) + f(3)

    out = pl.pallas_call(
        body, name=name, grid=(R // tr,),
        in_specs=[pl.BlockSpec((4, tr, C), lambda i: (0, i, 0))],
        out_specs=pl.BlockSpec((tr, C), lambda i: (i, 0)),
        out_shape=jax.ShapeDtypeStruct((R, C), F32),
        compiler_params=_cparams(("parallel",)),
    )(v)
    return out.reshape(shp)


def adamw(w, m, v, terms, name):
    shp = w.shape
    w2, m2, v2 = _as2d(w), _as2d(m), _as2d(v)
    flat = [_as2d(t) for inner in terms for t in inner]
    sizes = [len(inner) for inner in terms]
    R, C = w2.shape
    tr = _row_tile(R, C, 1 << 19)

    def body(*refs):
        w_ref, m_ref, v_ref = refs[:3]
        t_refs = refs[3:3 + len(flat)]
        g_ref, d_ref, nm_ref, nv_ref = refs[3 + len(flat):]
        g, t = None, 0
        for sz in sizes:
            inner = t_refs[t][...].astype(F32)
            for q in range(1, sz):
                inner = inner + t_refs[t + q][...].astype(F32)
            t += sz
            g = inner if g is None else g + inner
        nm = ADAM_B1 * m_ref[...] + (1.0 - ADAM_B1) * g
        nv = ADAM_B2 * v_ref[...] + (1.0 - ADAM_B2) * jnp.square(g)
        m_hat = nm / (1.0 - ADAM_B1 ** ADAM_STEP)
        v_hat = nv / (1.0 - ADAM_B2 ** ADAM_STEP)
        g_ref[...] = g
        d_ref[...] = -ADAM_LR * (m_hat / (jnp.sqrt(v_hat) + ADAM_EPS) + ADAM_WD * w_ref[...])
        nm_ref[...] = nm
        nv_ref[...] = nv

    spec = pl.BlockSpec((tr, C), lambda i: (i, 0))
    outs = pl.pallas_call(
        body, name=name, grid=(R // tr,),
        in_specs=[spec] * (3 + len(flat)), out_specs=[spec] * 4,
        out_shape=[jax.ShapeDtypeStruct((R, C), F32)] * 4,
        compiler_params=_cparams(("parallel",)),
    )(w2, m2, v2, *flat)
    return tuple(o.reshape(shp) for o in outs)


def _pack(arrs):
    flat = jnp.concatenate([a.reshape(-1).astype(F32) for a in arrs])
    unit = SUBLANE * LANE
    pad = (-flat.shape[0]) % unit
    return jnp.pad(flat, (0, pad)).reshape(-1, LANE)


def _unpack(pack, shapes, lead=()):
    flat = pack.reshape(tuple(lead) + (-1,))
    out, off = [], 0
    for s in shapes:
        n = math.prod(s)
        out.append(flat[..., off:off + n].reshape(tuple(lead) + tuple(s)))
        off += n
    return out


COL_SHARDED = ("attn_w_qkv", "lru_w_in", "conf_w_in", "ffn_w_up")
ROW_SHARDED = ("attn_w_o", "lru_w_out", "conf_w_out", "ffn_w_down")
SMALL_SHARDED = ("lru_conv_w", "lru_conv_b", "lru_ba", "lru_bx", "lru_lambda", "conf_b_in", "conf_dw_w", "conf_dw_b",
                 "conf_ln_g", "conf_ln_b", "conf_b_out", "ffn_conv_w")
REPLICATED = ("norm_mix_g", "norm_ffn_g", "attn_sink", "lru_wa", "lru_wx", "ffn_conv_b", "final_norm_g")
LOCAL_ONLY = ("c_ctx", "ada_b")
WEIGHTS = ("c_ctx", "ada_w", "ada_b", "norm_mix_g", "norm_ffn_g", "attn_w_qkv", "attn_w_o", "attn_sink", "lru_w_in",
           "lru_conv_w", "lru_conv_b", "lru_wa", "lru_ba", "lru_wx", "lru_bx", "lru_lambda", "lru_w_out", "conf_w_in",
           "conf_b_in", "conf_dw_w", "conf_dw_b", "conf_ln_g", "conf_ln_b", "conf_w_out", "conf_b_out", "ffn_w_up",
           "ffn_conv_w", "ffn_conv_b", "ffn_w_down", "final_norm_g")


def _full_last_axis(g4):
    moved = jnp.moveaxis(g4, 0, -2)
    return moved.reshape(moved.shape[:-2] + (-1,))


def _shards_last_axis(full):
    split = full.reshape(full.shape[:-1] + (4, full.shape[-1] // 4))
    return jnp.moveaxis(split, -2, 0)


def _train_step(W, M, V, x, c, ctx, loss_target):
    n_ctx = ctx.shape[1]
    depth = W["ada_w"].shape[0]
    D = x.shape[-1]
    my_c = lax.axis_index("c")
    my_s = 2 * lax.axis_index("x") + lax.axis_index("y")
    my_dev = 2 * my_s + my_c

    big = COL_SHARDED + ROW_SHARDED
    small_shapes = [W[k].shape for k in SMALL_SHARDED]
    small_pack = _pack([W[k] for k in SMALL_SHARDED])
    gathered = plane_allgather([W[k].astype(BF16) for k in big] + [small_pack],
                               [W[k].shape[0] for k in big] + [None], "gather_weights")
    P, PX = {}, {}
    for k, per_layer in zip(big, gathered[:len(big)]):
        if k in ROW_SHARDED:
            per_layer = [g.reshape(1, -1, g.shape[-1]) for g in per_layer]
        P[k] = per_layer
        PX[k] = [jnp.zeros(g.shape, BF16) for g in per_layer]
    for k, g4 in zip(SMALL_SHARDED, _unpack(gathered[-1], small_shapes, lead=(4,))):
        P[k] = _full_last_axis(g4)
    for k in REPLICATED:
        P[k] = W[k]

    c8 = gather_all_devices(c, "gather_c").reshape(8, D)
    cond16 = jnp.concatenate([c8, jnp.broadcast_to(W["c_ctx"][None], (8, D))], axis=0)
    act16, act_vjp = jax.vjp(jax.nn.silu, cond16)
    n_ada = W["ada_w"].shape[-1]
    m_shard = matmul_nn(act16, W["ada_w"], name="ada_fwd")
    m4 = plane_allgather([m_shard], [None], "gather_ada")[0].reshape(4, 16, depth, n_ada)
    m_full = m4.transpose(2, 1, 0, 3).reshape(depth, 16, 4 * n_ada) + W["ada_b"][:, None, :]
    m_lat = lax.dynamic_index_in_dim(m_full, my_dev, axis=1, keepdims=False)
    m_ctx = m_full[:, 8]
    mods = [tuple(jnp.stack([a, b])[:, None, :] for a, b in zip(jnp.split(m_ctx[i], 6), jnp.split(m_lat[i], 6)))
            for i in range(depth)]

    x_all = jnp.concatenate([ctx[0], x[0]], axis=0)
    loss, (gx, gmods, gP, gPX) = jax.value_and_grad(
        lambda xa, md, p, px: local_loss(xa, md, p, px, loss_target[0], n_ctx), argnums=(0, 1, 2, 3))(x_all, mods, P, PX)
    loss = lax.psum(loss, ("x", "y", "c"))
    grad_x = gx[n_ctx:][None]

    dm_mine = jnp.stack([jnp.stack([jnp.concatenate([g[r, 0] for g in gmods[i]]) for i in range(depth)])
                         for r in range(2)])
    dm8 = gather_all_devices(dm_mine, "gather_dmod")
    dm16 = jnp.concatenate([dm8[:, 1], dm8[:, 0]], axis=0).transpose(1, 0, 2)
    grad_ada_b = rowwise("ada_b_grad", lambda a: (jnp.sum(a, axis=0, keepdims=True),),
                         [dm16.transpose(1, 0, 2).reshape(16, -1)], [], [], 0, 16)[0].reshape(depth, -1)
    dm_cols = lax.dynamic_slice_in_dim(dm16, my_s * n_ada, n_ada, axis=2)
    dm_cols = dm_cols.transpose(1, 0, 2).reshape(16, depth * n_ada)
    grad_ada_w = matmul_tn(act16, dm_cols, depth, name="ada_dw")
    dact_part = matmul_nt(dm_cols, W["ada_w"], name="ada_dx")
    dact4 = plane_allgather([dact_part], [None], "gather_dact")[0]
    dact = ((dact4[0] + dact4[1]) + dact4[2]) + dact4[3]
    grad_c_ctx = jnp.sum(act_vjp(dact)[0][8:], axis=0)

    out = {}
    out["ada_w"] = adamw(W["ada_w"], M["ada_w"], V["ada_w"], [[grad_ada_w]], "adamw_ada_w")
    local_shapes = [W[k].shape for k in LOCAL_ONLY]
    res = adamw(_pack([W[k] for k in LOCAL_ONLY]), _pack([M[k] for k in LOCAL_ONLY]), _pack([V[k] for k in LOCAL_ONLY]),
                [[_pack([grad_c_ctx, grad_ada_b])]], "adamw_local")
    for k, vals in zip(LOCAL_ONLY, zip(*[_unpack(r, local_shapes) for r in res])):
        out[k] = vals

    groups = []
    for k in big:
        pieces = gPX[k]
        if k in ROW_SHARDED:
            pieces = [g.reshape((4, -1, g.shape[-1])) for g in pieces]
        groups.append(pieces)
    small_grads = _pack_shards([_shards_last_axis(gP[k]) for k in SMALL_SHARDED])
    groups.append([small_grads])
    received = plane_alltoall(groups, "scatter_grads")
    plane_sums = [sum_slots(r, "plane_sum") for r in received]
    sibling_sums = sibling_exchange(plane_sums, "sibling_grads")
    for k, own, sib in zip(big, plane_sums[:len(big)], sibling_sums[:len(big)]):
        out[k] = adamw(W[k], M[k], V[k], [[own, sib]], "adamw_" + k)
    res = adamw(small_pack, _pack([M[k] for k in SMALL_SHARDED]), _pack([V[k] for k in SMALL_SHARDED]),
                [[plane_sums[-1][0], sibling_sums[-1][0]]], "adamw_small")
    for k, vals in zip(SMALL_SHARDED, zip(*[_unpack(r, small_shapes) for r in res])):
        out[k] = vals

    rep_shapes = [W[k].shape for k in REPLICATED]
    own = plane_allgather([_pack([gP[k] for k in REPLICATED])], [None], "gather_rep_grads")[0]
    sib = sibling_exchange([own], "sibling_rep_grads")[0]
    res = adamw(_pack([W[k] for k in REPLICATED]), _pack([M[k] for k in REPLICATED]), _pack([V[k] for k in REPLICATED]),
                [[own[s], sib[s]] for s in range(4)], "adamw_rep")
    for k, vals in zip(REPLICATED, zip(*[_unpack(r, rep_shapes) for r in res])):
        out[k] = vals

    return (loss, grad_x) + tuple(out[k][j] for j in range(4) for k in WEIGHTS)


def _pack_shards(arrs4):
    return jnp.stack([_pack([a[s] for a in arrs4]) for s in range(4)])


def kernel(x, c, ctx, c_ctx, ada_w, ada_b, norm_mix_g, norm_ffn_g, attn_w_qkv, attn_w_o, attn_sink, lru_w_in, lru_conv_w, lru_conv_b, lru_wa, lru_ba, lru_wx, lru_bx, lru_lambda, lru_w_out, conf_w_in, conf_b_in, conf_dw_w, conf_dw_b, conf_ln_g, conf_ln_b, conf_w_out, conf_b_out, ffn_w_up, ffn_conv_w, ffn_conv_b, ffn_w_down, final_norm_g, loss_target, m_c_ctx, m_ada_w, m_ada_b, m_norm_mix_g, m_norm_ffn_g, m_attn_w_qkv, m_attn_w_o, m_attn_sink, m_lru_w_in, m_lru_conv_w, m_lru_conv_b, m_lru_wa, m_lru_ba, m_lru_wx, m_lru_bx, m_lru_lambda, m_lru_w_out, m_conf_w_in, m_conf_b_in, m_conf_dw_w, m_conf_dw_b, m_conf_ln_g, m_conf_ln_b, m_conf_w_out, m_conf_b_out, m_ffn_w_up, m_ffn_conv_w, m_ffn_conv_b, m_ffn_w_down, m_final_norm_g, v_c_ctx, v_ada_w, v_ada_b, v_norm_mix_g, v_norm_ffn_g, v_attn_w_qkv, v_attn_w_o, v_attn_sink, v_lru_w_in, v_lru_conv_w, v_lru_conv_b, v_lru_wa, v_lru_ba, v_lru_wx, v_lru_bx, v_lru_lambda, v_lru_w_out, v_conf_w_in, v_conf_b_in, v_conf_dw_w, v_conf_dw_b, v_conf_ln_g, v_conf_ln_b, v_conf_w_out, v_conf_b_out, v_ffn_w_up, v_ffn_conv_w, v_ffn_conv_b, v_ffn_w_down, v_final_norm_g):
    given = dict(locals())
    W = {k: given[k] for k in WEIGHTS}
    M = {k: given["m_" + k] for k in WEIGHTS}
    V = {k: given["v_" + k] for k in WEIGHTS}
    return _train_step(W, M, V, x, c, ctx, loss_target)
```

```python
import functools
import math

import jax
import jax.numpy as jnp
from jax import lax
from jax.experimental import pallas as pl
from jax.experimental.pallas import tpu as pltpu

F32 = jnp.float32
BF16 = jnp.bfloat16

VMEM_LIMIT_BYTES = 56 * 1024 * 1024
LANE = 128
SUBLANE = 8

HEAD_DIM = 128
GQA_GROUP = 4
WINDOW_BLOCK = 128
GRID_W = 64
ROPE_THETA = 10000.0
ROPE_FREQS = HEAD_DIM // 4
LRU_BLOCK = 128
LRU_C = 8.0
NORM_EPS = 1e-6
NEG_INF = -1e30
CONV_HALO = 32

ADAM_LR = 0.001
ADAM_B1 = 0.9
ADAM_B2 = 0.999
ADAM_EPS = 1e-08
ADAM_WD = 0.01
ADAM_STEP = 10

MESH = pl.DeviceIdType.MESH
ANY = pl.BlockSpec(memory_space=pl.ANY)


def _cparams(sem):
    return pltpu.CompilerParams(dimension_semantics=sem, vmem_limit_bytes=VMEM_LIMIT_BYTES)


def _div_tile(n, cap, unit):
    if n <= unit:
        return n
    best = None
    t = unit
    while t <= min(n, cap):
        if n % t == 0:
            best = t
        t += unit
    assert best is not None, (n, cap, unit)
    return best


def _dot(a, b, dims):
    return lax.dot_general(a.astype(BF16), b.astype(BF16), (dims, ((), ())), preferred_element_type=F32)


def matmul_nn(a, b, diag=False, name="mm_nn"):
    M = a.shape[0]
    G, K, n = b.shape
    tm = _div_tile(M, 768, SUBLANE)
    tn = _div_tile(n, 1408, LANE)
    tk = K if K <= 2048 else _div_tile(K, 1408, LANE)
    nk, nn = K // tk, n // tn
    assert a.shape[1] == (G * K if diag else K)

    def body(a_ref, b_ref, o_ref, acc_ref):
        r = pl.program_id(3)
        part = _dot(a_ref[...], b_ref[...], ((1,), (0,)))
        if nk == 1:
            o_ref[...] = part
        else:
            @pl.when(r == 0)
            def _():
                acc_ref[...] = part

            @pl.when(r > 0)
            def _():
                acc_ref[...] += part

            @pl.when(r == nk - 1)
            def _():
                o_ref[...] = acc_ref[...]

    a_map = (lambda i, g, j, r: (i, g * nk + r)) if diag else (lambda i, g, j, r: (i, r))
    return pl.pallas_call(
        body, name=name,
        grid=(M // tm, G, nn, nk),
        in_specs=[pl.BlockSpec((tm, tk), a_map),
                  pl.BlockSpec((None, tk, tn), lambda i, g, j, r: (g, r, j))],
        out_specs=pl.BlockSpec((tm, tn), lambda i, g, j, r: (i, g * nn + j)),
        out_shape=jax.ShapeDtypeStruct((M, G * n), F32),
        scratch_shapes=[pltpu.VMEM((tm, tn) if nk > 1 else (SUBLANE, LANE), F32)],
        compiler_params=_cparams(("parallel", "parallel", "parallel", "arbitrary")),
    )(a, b)


def matmul_nt(dy, b, diag=False, out_dtype=F32, name="mm_nt"):
    M = dy.shape[0]
    G, K, n = b.shape
    assert dy.shape[1] == G * n
    tm = _div_tile(M, 768, SUBLANE)
    tko = _div_tile(K, 1408, LANE)
    tr = _div_tile(n, 2048, LANE)
    nr, nko = n // tr, K // tko
    steps = nr if diag else G * nr

    def body(dy_ref, b_ref, o_ref, acc_ref):
        r = pl.program_id(3)
        part = _dot(dy_ref[...], b_ref[...], ((1,), (1,)))
        if steps == 1:
            o_ref[...] = part.astype(out_dtype)
        else:
            @pl.when(r == 0)
            def _():
                acc_ref[...] = part

            @pl.when(r > 0)
            def _():
                acc_ref[...] += part

            @pl.when(r == steps - 1)
            def _():
                o_ref[...] = acc_ref[...].astype(out_dtype)

    if diag:
        grid = (M // tm, G, nko, nr)
        dy_map = lambda i, g, kk, r: (i, g * nr + r)
        b_map = lambda i, g, kk, r: (g, kk, r)
        o_map = lambda i, g, kk, r: (i, g * nko + kk)
        out_cols = G * K
    else:
        grid = (M // tm, 1, nko, G * nr)
        dy_map = lambda i, g, kk, r: (i, r)
        b_map = lambda i, g, kk, r: (r // nr, kk, r % nr)
        o_map = lambda i, g, kk, r: (i, kk)
        out_cols = K
    return pl.pallas_call(
        body, name=name,
        grid=grid,
        in_specs=[pl.BlockSpec((tm, tr), dy_map), pl.BlockSpec((None, tko, tr), b_map)],
        out_specs=pl.BlockSpec((tm, tko), o_map),
        out_shape=jax.ShapeDtypeStruct((M, out_cols), out_dtype),
        scratch_shapes=[pltpu.VMEM((tm, tko) if steps > 1 else (SUBLANE, LANE), F32)],
        compiler_params=_cparams(("parallel", "parallel", "parallel", "arbitrary")),
    )(dy, b)


def matmul_tn(a, dy, G, diag=False, out_dtype=F32, name="mm_tn"):
    M = a.shape[0]
    n = dy.shape[1] // G
    K = a.shape[1] // G if diag else a.shape[1]
    tm = _div_tile(M, 768, SUBLANE)
    tk = _div_tile(K, 1408, LANE)
    tn = _div_tile(n, 1408, LANE)
    nkb, nn, nm = K // tk, n // tn, M // tm

    def body(a_ref, dy_ref, o_ref, acc_ref):
        r = pl.program_id(3)
        part = _dot(a_ref[...], dy_ref[...], ((0,), (0,)))
        if nm == 1:
            o_ref[...] = part.astype(out_dtype)
        else:
            @pl.when(r == 0)
            def _():
                acc_ref[...] = part

            @pl.when(r > 0)
            def _():
                acc_ref[...] += part

            @pl.when(r == nm - 1)
            def _():
                o_ref[...] = acc_ref[...].astype(out_dtype)

    a_map = (lambda g, kk, j, r: (r, g * nkb + kk)) if diag else (lambda g, kk, j, r: (r, kk))
    return pl.pallas_call(
        body, name=name,
        grid=(G, nkb, nn, nm),
        in_specs=[pl.BlockSpec((tm, tk), a_map),
                  pl.BlockSpec((tm, tn), lambda g, kk, j, r: (r, g * nn + j))],
        out_specs=pl.BlockSpec((None, tk, tn), lambda g, kk, j, r: (g, kk, j)),
        out_shape=jax.ShapeDtypeStruct((G, K, n), out_dtype),
        scratch_shapes=[pltpu.VMEM((tk, tn) if nm > 1 else (SUBLANE, LANE), F32)],
        compiler_params=_cparams(("parallel", "parallel", "parallel", "arbitrary")),
    )(a, dy)


def linear(a, w, w_grad_proxy, diag=False, grad_dtype=F32, name="lin"):
    G = w.shape[0]

    @jax.custom_vjp
    def op(a, w, proxy):
        return matmul_nn(a, w, diag, name + "_fwd")

    def fwd(a, w, proxy):
        return matmul_nn(a, w, diag, name + "_fwd"), (a, w)

    def bwd(res, dy):
        a, w = res
        da = matmul_nt(dy, w, diag, a.dtype, name + "_dx")
        dw = matmul_tn(a, dy, G, diag, grad_dtype, name + "_dw")
        return da, None, dw

    op.defvjp(fwd, bwd)
    return op(a, w, w_grad_proxy)


def rowwise(name, f, blocks, params, pkinds, n_out, tb, cb=None, nograd=()):
    rows = blocks[0].shape[0]
    tb = min(tb, rows)
    assert rows % tb == 0
    nrb = rows // tb
    nb, npar = len(blocks), len(params)
    widths = [b.shape[1] for b in blocks]
    if cb is None:
        ncb = 1
        bw = widths
    else:
        assert all(w == widths[0] for w in widths) and widths[0] % cb == 0
        ncb = widths[0] // cb
        bw = [cb] * nb
    pw = []
    for p, kind in zip(params, pkinds):
        full = p.shape[-1]
        pw.append(full if cb is None or full != widths[0] else cb)
    for p, kind in zip(params, pkinds):
        assert p.shape[:-1] == ((1,) if kind == "vec" else (2, 1)), (name, p.shape, kind)

    shapes = jax.eval_shape(
        f, *[jax.ShapeDtypeStruct((tb, w), b.dtype) for w, b in zip(bw, blocks)],
        *[jax.ShapeDtypeStruct((1, w), p.dtype) for w, p in zip(pw, params)])
    shapes = tuple(shapes)
    out_sh, acc_sh = shapes[:n_out], shapes[n_out:]
    n_acc = len(acc_sh)
    for s in out_sh:
        assert s.shape[0] == tb
    for s in acc_sh:
        assert s.shape[0] == 1

    def blk_spec(w):
        return pl.BlockSpec((tb, w), lambda c, r: (r, c))

    def par_spec(w, kind, full):
        col = (lambda c: c) if (cb is not None and full == widths[0]) else (lambda c: 0)
        if kind == "vec":
            return pl.BlockSpec((1, w), lambda c, r: (0, col(c)))
        return pl.BlockSpec((None, 1, w), lambda c, r: (jnp.minimum(r, 1), 0, col(c)))

    blk_specs = [blk_spec(w) for w in bw]
    par_specs = [par_spec(w, k, p.shape[-1]) for w, k, p in zip(pw, pkinds, params)]
    sem = _cparams(("parallel", "arbitrary"))

    def run_fwd(blocks, params):
        def body(*refs):
            ins = [r[...] for r in refs[:nb + npar]]
            outs = refs[nb + npar:]
            res = f(*ins)
            r_id = pl.program_id(1)
            for o_ref, val in zip(outs[:n_out], res[:n_out]):
                o_ref[...] = val
            for o_ref, val in zip(outs[n_out:], res[n_out:]):
                @pl.when(r_id == 0)
                def _(o_ref=o_ref, val=val):
                    o_ref[...] = val

                @pl.when(r_id > 0)
                def _(o_ref=o_ref, val=val):
                    o_ref[...] += val

        out_specs = [blk_spec(s.shape[1]) for s in out_sh] + \
                    [pl.BlockSpec((1, s.shape[1]), lambda c, r: (0, c)) for s in acc_sh]
        out_shape = [jax.ShapeDtypeStruct((rows, s.shape[1] * ncb), s.dtype) for s in out_sh] + \
                    [jax.ShapeDtypeStruct((1, s.shape[1] * ncb), s.dtype) for s in acc_sh]
        return tuple(pl.pallas_call(
            body, name=name + "_fwd", grid=(ncb, nrb),
            in_specs=blk_specs + par_specs, out_specs=out_specs, out_shape=out_shape,
            compiler_params=sem)(*blocks, *params))

    def run_bwd(blocks, params, cts):
        d_outs, d_accs = list(cts[:n_out]), list(cts[n_out:])
        want = [i for i in range(nb) if i not in nograd]

        def body(*refs):
            k = nb + npar
            ins = [r[...] for r in refs[:k]]
            ct = tuple(r[...] for r in refs[k:k + n_out + n_acc])
            outs = refs[k + n_out + n_acc:]
            _, vjp = jax.vjp(lambda *a: tuple(f(*a)), *ins)
            grads = vjp(ct)
            r_id = pl.program_id(1)
            for o_ref, i in zip(outs[:len(want)], want):
                o_ref[...] = grads[i].astype(o_ref.dtype)
            for o_ref, g, kind in zip(outs[len(want):], grads[nb:], pkinds):
                first = (r_id == 0) if kind == "vec" else (r_id <= 1)

                @pl.when(first)
                def _(o_ref=o_ref, g=g):
                    o_ref[...] = g.astype(o_ref.dtype)

                @pl.when(jnp.logical_not(first))
                def _(o_ref=o_ref, g=g):
                    o_ref[...] += g.astype(o_ref.dtype)

        ct_specs = [blk_spec(s.shape[1]) for s in out_sh] + \
                   [pl.BlockSpec((1, s.shape[1]), lambda c, r: (0, c)) for s in acc_sh]
        out_specs = [blk_specs[i] for i in want] + par_specs
        out_shape = [jax.ShapeDtypeStruct(blocks[i].shape, blocks[i].dtype) for i in want] + \
                    [jax.ShapeDtypeStruct(p.shape, p.dtype) for p in params]
        res = pl.pallas_call(
            body, name=name + "_bwd", grid=(ncb, nrb),
            in_specs=blk_specs + par_specs + ct_specs, out_specs=out_specs, out_shape=out_shape,
            compiler_params=sem)(*blocks, *params, *d_outs, *d_accs)
        d_blocks = [None] * nb
        for i, g in zip(want, res[:len(want)]):
            d_blocks[i] = g
        return tuple(d_blocks), tuple(res[len(want):])

    @jax.custom_vjp
    def op(blocks, params):
        return run_fwd(blocks, params)

    def op_fwd(blocks, params):
        return run_fwd(blocks, params), (blocks, params)

    def op_bwd(res, cts):
        return run_bwd(res[0], res[1], cts)

    op.defvjp(op_fwd, op_bwd)
    return op(tuple(blocks), tuple(params))


def _conv_window(prev_ref, cur_ref, next_ref, win_ref, i, tb, rows, n_ctx):
    starts = (i == 0) | (i * tb == n_ctx)
    ends = ((i + 1) * tb == rows) | ((i + 1) * tb == n_ctx)
    win_ref[pl.ds(0, CONV_HALO), :] = jnp.where(starts, 0.0, prev_ref[...])
    win_ref[pl.ds(CONV_HALO, tb), :] = cur_ref[...]
    win_ref[pl.ds(CONV_HALO + tb, CONV_HALO), :] = jnp.where(ends, 0.0, next_ref[...])


def _conv_specs(rows, tb, cb):
    hb = tb // CONV_HALO
    last = rows // CONV_HALO - 1
    prev = pl.BlockSpec((CONV_HALO, cb), lambda c, i: (jnp.maximum(i * hb - 1, 0), c))
    cur = pl.BlockSpec((tb, cb), lambda c, i: (i, c))
    nxt = pl.BlockSpec((CONV_HALO, cb), lambda c, i: (jnp.minimum((i + 1) * hb, last), c))
    return prev, cur, nxt


def _dwconv_apply(x, w, b, pad_left, n_ctx, name):
    rows, C = x.shape
    K = w.shape[0]
    tb = 256
    cb = _div_tile(C, 512, LANE)
    assert rows % tb == 0 and n_ctx % tb == 0 and K - 1 <= CONV_HALO

    def body(prev_ref, cur_ref, next_ref, w_ref, b_ref, o_ref, win_ref):
        i = pl.program_id(1)
        _conv_window(prev_ref, cur_ref, next_ref, win_ref, i, tb, rows, n_ctx)
        acc = b_ref[...] + win_ref[pl.ds(CONV_HALO - pad_left, tb), :] * w_ref[pl.ds(0, 1), :]
        for k in range(1, K):
            acc = acc + win_ref[pl.ds(CONV_HALO + k - pad_left, tb), :] * w_ref[pl.ds(k, 1), :]
        o_ref[...] = acc

    prev, cur, nxt = _conv_specs(rows, tb, cb)
    return pl.pallas_call(
        body, name=name, grid=(C // cb, rows // tb),
        in_specs=[prev, cur, nxt, pl.BlockSpec((K, cb), lambda c, i: (0, c)),
                  pl.BlockSpec((1, cb), lambda c, i: (0, c))],
        out_specs=pl.BlockSpec((tb, cb), lambda c, i: (i, c)),
        out_shape=jax.ShapeDtypeStruct((rows, C), F32),
        scratch_shapes=[pltpu.VMEM((tb + 2 * CONV_HALO, cb), F32)],
        compiler_params=_cparams(("parallel", "arbitrary")),
    )(x, x, x, w, b)


def _dwconv_wgrad(x, dy, K, pad_left, n_ctx, name):
    rows, C = x.shape
    tb = 256
    cb = _div_tile(C, 512, LANE)

    def body(prev_ref, cur_ref, next_ref, dy_ref, dw_ref, db_ref, win_ref):
        i = pl.program_id(1)
        _conv_window(prev_ref, cur_ref, next_ref, win_ref, i, tb, rows, n_ctx)
        dy = dy_ref[...]

        @pl.when(i == 0)
        def _():
            dw_ref[...] = jnp.zeros_like(dw_ref)
            db_ref[...] = jnp.zeros_like(db_ref)

        db_ref[...] += jnp.sum(dy, axis=0, keepdims=True)
        for k in range(K):
            tap = win_ref[pl.ds(CONV_HALO + k - pad_left, tb), :]
            dw_ref[pl.ds(k, 1), :] += jnp.sum(tap * dy, axis=0, keepdims=True)

    prev, cur, nxt = _conv_specs(rows, tb, cb)
    return pl.pallas_call(
        body, name=name, grid=(C // cb, rows // tb),
        in_specs=[prev, cur, nxt, pl.BlockSpec((tb, cb), lambda c, i: (i, c))],
        out_specs=[pl.BlockSpec((K, cb), lambda c, i: (0, c)), pl.BlockSpec((1, cb), lambda c, i: (0, c))],
        out_shape=[jax.ShapeDtypeStruct((K, C), F32), jax.ShapeDtypeStruct((1, C), F32)],
        scratch_shapes=[pltpu.VMEM((tb + 2 * CONV_HALO, cb), F32)],
        compiler_params=_cparams(("parallel", "arbitrary")),
    )(x, x, x, dy)


def dwconv(x, w, b, pad_left, n_ctx, name):
    K = w.shape[0]

    @jax.custom_vjp
    def op(x, w, b):
        return _dwconv_apply(x, w, b, pad_left, n_ctx, name + "_fwd")

    def fwd(x, w, b):
        return _dwconv_apply(x, w, b, pad_left, n_ctx, name + "_fwd"), (x, w)

    def bwd(res, dy):
        x, w = res
        dx = _dwconv_apply(dy, w[::-1], jnp.zeros((1, w.shape[1]), F32), K - 1 - pad_left, n_ctx, name + "_dx")
        dw, db = _dwconv_wgrad(x, dy, K, pad_left, n_ctx, name + "_dw")
        return dx, dw, db

    op.defvjp(fwd, bwd)
    return op(x, w, b)


def _block_scan(a, b, reverse):
    tb = a.shape[0]
    row = lax.broadcasted_iota(jnp.int32, (tb, 1), 0)
    s = 1
    while s < tb:
        if reverse:
            keep = row < tb - s
            a_sh = pltpu.roll(a, tb - s, 0)
            b_sh = pltpu.roll(b, tb - s, 0)
        else:
            keep = row >= s
            a_sh = pltpu.roll(a, s, 0)
            b_sh = pltpu.roll(b, s, 0)
        b = jnp.where(keep, a * b_sh + b, b)
        a = jnp.where(keep, a * a_sh, a)
        s *= 2
    return a, b


def _shift_in(h, carry, reverse):
    tb = h.shape[0]
    row = lax.broadcasted_iota(jnp.int32, (tb, 1), 0)
    if reverse:
        return jnp.where(row == tb - 1, carry, pltpu.roll(h, tb - 1, 0))
    return jnp.where(row == 0, carry, pltpu.roll(h, 1, 0))


def _scan_maps(nrb, rot, reverse):
    def phys(c, i):
        q = (nrb - 1 - i) if reverse else i
        return (lax.rem(q + rot, nrb), c)
    return phys


def _scan_fwd(a, b, rot, reverse, name):
    rows, C = a.shape
    tb = 256
    cb = _div_tile(C, 512, LANE)
    nrb = rows // tb
    last_row = 0 if reverse else tb - 1

    def body(a_ref, b_ref, h_ref, hp_ref, carry_ref):
        i = pl.program_id(1)

        @pl.when(i == 0)
        def _():
            carry_ref[...] = jnp.zeros_like(carry_ref)

        carry = carry_ref[pl.ds(0, 1), :]
        A, B = _block_scan(a_ref[...], b_ref[...], reverse)
        h = A * carry + B
        h_ref[...] = h
        hp_ref[...] = _shift_in(h, carry, reverse)
        carry_ref[pl.ds(0, 1), :] = h[last_row:last_row + 1, :]

    spec = pl.BlockSpec((tb, cb), _scan_maps(nrb, rot, reverse))
    return pl.pallas_call(
        body, name=name, grid=(C // cb, nrb),
        in_specs=[spec, spec], out_specs=[spec, spec],
        out_shape=[jax.ShapeDtypeStruct((rows, C), F32)] * 2,
        scratch_shapes=[pltpu.VMEM((SUBLANE, cb), F32)],
        compiler_params=_cparams(("parallel", "arbitrary")),
    )(a, b)


def _scan_bwd(a, dh, h_prev, rot, reverse, name):
    rows, C = a.shape
    tb = 256
    cb = _div_tile(C, 512, LANE)
    nrb = rows // tb
    adj = not reverse
    last_row = 0 if adj else tb - 1

    def body(a_ref, dh_ref, hp_ref, da_ref, db_ref, carry_ref):
        i = pl.program_id(1)

        @pl.when(i == 0)
        def _():
            carry_ref[...] = jnp.zeros_like(carry_ref)

        carry = carry_ref[pl.ds(0, 1), :]
        a = a_ref[...]
        dh = dh_ref[...]
        A, B = _block_scan(a, a * dh, adj)
        u = A * carry + B
        g = dh + _shift_in(u, carry, adj)
        db_ref[...] = g
        da_ref[...] = g * hp_ref[...]
        carry_ref[pl.ds(0, 1), :] = u[last_row:last_row + 1, :]

    spec = pl.BlockSpec((tb, cb), _scan_maps(nrb, rot, adj))
    return pl.pallas_call(
        body, name=name, grid=(C // cb, nrb),
        in_specs=[spec, spec, spec], out_specs=[spec, spec],
        out_shape=[jax.ShapeDtypeStruct((rows, C), F32)] * 2,
        scratch_shapes=[pltpu.VMEM((SUBLANE, cb), F32)],
        compiler_params=_cparams(("parallel", "arbitrary")),
    )(a, dh, h_prev)


def linear_scan(a, b, rot, reverse, name):
    @jax.custom_vjp
    def op(a, b):
        return _scan_fwd(a, b, rot, reverse, name + "_fwd")[0]

    def fwd(a, b):
        h, hp = _scan_fwd(a, b, rot, reverse, name + "_fwd")
        return h, (a, hp)

    def bwd(res, dh):
        a, hp = res
        da, db = _scan_bwd(a, dh, hp, rot, reverse, name + "_bwd")
        return da, db

    op.defvjp(fwd, bwd)
    return op(a, b)


def rope_tables(n_ctx, n_lat):
    t = jnp.arange(n_lat)
    pos = jnp.stack([t // GRID_W, t % GRID_W], axis=-1).astype(F32)
    freq = ROPE_THETA ** (-jnp.arange(ROPE_FREQS, dtype=F32) / ROPE_FREQS)
    ang = pos[:, :, None] * freq
    cos, sin = jnp.cos(ang), jnp.sin(ang)
    c = jnp.concatenate([cos[:, 0], cos[:, 0], cos[:, 1], cos[:, 1]], axis=-1)
    s = jnp.concatenate([-sin[:, 0], sin[:, 0], -sin[:, 1], sin[:, 1]], axis=-1)
    c = jnp.concatenate([jnp.ones((n_ctx, HEAD_DIM), F32), c], axis=0)
    s = jnp.concatenate([jnp.zeros((n_ctx, HEAD_DIM), F32), s], axis=0)
    return c, s


def _rope_apply(qkv, c_tab, s_tab, n_rot_heads, name):
    rows, cols = qkv.shape
    tb = _div_tile(rows, 768, SUBLANE)
    heads = cols // HEAD_DIM
    hb = max(h for h in (4, 2, 1) if heads % h == 0 and n_rot_heads % h == 0)
    wb = hb * HEAD_DIM

    def body(x_ref, c_ref, s_ref, o_ref):
        x = x_ref[...]
        lane = lax.broadcasted_iota(jnp.int32, x.shape, 1)
        swapped = jnp.where((lane & 63) < 32, pltpu.roll(x, wb - 32, 1), pltpu.roll(x, 32, 1))
        roped = x * jnp.tile(c_ref[...], (1, hb)) + swapped * jnp.tile(s_ref[...], (1, hb))
        o_ref[...] = jnp.where(pl.program_id(1) * hb < n_rot_heads, roped, x)

    tab = pl.BlockSpec((tb, HEAD_DIM), lambda i, j: (i, 0))
    blk = pl.BlockSpec((tb, wb), lambda i, j: (i, j))
    return pl.pallas_call(
        body, name=name, grid=(rows // tb, cols // wb),
        in_specs=[blk, tab, tab], out_specs=blk,
        out_shape=jax.ShapeDtypeStruct((rows, cols), F32),
        compiler_params=_cparams(("parallel", "arbitrary")),
    )(qkv, c_tab, s_tab)


def rope(qkv, c_tab, s_tab, n_rot_heads, name):
    @jax.custom_vjp
    def op(qkv):
        return _rope_apply(qkv, c_tab, s_tab, n_rot_heads, name + "_fwd")

    def fwd(qkv):
        return _rope_apply(qkv, c_tab, s_tab, n_rot_heads, name + "_fwd"), None

    def bwd(_, d):
        return (_rope_apply(d, c_tab, -s_tab, n_rot_heads, name + "_bwd"),)

    op.defvjp(fwd, bwd)
    return op(qkv)


def _attn_in_specs(H, KV, n_ctx, nqb):
    ncb = n_ctx // WINDOW_BLOCK
    G = GQA_GROUP

    def loc(delta, col0):
        return pl.BlockSpec((WINDOW_BLOCK, HEAD_DIM),
                            lambda g, i: (jnp.clip(i + delta, ncb, nqb - 1), col0 + g))

    q = pl.BlockSpec((WINDOW_BLOCK, G * HEAD_DIM), lambda g, i: (i, g))
    kc = pl.BlockSpec((n_ctx, HEAD_DIM), lambda g, i: (0, H + g))
    vc = pl.BlockSpec((n_ctx, HEAD_DIM), lambda g, i: (0, H + KV + g))
    sink = pl.BlockSpec((None, G * WINDOW_BLOCK, 1), lambda g, i: (g, 0, 0))
    bias = pl.BlockSpec((None, G * WINDOW_BLOCK, n_ctx + 3 * WINDOW_BLOCK),
                        lambda g, i: (jnp.where(i < ncb, 3, jnp.where(i == ncb, 1, jnp.where(i == nqb - 1, 2, 0))),
                                      0, 0))
    return [q, kc, loc(-1, H), loc(0, H), loc(1, H), vc, loc(-1, H + KV), loc(0, H + KV), loc(1, H + KV), sink, bias]


def _attn_bias(n_ctx):
    nq, nk = GQA_GROUP * WINDOW_BLOCK, n_ctx + 3 * WINDOW_BLOCK
    r = (jnp.arange(nq) % WINDOW_BLOCK)[:, None]
    col = jnp.arange(nk)[None, :]
    blk = (col - n_ctx) // WINDOW_BLOCK
    rk = (col - n_ctx) % WINDOW_BLOCK
    is_ctx = jnp.broadcast_to(col < n_ctx, (nq, nk))
    prev = (blk == 0) & (rk >= r)
    cur = jnp.broadcast_to(blk == 1, (nq, nk))
    nxt = (blk == 2) & (rk <= r)
    valid = jnp.stack([is_ctx | prev | cur | nxt, is_ctx | cur | nxt, is_ctx | prev | cur, is_ctx])
    return jnp.where(valid, 0.0, NEG_INF).astype(F32)


def _stack_heads(x):
    return jnp.concatenate([x[:, h * HEAD_DIM:(h + 1) * HEAD_DIM] for h in range(GQA_GROUP)], axis=0)


def _attn_probs(q_ref, kc_ref, kp_ref, kcur_ref, kn_ref, sink_ref, bias_ref):
    qs = _stack_heads(q_ref[...]).astype(BF16)
    k = jnp.concatenate([kc_ref[...], kp_ref[...], kcur_ref[...], kn_ref[...]], axis=0).astype(BF16)
    s = _dot(qs, k, ((1,), (1,))) * (HEAD_DIM ** -0.5) + bias_ref[...]
    sk = sink_ref[...]
    m = jnp.maximum(jnp.max(s, axis=1, keepdims=True), sk)
    e = jnp.exp(s - m)
    es = jnp.exp(sk - m)
    inv = 1.0 / (jnp.sum(e, axis=1, keepdims=True) + es)
    return qs, k, e * inv, es * inv


def _attn_fwd(qkv, sink_col, H, KV, n_ctx, name):
    rows = qkv.shape[0]
    nqb = rows // WINDOW_BLOCK
    assert (rows - n_ctx) // WINDOW_BLOCK >= 2
    G = GQA_GROUP

    def body(q_ref, kc_ref, kp_ref, kcur_ref, kn_ref, vc_ref, vp_ref, vcur_ref, vn_ref, sink_ref, bias_ref, o_ref):
        _, _, p, _ = _attn_probs(q_ref, kc_ref, kp_ref, kcur_ref, kn_ref, sink_ref, bias_ref)
        v = jnp.concatenate([vc_ref[...], vp_ref[...], vcur_ref[...], vn_ref[...]], axis=0).astype(BF16)
        o = _dot(p, v, ((1,), (0,)))
        for h in range(G):
            o_ref[:, h * HEAD_DIM:(h + 1) * HEAD_DIM] = o[h * WINDOW_BLOCK:(h + 1) * WINDOW_BLOCK, :].astype(BF16)

    return pl.pallas_call(
        body, name=name, grid=(KV, nqb),
        in_specs=_attn_in_specs(H, KV, n_ctx, nqb),
        out_specs=pl.BlockSpec((WINDOW_BLOCK, G * HEAD_DIM), lambda g, i: (i, g)),
        out_shape=jax.ShapeDtypeStruct((rows, H * HEAD_DIM), BF16),
        compiler_params=_cparams(("parallel", "arbitrary")),
    )(*([qkv] * 9), sink_col, _attn_bias(n_ctx))


def _attn_bwd(qkv, sink_col, o, do, H, KV, n_ctx, name):
    rows = qkv.shape[0]
    nqb = rows // WINDOW_BLOCK
    G = GQA_GROUP
    WB = WINDOW_BLOCK

    def body(q_ref, kc_ref, kp_ref, kcur_ref, kn_ref, vc_ref, vp_ref, vcur_ref, vn_ref, sink_ref, bias_ref, o_ref,
             do_ref, dq_ref, dkc_ref, dvc_ref, dkp_ref, dkcur_ref, dkn_ref, dvp_ref, dvcur_ref, dvn_ref, dsink_ref):
        i = pl.program_id(1)
        qs, k, p, ps = _attn_probs(q_ref, kc_ref, kp_ref, kcur_ref, kn_ref, sink_ref, bias_ref)
        v = jnp.concatenate([vc_ref[...], vp_ref[...], vcur_ref[...], vn_ref[...]], axis=0).astype(BF16)
        do_s = _stack_heads(do_ref[...]).astype(F32)
        o_s = _stack_heads(o_ref[...]).astype(F32)
        delta = jnp.sum(do_s * o_s, axis=1, keepdims=True)
        dp = _dot(do_s, v, ((1,), (1,)))
        ds = p * (dp - delta) * (HEAD_DIM ** -0.5)
        dq = _dot(ds, k, ((1,), (0,)))
        dk = _dot(ds, qs, ((0,), (0,)))
        dv = _dot(p, do_s, ((0,), (0,)))
        for h in range(G):
            dq_ref[:, h * HEAD_DIM:(h + 1) * HEAD_DIM] = dq[h * WB:(h + 1) * WB, :]

        @pl.when(i == 0)
        def _():
            dkc_ref[...] = jnp.zeros_like(dkc_ref)
            dvc_ref[...] = jnp.zeros_like(dvc_ref)
            dsink_ref[...] = jnp.zeros_like(dsink_ref)

        dkc_ref[...] += dk[:n_ctx]
        dvc_ref[...] += dv[:n_ctx]
        dsink_ref[...] += -ps * delta
        for j, (dk_ref, dv_ref) in enumerate(((dkp_ref, dvp_ref), (dkcur_ref, dvcur_ref), (dkn_ref, dvn_ref))):
            dk_ref[...] = dk[n_ctx + j * WB:n_ctx + (j + 1) * WB]
            dv_ref[...] = dv[n_ctx + j * WB:n_ctx + (j + 1) * WB]

    qblk = pl.BlockSpec((WB, G * HEAD_DIM), lambda g, i: (i, g))
    ctx = pl.BlockSpec((n_ctx, HEAD_DIM), lambda g, i: (0, g))
    piece = pl.BlockSpec((WB, HEAD_DIM), lambda g, i: (i, g))
    sink = pl.BlockSpec((None, G * WB, 1), lambda g, i: (g, 0, 0))
    kv_shape = jax.ShapeDtypeStruct((rows, KV * HEAD_DIM), F32)
    ctx_shape = jax.ShapeDtypeStruct((n_ctx, KV * HEAD_DIM), F32)
    return pl.pallas_call(
        body, name=name, grid=(KV, nqb),
        in_specs=_attn_in_specs(H, KV, n_ctx, nqb) + [qblk, qblk],
        out_specs=[qblk, ctx, ctx] + [piece] * 6 + [sink],
        out_shape=[jax.ShapeDtypeStruct((rows, H * HEAD_DIM), F32), ctx_shape, ctx_shape] + [kv_shape] * 6 +
                  [jax.ShapeDtypeStruct(sink_col.shape, F32)],
        compiler_params=_cparams(("parallel", "arbitrary")),
    )(*([qkv] * 9), sink_col, _attn_bias(n_ctx), o, do)


def _shift_blocks(x, n_ctx, delta):
    lat = x[n_ctx:]
    z = jnp.zeros((WINDOW_BLOCK, x.shape[1]), x.dtype)
    if delta == 1:
        lat = jnp.concatenate([z, lat[:-WINDOW_BLOCK]], axis=0)
    elif delta == -1:
        lat = jnp.concatenate([lat[WINDOW_BLOCK:], z], axis=0)
    return jnp.concatenate([jnp.zeros((n_ctx, x.shape[1]), x.dtype), lat], axis=0)


def attention(qkv, sink_col, H, KV, n_ctx, name):
    @jax.custom_vjp
    def op(qkv, sink_col):
        return _attn_fwd(qkv, sink_col, H, KV, n_ctx, name + "_fwd")

    def fwd(qkv, sink_col):
        o = _attn_fwd(qkv, sink_col, H, KV, n_ctx, name + "_fwd")
        return o, (qkv, sink_col, o)

    def bwd(res, do):
        qkv, sink_col, o = res
        dq, dkc, dvc, dkp, dkcur, dkn, dvp, dvcur, dvn, dsink = _attn_bwd(qkv, sink_col, o, do, H, KV, n_ctx,
                                                                          name + "_bwd")

        def gather_pieces(prev, cur, nxt, ctx):
            pad = jnp.concatenate([ctx, jnp.zeros((qkv.shape[0] - n_ctx, ctx.shape[1]), F32)], axis=0)
            return rowwise(name + "_kvsum", lambda a, b, c, d: (a + b + c + d,),
                           [cur, _shift_blocks(prev, n_ctx, -1), _shift_blocks(nxt, n_ctx, 1), pad], [], [], 1, 256)[0]

        dk = gather_pieces(dkp, dkcur, dkn, dkc)
        dv = gather_pieces(dvp, dvcur, dvn, dvc)
        return jnp.concatenate([dq, dk, dv], axis=1), dsink

    op.defvjp(fwd, bwd)
    return op(qkv, sink_col)


ROW_BLOCK = 256


def _rms_norm(x, g):
    return (x * lax.rsqrt(jnp.mean(x * x, axis=-1, keepdims=True) + NORM_EPS)) * g


def _modulate_f(x, g, shift, scale):
    return ((_rms_norm(x, g) * (1.0 + scale) + shift).astype(BF16),)


def _expm1(x):
    series = x * (1 + x / 2 * (1 + x / 3 * (1 + x / 4 * (1 + x / 5 * (1 + x / 6)))))
    return jnp.where(jnp.abs(x) < 0.1, series, jnp.exp(x) - 1.0)


def _lru_gates_f(ra, rx, uc, ba, bx, sp):
    r = jax.nn.sigmoid(ra + ba)
    ig = jax.nn.sigmoid(rx + bx)
    log_a = -LRU_C * r * sp
    return jnp.exp(log_a), jnp.sqrt(-_expm1(2.0 * log_a)) * (ig * uc)


def _ln_silu_f(z, g, b):
    mu = jnp.mean(z, axis=-1, keepdims=True)
    var = jnp.mean(jnp.square(z - mu), axis=-1, keepdims=True)
    return (jax.nn.silu((z - mu) * lax.rsqrt(var + NORM_EPS) * g + b).astype(BF16),)


def _loss_f(x, target, g):
    err = _rms_norm(x, g) - target
    return (jnp.sum(0.5 * err * err, axis=0, keepdims=True) / x.shape[1],)


def _modulate(x, g, shift, scale):
    return rowwise("modulate", _modulate_f, [x], [g, shift, scale], ["vec", "seg", "seg"], 1, ROW_BLOCK)[0]


def _residual(x, y, gate):
    return rowwise("residual", lambda x, y, g: (x + g * y,), [x, y], [gate], ["seg"], 1, ROW_BLOCK)[0]


def _split_linear(h, w, px, name):
    return (linear(h, w[0:2], px[0:2], grad_dtype=BF16, name=name + "_a"),
            linear(h, w[2:4], px[2:4], grad_dtype=BF16, name=name + "_b"))


def _attention_mixer(h, P, PX, j, n_ctx, tabs):
    H = P["attn_w_o"][j].shape[1] // HEAD_DIM
    KV = H // GQA_GROUP
    qkv = linear(h, P["attn_w_qkv"][j], PX["attn_w_qkv"][j], grad_dtype=BF16, name="attn_qkv")
    qkv = rope(qkv, tabs[0], tabs[1], H + KV, "rope")
    sink_col = jnp.repeat(P["attn_sink"][j].reshape(KV, GQA_GROUP), WINDOW_BLOCK, axis=1)[..., None]
    o = attention(qkv, sink_col, H, KV, n_ctx, "attn")
    return linear(o, P["attn_w_o"][j], PX["attn_w_o"][j], grad_dtype=BF16, name="attn_o"), None


def _rglru_mixer(h, P, PX, j, n_ctx):
    gate, xb = _split_linear(h, P["lru_w_in"][j], PX["lru_w_in"][j], "lru_in")
    R = xb.shape[1]
    cb = _div_tile(R, 512, LANE)
    sp = jax.nn.softplus(-P["lru_lambda"][j])
    hs = []
    for d in range(2):
        K = P["lru_conv_w"][j].shape[1]
        uc = dwconv(xb, P["lru_conv_w"][j][d], P["lru_conv_b"][j][d][None], 0 if d == 1 else K - 1, n_ctx,
                    "lru_conv")
        ra = linear(uc, P["lru_wa"][j][d], P["lru_wa"][j][d], diag=True, name="lru_wa")
        rx = linear(uc, P["lru_wx"][j][d], P["lru_wx"][j][d], diag=True, name="lru_wx")
        a, bt = rowwise("lru_gates", _lru_gates_f, [ra, rx, uc],
                        [P["lru_ba"][j][d][None], P["lru_bx"][j][d][None], sp[d][None]], ["vec"] * 3, 2,
                        ROW_BLOCK, cb=cb)
        hs.append(linear_scan(a, bt, d, d == 1, "lru_scan"))
    y_in = rowwise("lru_gelu", lambda g, h0, h1: ((jax.nn.gelu(g) * (h0 + h1)).astype(BF16),), [gate, hs[0], hs[1]], [], [], 1,
                   ROW_BLOCK, cb=cb)[0]
    return linear(y_in, P["lru_w_out"][j], PX["lru_w_out"][j], grad_dtype=BF16, name="lru_out"), None


def _conformer_mixer(h, P, PX, j, n_ctx):
    z1, z2 = _split_linear(h, P["conf_w_in"][j], PX["conf_w_in"][j], "conf_in")
    Dm = z1.shape[1]
    b_in = P["conf_b_in"][j]
    z = rowwise("conf_glu", lambda a, b, ba, bb: ((a + ba) * jax.nn.sigmoid(b + bb),), [z1, z2],
                [b_in[None, :Dm], b_in[None, Dm:]], ["vec", "vec"], 1, ROW_BLOCK, cb=_div_tile(Dm, 512, LANE))[0]
    K = P["conf_dw_w"][j].shape[0]
    zc = dwconv(z, P["conf_dw_w"][j], P["conf_dw_b"][j][None], K // 2, n_ctx, "conf_conv")
    zs = rowwise("conf_ln_silu", _ln_silu_f, [zc], [P["conf_ln_g"][j][None], P["conf_ln_b"][j][None]],
                 ["vec", "vec"], 1, ROW_BLOCK)[0]
    y = linear(zs, P["conf_w_out"][j], PX["conf_w_out"][j], grad_dtype=BF16, name="conf_out")
    return y, P["conf_b_out"][j][None]


def _conv_ffn(u, P, PX, i, n_ctx):
    g, v = _split_linear(u, P["ffn_w_up"][i], PX["ffn_w_up"][i], "ffn_up")
    K = P["ffn_conv_w"][i].shape[0]
    gc = dwconv(g, P["ffn_conv_w"][i], P["ffn_conv_b"][i][None], K // 2, n_ctx, "ffn_conv")
    a = rowwise("ffn_swiglu", lambda g, v: ((jax.nn.silu(g) * v).astype(BF16),), [gc, v], [], [], 1, ROW_BLOCK,
                cb=_div_tile(g.shape[1], 1408, LANE))[0]
    return linear(a, P["ffn_w_down"][i], PX["ffn_w_down"][i], grad_dtype=BF16, name="ffn_down")


def local_loss(x_all, mods, P, PX, target, n_ctx):
    assert n_ctx == ROW_BLOCK
    depth = len(mods)
    tabs = rope_tables(n_ctx, x_all.shape[0] - n_ctx)
    x = x_all
    for i in range(depth):
        kind, j = i % 3, i // 3
        sh1, sc1, g1, sh2, sc2, g2 = mods[i]
        h = _modulate(x, P["norm_mix_g"][i][None], sh1, sc1)
        if kind == 0:
            y, bias = _attention_mixer(h, P, PX, j, n_ctx, tabs)
        elif kind == 1:
            y, bias = _rglru_mixer(h, P, PX, j, n_ctx)
        else:
            y, bias = _conformer_mixer(h, P, PX, j, n_ctx)
        if bias is None:
            x = _residual(x, y, g1)
        else:
            x = rowwise("residual_bias", lambda x, y, b, g: (x + g * (y + b),), [x, y], [bias, g1],
                        ["vec", "seg"], 1, ROW_BLOCK)[0]
        u = _modulate(x, P["norm_ffn_g"][i][None], sh2, sc2)
        x = _residual(x, _conv_ffn(u, P, PX, i, n_ctx), g2)
    per_feature = rowwise("loss_head", _loss_f, [x[n_ctx:], target], [P["final_norm_g"][None]], ["vec"], 0,
                          ROW_BLOCK, nograd=(1,))[0]
    return jnp.sum(per_feature)


def _plane_peers():
    x, y, c = lax.axis_index("x"), lax.axis_index("y"), lax.axis_index("c")
    me = 2 * x + y
    peers = [((1 - x, y, c), 2 * (1 - x) + y),
             ((x, 1 - y, c), 2 * x + (1 - y)),
             ((1 - x, 1 - y, c), 2 * (1 - x) + (1 - y))]
    return me, peers


def plane_allgather(arrays, layers, name):
    flat = []
    for k, L in enumerate(layers):
        flat += [(k, None)] if L is None else [(k, l) for l in range(L)]
    n_in, n = len(arrays), len(flat)

    def body(*refs):
        ins, outs = refs[:n_in], refs[n_in:n_in + n]
        lsem, ssem, rsem = refs[n_in + n:]
        me, peers = _plane_peers()

        def src(t):
            k, l = flat[t]
            return ins[k] if l is None else ins[k].at[l]

        def remote(t, p, slot):
            return pltpu.make_async_remote_copy(src(t), outs[t].at[slot], ssem.at[3 * t + p], rsem.at[3 * t + p],
                                                device_id=peers[p][0], device_id_type=MESH)

        local = [pltpu.make_async_copy(src(t), outs[t].at[me], lsem.at[t]) for t in range(n)]
        for t in range(n):
            local[t].start()
            for p in range(3):
                remote(t, p, me).start()
        for t in range(n):
            local[t].wait()
            for p in range(3):
                remote(t, p, peers[p][1]).wait()

    out_shape = []
    for k, l in flat:
        shp = arrays[k].shape if l is None else arrays[k].shape[1:]
        out_shape.append(jax.ShapeDtypeStruct((4,) + tuple(shp), arrays[k].dtype))
    res = pl.pallas_call(
        body, name=name, in_specs=[ANY] * n_in, out_specs=[ANY] * n, out_shape=out_shape,
        scratch_shapes=[pltpu.SemaphoreType.DMA((n,)), pltpu.SemaphoreType.DMA((3 * n,)),
                        pltpu.SemaphoreType.DMA((3 * n,))],
    )(*arrays)
    out, t = [], 0
    for L in layers:
        if L is None:
            out.append(res[t])
            t += 1
        else:
            out.append(list(res[t:t + L]))
            t += L
    return out


def plane_alltoall(groups, name):
    flat = [(k, l) for k, grp in enumerate(groups) for l in range(len(grp))]
    arrays = [a for grp in groups for a in grp]
    n, ng = len(flat), len(groups)

    def body(*refs):
        ins, outs = refs[:n], refs[n:n + ng]
        lsem, ssem, rsem = refs[n + ng:]
        me, peers = _plane_peers()

        def remote(t, p, src_slot, dst_slot):
            k, l = flat[t]
            return pltpu.make_async_remote_copy(ins[t].at[src_slot], outs[k].at[dst_slot, l], ssem.at[3 * t + p],
                                                rsem.at[3 * t + p], device_id=peers[p][0], device_id_type=MESH)

        local = [pltpu.make_async_copy(ins[t].at[me], outs[flat[t][0]].at[me, flat[t][1]], lsem.at[t])
                 for t in range(n)]
        for t in range(n):
            local[t].start()
            for p in range(3):
                remote(t, p, peers[p][1], me).start()
        for t in range(n):
            local[t].wait()
            for p in range(3):
                remote(t, p, peers[p][1], peers[p][1]).wait()

    out_shape = [jax.ShapeDtypeStruct((4, len(grp)) + tuple(grp[0].shape[1:]), grp[0].dtype) for grp in groups]
    return pl.pallas_call(
        body, name=name, in_specs=[ANY] * n, out_specs=[ANY] * ng, out_shape=out_shape,
        scratch_shapes=[pltpu.SemaphoreType.DMA((n,)), pltpu.SemaphoreType.DMA((3 * n,)),
                        pltpu.SemaphoreType.DMA((3 * n,))],
    )(*arrays)


def sibling_exchange(arrays, name):
    n = len(arrays)

    def body(*refs):
        ins, outs = refs[:n], refs[n:2 * n]
        ssem, rsem = refs[2 * n:]
        sibling = (lax.axis_index("x"), lax.axis_index("y"), 1 - lax.axis_index("c"))
        copies = [pltpu.make_async_remote_copy(ins[t], outs[t], ssem.at[t], rsem.at[t], device_id=sibling,
                                               device_id_type=MESH) for t in range(n)]
        for cp in copies:
            cp.start()
        for cp in copies:
            cp.wait()

    return pl.pallas_call(
        body, name=name, in_specs=[ANY] * n, out_specs=[ANY] * n,
        out_shape=[jax.ShapeDtypeStruct(a.shape, a.dtype) for a in arrays],
        scratch_shapes=[pltpu.SemaphoreType.DMA((n,)), pltpu.SemaphoreType.DMA((n,))],
    )(*arrays)


def _sibling():
    return (lax.axis_index("x"), lax.axis_index("y"), 1 - lax.axis_index("c"))


def _rows_half(ref, h, rows, axis):
    idx = (slice(None),) * axis + (pl.ds(h * (rows // 2), rows // 2),)
    return ref.at[idx]


def plane_allgather_shared(arrays, layers, name):
    flat = []
    for k, L in enumerate(layers):
        flat += [(k, None)] if L is None else [(k, l) for l in range(L)]
    n_in, n = len(arrays), len(flat)
    shard_rows = [arrays[k].shape[0] if l is None else arrays[k].shape[1] for k, l in flat]
    assert all(r % 32 == 0 for r in shard_rows)

    def body(*refs):
        ins, outs = refs[:n_in], refs[n_in:n_in + n]
        lsem, isend, irecv, dsend, drecv = refs[n_in + n:]
        me, peers = _plane_peers()
        c = lax.axis_index("c")

        def src(t):
            k, l = flat[t]
            return ins[k] if l is None else ins[k].at[l]

        def over_ici(t, p, slot):
            return pltpu.make_async_remote_copy(
                _rows_half(src(t), c, shard_rows[t], 0), _rows_half(outs[t].at[slot], c, shard_rows[t], 0),
                isend.at[3 * t + p], irecv.at[3 * t + p], device_id=peers[p][0], device_id_type=MESH)

        def over_d2d(t, p, h):
            piece = _rows_half(outs[t].at[peers[p][1]], h, shard_rows[t], 0)
            return pltpu.make_async_remote_copy(piece, piece, dsend.at[3 * t + p], drecv.at[3 * t + p],
                                                device_id=_sibling(), device_id_type=MESH)

        local = [pltpu.make_async_copy(src(t), outs[t].at[me], lsem.at[t]) for t in range(n)]
        for t in range(n):
            local[t].start()
            for p in range(3):
                over_ici(t, p, me).start()
        for t in range(n):
            for p in range(3):
                over_ici(t, p, peers[p][1]).wait_recv()
                over_d2d(t, p, c).start()
        for t in range(n):
            local[t].wait()
            for p in range(3):
                over_d2d(t, p, 1 - c).wait_recv()
                over_ici(t, p, me).wait_send()
                over_d2d(t, p, c).wait_send()

    out_shape = []
    for k, l in flat:
        shp = arrays[k].shape if l is None else arrays[k].shape[1:]
        out_shape.append(jax.ShapeDtypeStruct((4,) + tuple(shp), arrays[k].dtype))
    res = pl.pallas_call(
        body, name=name, in_specs=[ANY] * n_in, out_specs=[ANY] * n, out_shape=out_shape,
        scratch_shapes=[pltpu.SemaphoreType.DMA((n,))] + [pltpu.SemaphoreType.DMA((3 * n,))] * 4,
    )(*arrays)
    out, t = [], 0
    for L in layers:
        if L is None:
            out.append(res[t])
            t += 1
        else:
            out.append(list(res[t:t + L]))
            t += L
    return out


def sibling_swap_halves(groups, name):
    flat = [(k, l) for k, grp in enumerate(groups) for l in range(len(grp))]
    arrays = [a for grp in groups for a in grp]
    n, ng = len(flat), len(groups)
    rows = [a.shape[1] for a in arrays]
    assert all(r % 32 == 0 for r in rows)

    def body(*refs):
        ins, mine, theirs = refs[:n], refs[n:n + ng], refs[n + ng:n + 2 * ng]
        lsem, ssem, rsem = refs[n + 2 * ng:]
        c = lax.axis_index("c")
        local, remote = [], []
        for t, (k, l) in enumerate(flat):
            local.append(pltpu.make_async_copy(_rows_half(ins[t], c, rows[t], 1), mine[k].at[:, l], lsem.at[t]))
            remote.append(pltpu.make_async_remote_copy(_rows_half(ins[t], 1 - c, rows[t], 1), theirs[k].at[:, l],
                                                       ssem.at[t], rsem.at[t], device_id=_sibling(),
                                                       device_id_type=MESH))
        for t in range(n):
            local[t].start()
            remote[t].start()
        for t in range(n):
            local[t].wait()
            remote[t].wait()

    half_shape = [jax.ShapeDtypeStruct((4, len(grp), grp[0].shape[1] // 2) + tuple(grp[0].shape[2:]), grp[0].dtype)
                  for grp in groups]
    res = pl.pallas_call(
        body, name=name, in_specs=[ANY] * n, out_specs=[ANY] * (2 * ng), out_shape=half_shape + half_shape,
        scratch_shapes=[pltpu.SemaphoreType.DMA((n,))] * 3,
    )(*arrays)
    return list(zip(res[:ng], res[ng:]))


def sibling_merge_halves(halves, name):
    n = len(halves)

    def body(*refs):
        ins, outs = refs[:n], refs[n:2 * n]
        lsem, ssem, rsem = refs[2 * n:]
        c = lax.axis_index("c")
        local, remote = [], []
        for t in range(n):
            rows = 2 * halves[t].shape[1]
            local.append(pltpu.make_async_copy(ins[t], _rows_half(outs[t], c, rows, 1), lsem.at[t]))
            remote.append((
                pltpu.make_async_remote_copy(ins[t], _rows_half(outs[t], c, rows, 1), ssem.at[t], rsem.at[t],
                                             device_id=_sibling(), device_id_type=MESH),
                pltpu.make_async_remote_copy(ins[t], _rows_half(outs[t], 1 - c, rows, 1), ssem.at[t], rsem.at[t],
                                             device_id=_sibling(), device_id_type=MESH)))
        for t in range(n):
            local[t].start()
            remote[t][0].start()
        for t in range(n):
            local[t].wait()
            remote[t][0].wait_send()
            remote[t][1].wait_recv()

    out_shape = [jax.ShapeDtypeStruct((h.shape[0], 2 * h.shape[1]) + tuple(h.shape[2:]), h.dtype) for h in halves]
    return pl.pallas_call(
        body, name=name, in_specs=[ANY] * n, out_specs=[ANY] * n, out_shape=out_shape,
        scratch_shapes=[pltpu.SemaphoreType.DMA((n,))] * 3,
    )(*halves)


def add_pair(a, b, out_dtype, name):
    shp = a.shape
    R, C = _shape2d(shp)
    tr = _row_tile(R, C, 1 << 20)

    def body(a_ref, b_ref, o_ref):
        o_ref[...] = (a_ref[...].astype(F32) + b_ref[...].astype(F32)).astype(out_dtype)

    spec = pl.BlockSpec((tr, C), lambda i: (i, 0))
    return pl.pallas_call(
        body, name=name, grid=(R // tr,), in_specs=[spec, spec], out_specs=spec,
        out_shape=jax.ShapeDtypeStruct((R, C), out_dtype), compiler_params=_cparams(("parallel",)),
    )(a.reshape(R, C), b.reshape(R, C)).reshape(shp)


def gather_all_devices(a, name):
    own = plane_allgather([a], [None], name + "_plane")[0]
    sib = sibling_exchange([own], name + "_sibling")[0]
    c = lax.axis_index("c")
    c0 = jnp.where(c == 0, own, sib)
    c1 = jnp.where(c == 0, sib, own)
    return jnp.stack([c0, c1], axis=1).reshape((8,) + a.shape)


def _shape2d(shape):
    if len(shape) >= 2 and shape[-1] % LANE == 0:
        return (math.prod(shape[:-1]), shape[-1])
    return tuple(shape) if len(shape) == 2 else (1, math.prod(shape))


def _as2d(a):
    return a.reshape(_shape2d(a.shape))


def _row_tile(R, C, budget_bytes):
    if R * C * 4 <= budget_bytes or R % SUBLANE:
        return R
    cap = max(SUBLANE, (budget_bytes // (C * 4)) // SUBLANE * SUBLANE)
    return _div_tile(R, cap, SUBLANE)


def sum_slots(x4, name):
    shp = x4.shape[1:]
    R, C = _shape2d(shp)
    v = x4.reshape(4, R, C)
    tr = _row_tile(R, C, 1 << 20)

    def body(x_ref, o_ref):
        f = lambda s: x_ref[s].astype(F32)
        o_ref[...] = ((f(0) + f(1)) + f(2)) + f(3)

    out = pl.pallas_call(
        body, name=name, grid=(R // tr,),
        in_specs=[pl.BlockSpec((4, tr, C), lambda i: (0, i, 0))],
        out_specs=pl.BlockSpec((tr, C), lambda i: (i, 0)),
        out_shape=jax.ShapeDtypeStruct((R, C), F32),
        compiler_params=_cparams(("parallel",)),
    )(v)
    return out.reshape(shp)


def adamw(w, m, v, terms, name):
    shp = w.shape
    w2, m2, v2 = _as2d(w), _as2d(m), _as2d(v)
    flat = [_as2d(t) for inner in terms for t in inner]
    sizes = [len(inner) for inner in terms]
    R, C = w2.shape
    tr = _row_tile(R, C, 1 << 19)

    def body(*refs):
        w_ref, m_ref, v_ref = refs[:3]
        t_refs = refs[3:3 + len(flat)]
        g_ref, d_ref, nm_ref, nv_ref = refs[3 + len(flat):]
        g, t = None, 0
        for sz in sizes:
            inner = t_refs[t][...].astype(F32)
            for q in range(1, sz):
                inner = inner + t_refs[t + q][...].astype(F32)
            t += sz
            g = inner if g is None else g + inner
        nm = ADAM_B1 * m_ref[...] + (1.0 - ADAM_B1) * g
        nv = ADAM_B2 * v_ref[...] + (1.0 - ADAM_B2) * jnp.square(g)
        m_hat = nm / (1.0 - ADAM_B1 ** ADAM_STEP)
        v_hat = nv / (1.0 - ADAM_B2 ** ADAM_STEP)
        g_ref[...] = g
        d_ref[...] = -ADAM_LR * (m_hat / (jnp.sqrt(v_hat) + ADAM_EPS) + ADAM_WD * w_ref[...])
        nm_ref[...] = nm
        nv_ref[...] = nv

    spec = pl.BlockSpec((tr, C), lambda i: (i, 0))
    outs = pl.pallas_call(
        body, name=name, grid=(R // tr,),
        in_specs=[spec] * (3 + len(flat)), out_specs=[spec] * 4,
        out_shape=[jax.ShapeDtypeStruct((R, C), F32)] * 4,
        compiler_params=_cparams(("parallel",)),
    )(w2, m2, v2, *flat)
    return tuple(o.reshape(shp) for o in outs)


def _pack(arrs):
    flat = jnp.concatenate([a.reshape(-1).astype(F32) for a in arrs])
    unit = 32 * LANE
    pad = (-flat.shape[0]) % unit
    return jnp.pad(flat, (0, pad)).reshape(-1, LANE)


def _unpack(pack, shapes, lead=()):
    flat = pack.reshape(tuple(lead) + (-1,))
    out, off = [], 0
    for s in shapes:
        n = math.prod(s)
        out.append(flat[..., off:off + n].reshape(tuple(lead) + tuple(s)))
        off += n
    return out


COL_SHARDED = ("attn_w_qkv", "lru_w_in", "conf_w_in", "ffn_w_up")
ROW_SHARDED = ("attn_w_o", "lru_w_out", "conf_w_out", "ffn_w_down")
SMALL_SHARDED = ("lru_conv_w", "lru_conv_b", "lru_ba", "lru_bx", "lru_lambda", "conf_b_in", "conf_dw_w", "conf_dw_b",
                 "conf_ln_g", "conf_ln_b", "conf_b_out", "ffn_conv_w")
REPLICATED = ("norm_mix_g", "norm_ffn_g", "attn_sink", "lru_wa", "lru_wx", "ffn_conv_b", "final_norm_g")
LOCAL_ONLY = ("c_ctx", "ada_b")
WEIGHTS = ("c_ctx", "ada_w", "ada_b", "norm_mix_g", "norm_ffn_g", "attn_w_qkv", "attn_w_o", "attn_sink", "lru_w_in",
           "lru_conv_w", "lru_conv_b", "lru_wa", "lru_ba", "lru_wx", "lru_bx", "lru_lambda", "lru_w_out", "conf_w_in",
           "conf_b_in", "conf_dw_w", "conf_dw_b", "conf_ln_g", "conf_ln_b", "conf_w_out", "conf_b_out", "ffn_w_up",
           "ffn_conv_w", "ffn_conv_b", "ffn_w_down", "final_norm_g")


def _full_last_axis(g4):
    moved = jnp.moveaxis(g4, 0, -2)
    return moved.reshape(moved.shape[:-2] + (-1,))


def _shards_last_axis(full):
    split = full.reshape(full.shape[:-1] + (4, full.shape[-1] // 4))
    return jnp.moveaxis(split, -2, 0)


def _train_step(W, M, V, x, c, ctx, loss_target):
    n_ctx = ctx.shape[1]
    depth = W["ada_w"].shape[0]
    D = x.shape[-1]
    my_c = lax.axis_index("c")
    my_s = 2 * lax.axis_index("x") + lax.axis_index("y")
    my_dev = 2 * my_s + my_c

    big = COL_SHARDED + ROW_SHARDED
    small_shapes = [W[k].shape for k in SMALL_SHARDED]
    small_pack = _pack([W[k] for k in SMALL_SHARDED])
    gathered = plane_allgather_shared([W[k].astype(BF16) for k in big] + [small_pack],
                                      [W[k].shape[0] for k in big] + [None], "gather_weights")
    P, PX = {}, {}
    for k, per_layer in zip(big, gathered[:len(big)]):
        if k in ROW_SHARDED:
            per_layer = [g.reshape(1, -1, g.shape[-1]) for g in per_layer]
        P[k] = per_layer
        PX[k] = [jnp.zeros(g.shape, BF16) for g in per_layer]
    for k, g4 in zip(SMALL_SHARDED, _unpack(gathered[-1], small_shapes, lead=(4,))):
        P[k] = _full_last_axis(g4)
    for k in REPLICATED:
        P[k] = W[k]

    c8 = gather_all_devices(c, "gather_c").reshape(8, D)
    cond16 = jnp.concatenate([c8, jnp.broadcast_to(W["c_ctx"][None], (8, D))], axis=0)
    act16, act_vjp = jax.vjp(jax.nn.silu, cond16)
    n_ada = W["ada_w"].shape[-1]
    m_shard = matmul_nn(act16, W["ada_w"], name="ada_fwd")
    m4 = plane_allgather([m_shard], [None], "gather_ada")[0].reshape(4, 16, depth, n_ada)
    m_full = m4.transpose(2, 1, 0, 3).reshape(depth, 16, 4 * n_ada) + W["ada_b"][:, None, :]
    m_lat = lax.dynamic_index_in_dim(m_full, my_dev, axis=1, keepdims=False)
    m_ctx = m_full[:, 8]
    mods = [tuple(jnp.stack([a, b])[:, None, :] for a, b in zip(jnp.split(m_ctx[i], 6), jnp.split(m_lat[i], 6)))
            for i in range(depth)]

    x_all = jnp.concatenate([ctx[0], x[0]], axis=0)
    loss, (gx, gmods, gP, gPX) = jax.value_and_grad(
        lambda xa, md, p, px: local_loss(xa, md, p, px, loss_target[0], n_ctx), argnums=(0, 1, 2, 3))(x_all, mods, P, PX)
    loss = lax.psum(loss, ("x", "y", "c"))
    grad_x = gx[n_ctx:][None]

    dm_mine = jnp.stack([jnp.stack([jnp.concatenate([g[r, 0] for g in gmods[i]]) for i in range(depth)])
                         for r in range(2)])
    dm8 = gather_all_devices(dm_mine, "gather_dmod")
    dm16 = jnp.concatenate([dm8[:, 1], dm8[:, 0]], axis=0).transpose(1, 0, 2)
    grad_ada_b = rowwise("ada_b_grad", lambda a: (jnp.sum(a, axis=0, keepdims=True),),
                         [dm16.transpose(1, 0, 2).reshape(16, -1)], [], [], 0, 16)[0].reshape(depth, -1)
    dm_cols = lax.dynamic_slice_in_dim(dm16, my_s * n_ada, n_ada, axis=2)
    dm_cols = dm_cols.transpose(1, 0, 2).reshape(16, depth * n_ada)
    grad_ada_w = matmul_tn(act16, dm_cols, depth, name="ada_dw")
    dact_part = matmul_nt(dm_cols, W["ada_w"], name="ada_dx")
    dact4 = plane_allgather([dact_part], [None], "gather_dact")[0]
    dact = ((dact4[0] + dact4[1]) + dact4[2]) + dact4[3]
    grad_c_ctx = jnp.sum(act_vjp(dact)[0][8:], axis=0)

    out = {}
    out["ada_w"] = adamw(W["ada_w"], M["ada_w"], V["ada_w"], [[grad_ada_w]], "adamw_ada_w")
    local_shapes = [W[k].shape for k in LOCAL_ONLY]
    res = adamw(_pack([W[k] for k in LOCAL_ONLY]), _pack([M[k] for k in LOCAL_ONLY]), _pack([V[k] for k in LOCAL_ONLY]),
                [[_pack([grad_c_ctx, grad_ada_b])]], "adamw_local")
    for k, vals in zip(LOCAL_ONLY, zip(*[_unpack(r, local_shapes) for r in res])):
        out[k] = vals

    groups = []
    for k in big:
        pieces = gPX[k]
        if k in ROW_SHARDED:
            pieces = [g.reshape((4, -1, g.shape[-1])) for g in pieces]
        groups.append(pieces)
    small_grads = _pack_shards([_shards_last_axis(gP[k]) for k in SMALL_SHARDED])
    groups.append([small_grads])
    swapped = sibling_swap_halves(groups, "swap_grad_halves")
    chip_sums = [add_pair(mine, theirs, mine.dtype, "chip_sum") for mine, theirs in swapped]
    received = plane_alltoall([[cs] for cs in chip_sums], "scatter_grads")
    half_sums = [sum_slots(r, "plane_sum")[0] for r in received]
    full = sibling_merge_halves(half_sums, "merge_grad_halves")
    for k, g in zip(big, full[:len(big)]):
        out[k] = adamw(W[k], M[k], V[k], [[g]], "adamw_" + k)
    res = adamw(small_pack, _pack([M[k] for k in SMALL_SHARDED]), _pack([V[k] for k in SMALL_SHARDED]),
                [[full[-1][0]]], "adamw_small")
    for k, vals in zip(SMALL_SHARDED, zip(*[_unpack(r, small_shapes) for r in res])):
        out[k] = vals

    rep_shapes = [W[k].shape for k in REPLICATED]
    own = plane_allgather([_pack([gP[k] for k in REPLICATED])], [None], "gather_rep_grads")[0]
    sib = sibling_exchange([own], "sibling_rep_grads")[0]
    res = adamw(_pack([W[k] for k in REPLICATED]), _pack([M[k] for k in REPLICATED]), _pack([V[k] for k in REPLICATED]),
                [[own[s], sib[s]] for s in range(4)], "adamw_rep")
    for k, vals in zip(REPLICATED, zip(*[_unpack(r, rep_shapes) for r in res])):
        out[k] = vals

    return (loss, grad_x) + tuple(out[k][j] for j in range(4) for k in WEIGHTS)


def _pack_shards(arrs4):
    return jnp.stack([_pack([a[s] for a in arrs4]) for s in range(4)])


def kernel(x, c, ctx, c_ctx, ada_w, ada_b, norm_mix_g, norm_ffn_g, attn_w_qkv, attn_w_o, attn_sink, lru_w_in, lru_conv_w, lru_conv_b, lru_wa, lru_ba, lru_wx, lru_bx, lru_lambda, lru_w_out, conf_w_in, conf_b_in, conf_dw_w, conf_dw_b, conf_ln_g, conf_ln_b, conf_w_out, conf_b_out, ffn_w_up, ffn_conv_w, ffn_conv_b, ffn_w_down, final_norm_g, loss_target, m_c_ctx, m_ada_w, m_ada_b, m_norm_mix_g, m_norm_ffn_g, m_attn_w_qkv, m_attn_w_o, m_attn_sink, m_lru_w_in, m_lru_conv_w, m_lru_conv_b, m_lru_wa, m_lru_ba, m_lru_wx, m_lru_bx, m_lru_lambda, m_lru_w_out, m_conf_w_in, m_conf_b_in, m_conf_dw_w, m_conf_dw_b, m_conf_ln_g, m_conf_ln_b, m_conf_w_out, m_conf_b_out, m_ffn_w_up, m_ffn_conv_w, m_ffn_conv_b, m_ffn_w_down, m_final_norm_g, v_c_ctx, v_ada_w, v_ada_b, v_norm_mix_g, v_norm_ffn_g, v_attn_w_qkv, v_attn_w_o, v_attn_sink, v_lru_w_in, v_lru_conv_w, v_lru_conv_b, v_lru_wa, v_lru_ba, v_lru_wx, v_lru_bx, v_lru_lambda, v_lru_w_out, v_conf_w_in, v_conf_b_in, v_conf_dw_w, v_conf_dw_b, v_conf_ln_g, v_conf_ln_b, v_conf_w_out, v_conf_b_out, v_ffn_w_up, v_ffn_conv_w, v_ffn_conv_b, v_ffn_w_down, v_final_norm_g):
    given = dict(locals())
    W = {k: given[k] for k in WEIGHTS}
    M = {k: given["m_" + k] for k in WEIGHTS}
    V = {k: given["v_" + k] for k in WEIGHTS}
    return _train_step(W, M, V, x, c, ctx, loss_target)
```

```python
import functools
import math

import jax
import jax.numpy as jnp
from jax import lax
from jax.experimental import pallas as pl
from jax.experimental.pallas import tpu as pltpu

F32 = jnp.float32
BF16 = jnp.bfloat16

VMEM_LIMIT_BYTES = 56 * 1024 * 1024
LANE = 128
SUBLANE = 8

HEAD_DIM = 128
GQA_GROUP = 4
WINDOW_BLOCK = 128
GRID_W = 64
ROPE_THETA = 10000.0
ROPE_FREQS = HEAD_DIM // 4
LRU_BLOCK = 128
LRU_C = 8.0
NORM_EPS = 1e-6
NEG_INF = -1e30
CONV_HALO = 32

ADAM_LR = 0.001
ADAM_B1 = 0.9
ADAM_B2 = 0.999
ADAM_EPS = 1e-08
ADAM_WD = 0.01
ADAM_STEP = 10

MESH = pl.DeviceIdType.MESH
ANY = pl.BlockSpec(memory_space=pl.ANY)


def _cparams(sem):
    return pltpu.CompilerParams(dimension_semantics=sem, vmem_limit_bytes=VMEM_LIMIT_BYTES)


def _div_tile(n, cap, unit):
    if n <= unit:
        return n
    best = None
    t = unit
    while t <= min(n, cap):
        if n % t == 0:
            best = t
        t += unit
    assert best is not None, (n, cap, unit)
    return best


def _dot(a, b, dims):
    return lax.dot_general(a.astype(BF16), b.astype(BF16), (dims, ((), ())), preferred_element_type=F32)


def matmul_nn(a, b, diag=False, name="mm_nn"):
    M = a.shape[0]
    G, K, n = b.shape
    tm = _div_tile(M, 768, SUBLANE)
    tn = _div_tile(n, 1408, LANE)
    tk = K if K <= 2048 else _div_tile(K, 1408, LANE)
    nk, nn = K // tk, n // tn
    assert a.shape[1] == (G * K if diag else K)

    def body(a_ref, b_ref, o_ref, acc_ref):
        r = pl.program_id(3)
        part = _dot(a_ref[...], b_ref[...], ((1,), (0,)))
        if nk == 1:
            o_ref[...] = part
        else:
            @pl.when(r == 0)
            def _():
                acc_ref[...] = part

            @pl.when(r > 0)
            def _():
                acc_ref[...] += part

            @pl.when(r == nk - 1)
            def _():
                o_ref[...] = acc_ref[...]

    a_map = (lambda i, g, j, r: (i, g * nk + r)) if diag else (lambda i, g, j, r: (i, r))
    return pl.pallas_call(
        body, name=name,
        grid=(M // tm, G, nn, nk),
        in_specs=[pl.BlockSpec((tm, tk), a_map),
                  pl.BlockSpec((None, tk, tn), lambda i, g, j, r: (g, r, j))],
        out_specs=pl.BlockSpec((tm, tn), lambda i, g, j, r: (i, g * nn + j)),
        out_shape=jax.ShapeDtypeStruct((M, G * n), F32),
        scratch_shapes=[pltpu.VMEM((tm, tn) if nk > 1 else (SUBLANE, LANE), F32)],
        compiler_params=_cparams(("parallel", "parallel", "parallel", "arbitrary")),
    )(a, b)


def matmul_nt(dy, b, diag=False, out_dtype=F32, name="mm_nt"):
    M = dy.shape[0]
    G, K, n = b.shape
    assert dy.shape[1] == G * n
    tm = _div_tile(M, 768, SUBLANE)
    tko = _div_tile(K, 1408, LANE)
    tr = _div_tile(n, 2048, LANE)
    nr, nko = n // tr, K // tko
    steps = nr if diag else G * nr

    def body(dy_ref, b_ref, o_ref, acc_ref):
        r = pl.program_id(3)
        part = _dot(dy_ref[...], b_ref[...], ((1,), (1,)))
        if steps == 1:
            o_ref[...] = part.astype(out_dtype)
        else:
            @pl.when(r == 0)
            def _():
                acc_ref[...] = part

            @pl.when(r > 0)
            def _():
                acc_ref[...] += part

            @pl.when(r == steps - 1)
            def _():
                o_ref[...] = acc_ref[...].astype(out_dtype)

    if diag:
        grid = (M // tm, G, nko, nr)
        dy_map = lambda i, g, kk, r: (i, g * nr + r)
        b_map = lambda i, g, kk, r: (g, kk, r)
        o_map = lambda i, g, kk, r: (i, g * nko + kk)
        out_cols = G * K
    else:
        grid = (M // tm, 1, nko, G * nr)
        dy_map = lambda i, g, kk, r: (i, r)
        b_map = lambda i, g, kk, r: (r // nr, kk, r % nr)
        o_map = lambda i, g, kk, r: (i, kk)
        out_cols = K
    return pl.pallas_call(
        body, name=name,
        grid=grid,
        in_specs=[pl.BlockSpec((tm, tr), dy_map), pl.BlockSpec((None, tko, tr), b_map)],
        out_specs=pl.BlockSpec((tm, tko), o_map),
        out_shape=jax.ShapeDtypeStruct((M, out_cols), out_dtype),
        scratch_shapes=[pltpu.VMEM((tm, tko) if steps > 1 else (SUBLANE, LANE), F32)],
        compiler_params=_cparams(("parallel", "parallel", "parallel", "arbitrary")),
    )(dy, b)


def matmul_tn(a, dy, G, diag=False, out_dtype=F32, name="mm_tn"):
    M = a.shape[0]
    n = dy.shape[1] // G
    K = a.shape[1] // G if diag else a.shape[1]
    tm = _div_tile(M, 768, SUBLANE)
    tk = _div_tile(K, 1408, LANE)
    tn = _div_tile(n, 1408, LANE)
    nkb, nn, nm = K // tk, n // tn, M // tm

    def body(a_ref, dy_ref, o_ref, acc_ref):
        r = pl.program_id(3)
        part = _dot(a_ref[...], dy_ref[...], ((0,), (0,)))
        if nm == 1:
            o_ref[...] = part.astype(out_dtype)
        else:
            @pl.when(r == 0)
            def _():
                acc_ref[...] = part

            @pl.when(r > 0)
            def _():
                acc_ref[...] += part

            @pl.when(r == nm - 1)
            def _():
                o_ref[...] = acc_ref[...].astype(out_dtype)

    a_map = (lambda g, kk, j, r: (r, g * nkb + kk)) if diag else (lambda g, kk, j, r: (r, kk))
    return pl.pallas_call(
        body, name=name,
        grid=(G, nkb, nn, nm),
        in_specs=[pl.BlockSpec((tm, tk), a_map),
                  pl.BlockSpec((tm, tn), lambda g, kk, j, r: (r, g * nn + j))],
        out_specs=pl.BlockSpec((None, tk, tn), lambda g, kk, j, r: (g, kk, j)),
        out_shape=jax.ShapeDtypeStruct((G, K, n), out_dtype),
        scratch_shapes=[pltpu.VMEM((tk, tn) if nm > 1 else (SUBLANE, LANE), F32)],
        compiler_params=_cparams(("parallel", "parallel", "parallel", "arbitrary")),
    )(a, dy)


def blockdiag_nn(a, b, transpose_b, name):
    M = a.shape[0]
    G, K, n = b.shape
    kin, kout = (n, K) if transpose_b else (K, n)
    tm = _div_tile(M, 768, SUBLANE)
    dims = ((1,), (1,)) if transpose_b else ((1,), (0,))

    def body(a_ref, b_ref, o_ref):
        for g in range(G):
            o_ref[:, g * kout:(g + 1) * kout] = _dot(a_ref[:, g * kin:(g + 1) * kin], b_ref[g], dims)

    return pl.pallas_call(
        body, name=name, grid=(M // tm,),
        in_specs=[pl.BlockSpec((tm, G * kin), lambda i: (i, 0)), pl.BlockSpec((G, K, n), lambda i: (0, 0, 0))],
        out_specs=pl.BlockSpec((tm, G * kout), lambda i: (i, 0)),
        out_shape=jax.ShapeDtypeStruct((M, G * kout), F32),
        compiler_params=_cparams(("parallel",)),
    )(a, b)


def blockdiag_tn(a, dy, G, name):
    M = a.shape[0]
    K, n = a.shape[1] // G, dy.shape[1] // G
    tm = _div_tile(M, 768, SUBLANE)

    def body(a_ref, dy_ref, o_ref):
        @pl.when(pl.program_id(0) == 0)
        def _():
            o_ref[...] = jnp.zeros_like(o_ref)

        for g in range(G):
            o_ref[g] += _dot(a_ref[:, g * K:(g + 1) * K], dy_ref[:, g * n:(g + 1) * n], ((0,), (0,)))

    return pl.pallas_call(
        body, name=name, grid=(M // tm,),
        in_specs=[pl.BlockSpec((tm, G * K), lambda i: (i, 0)), pl.BlockSpec((tm, G * n), lambda i: (i, 0))],
        out_specs=pl.BlockSpec((G, K, n), lambda i: (0, 0, 0)),
        out_shape=jax.ShapeDtypeStruct((G, K, n), F32),
        compiler_params=_cparams(("arbitrary",)),
    )(a, dy)


def blockdiag_linear(a, w, name):
    G = w.shape[0]

    @jax.custom_vjp
    def op(a, w):
        return blockdiag_nn(a, w, False, name + "_fwd")

    def fwd(a, w):
        return blockdiag_nn(a, w, False, name + "_fwd"), (a, w)

    def bwd(res, dy):
        a, w = res
        return blockdiag_nn(dy, w, True, name + "_dx"), blockdiag_tn(a, dy, G, name + "_dw")

    op.defvjp(fwd, bwd)
    return op(a, w)


def linear(a, w, w_grad_proxy, diag=False, grad_dtype=F32, name="lin"):
    G = w.shape[0]

    @jax.custom_vjp
    def op(a, w, proxy):
        return matmul_nn(a, w, diag, name + "_fwd")

    def fwd(a, w, proxy):
        return matmul_nn(a, w, diag, name + "_fwd"), (a, w)

    def bwd(res, dy):
        a, w = res
        da = matmul_nt(dy, w, diag, a.dtype, name + "_dx")
        dw = matmul_tn(a, dy, G, diag, grad_dtype, name + "_dw")
        return da, None, dw

    op.defvjp(fwd, bwd)
    return op(a, w, w_grad_proxy)


def rowwise(name, f, blocks, params, pkinds, n_out, tb, cb=None, nograd=()):
    rows = blocks[0].shape[0]
    tb = min(tb, rows)
    assert rows % tb == 0
    nrb = rows // tb
    nb, npar = len(blocks), len(params)
    widths = [b.shape[1] for b in blocks]
    if cb is None:
        ncb = 1
        bw = widths
    else:
        assert all(w == widths[0] for w in widths) and widths[0] % cb == 0
        ncb = widths[0] // cb
        bw = [cb] * nb
    pw = []
    for p, kind in zip(params, pkinds):
        full = p.shape[-1]
        pw.append(full if cb is None or full != widths[0] else cb)
    for p, kind in zip(params, pkinds):
        assert p.shape[:-1] == ((1,) if kind == "vec" else (2, 1)), (name, p.shape, kind)

    shapes = jax.eval_shape(
        f, *[jax.ShapeDtypeStruct((tb, w), b.dtype) for w, b in zip(bw, blocks)],
        *[jax.ShapeDtypeStruct((1, w), p.dtype) for w, p in zip(pw, params)])
    shapes = tuple(shapes)
    out_sh, acc_sh = shapes[:n_out], shapes[n_out:]
    n_acc = len(acc_sh)
    for s in out_sh:
        assert s.shape[0] == tb
    for s in acc_sh:
        assert s.shape[0] == 1

    def blk_spec(w):
        return pl.BlockSpec((tb, w), lambda c, r: (r, c))

    def par_spec(w, kind, full):
        col = (lambda c: c) if (cb is not None and full == widths[0]) else (lambda c: 0)
        if kind == "vec":
            return pl.BlockSpec((1, w), lambda c, r: (0, col(c)))
        return pl.BlockSpec((None, 1, w), lambda c, r: (jnp.minimum(r, 1), 0, col(c)))

    blk_specs = [blk_spec(w) for w in bw]
    par_specs = [par_spec(w, k, p.shape[-1]) for w, k, p in zip(pw, pkinds, params)]
    sem = _cparams(("parallel", "arbitrary"))

    def run_fwd(blocks, params):
        def body(*refs):
            ins = [r[...] for r in refs[:nb + npar]]
            outs = refs[nb + npar:]
            res = f(*ins)
            r_id = pl.program_id(1)
            for o_ref, val in zip(outs[:n_out], res[:n_out]):
                o_ref[...] = val
            for o_ref, val in zip(outs[n_out:], res[n_out:]):
                @pl.when(r_id == 0)
                def _(o_ref=o_ref, val=val):
                    o_ref[...] = val

                @pl.when(r_id > 0)
                def _(o_ref=o_ref, val=val):
                    o_ref[...] += val

        out_specs = [blk_spec(s.shape[1]) for s in out_sh] + \
                    [pl.BlockSpec((1, s.shape[1]), lambda c, r: (0, c)) for s in acc_sh]
        out_shape = [jax.ShapeDtypeStruct((rows, s.shape[1] * ncb), s.dtype) for s in out_sh] + \
                    [jax.ShapeDtypeStruct((1, s.shape[1] * ncb), s.dtype) for s in acc_sh]
        return tuple(pl.pallas_call(
            body, name=name + "_fwd", grid=(ncb, nrb),
            in_specs=blk_specs + par_specs, out_specs=out_specs, out_shape=out_shape,
            compiler_params=sem)(*blocks, *params))

    def run_bwd(blocks, params, cts):
        d_outs, d_accs = list(cts[:n_out]), list(cts[n_out:])
        want = [i for i in range(nb) if i not in nograd]

        def body(*refs):
            k = nb + npar
            ins = [r[...] for r in refs[:k]]
            ct = tuple(r[...] for r in refs[k:k + n_out + n_acc])
            outs = refs[k + n_out + n_acc:]
            _, vjp = jax.vjp(lambda *a: tuple(f(*a)), *ins)
            grads = vjp(ct)
            r_id = pl.program_id(1)
            for o_ref, i in zip(outs[:len(want)], want):
                o_ref[...] = grads[i].astype(o_ref.dtype)
            for o_ref, g, kind in zip(outs[len(want):], grads[nb:], pkinds):
                first = (r_id == 0) if kind == "vec" else (r_id <= 1)

                @pl.when(first)
                def _(o_ref=o_ref, g=g):
                    o_ref[...] = g.astype(o_ref.dtype)

                @pl.when(jnp.logical_not(first))
                def _(o_ref=o_ref, g=g):
                    o_ref[...] += g.astype(o_ref.dtype)

        ct_specs = [blk_spec(s.shape[1]) for s in out_sh] + \
                   [pl.BlockSpec((1, s.shape[1]), lambda c, r: (0, c)) for s in acc_sh]
        out_specs = [blk_specs[i] for i in want] + par_specs
        out_shape = [jax.ShapeDtypeStruct(blocks[i].shape, blocks[i].dtype) for i in want] + \
                    [jax.ShapeDtypeStruct(p.shape, p.dtype) for p in params]
        res = pl.pallas_call(
            body, name=name + "_bwd", grid=(ncb, nrb),
            in_specs=blk_specs + par_specs + ct_specs, out_specs=out_specs, out_shape=out_shape,
            compiler_params=sem)(*blocks, *params, *d_outs, *d_accs)
        d_blocks = [None] * nb
        for i, g in zip(want, res[:len(want)]):
            d_blocks[i] = g
        return tuple(d_blocks), tuple(res[len(want):])

    @jax.custom_vjp
    def op(blocks, params):
        return run_fwd(blocks, params)

    def op_fwd(blocks, params):
        return run_fwd(blocks, params), (blocks, params)

    def op_bwd(res, cts):
        return run_bwd(res[0], res[1], cts)

    op.defvjp(op_fwd, op_bwd)
    return op(tuple(blocks), tuple(params))


def _conv_window(prev_ref, cur_ref, next_ref, win_ref, i, tb, rows, n_ctx):
    starts = (i == 0) | (i * tb == n_ctx)
    ends = ((i + 1) * tb == rows) | ((i + 1) * tb == n_ctx)
    win_ref[pl.ds(0, CONV_HALO), :] = jnp.where(starts, 0.0, prev_ref[...])
    win_ref[pl.ds(CONV_HALO, tb), :] = cur_ref[...]
    win_ref[pl.ds(CONV_HALO + tb, CONV_HALO), :] = jnp.where(ends, 0.0, next_ref[...])


def _conv_specs(rows, tb, cb):
    hb = tb // CONV_HALO
    last = rows // CONV_HALO - 1
    prev = pl.BlockSpec((CONV_HALO, cb), lambda c, i: (jnp.maximum(i * hb - 1, 0), c))
    cur = pl.BlockSpec((tb, cb), lambda c, i: (i, c))
    nxt = pl.BlockSpec((CONV_HALO, cb), lambda c, i: (jnp.minimum((i + 1) * hb, last), c))
    return prev, cur, nxt


def _dwconv_apply(x, w, b, pad_left, n_ctx, name):
    rows, C = x.shape
    K = w.shape[0]
    tb = 256
    cb = _div_tile(C, 512, LANE)
    assert rows % tb == 0 and n_ctx % tb == 0 and K - 1 <= CONV_HALO

    def body(prev_ref, cur_ref, next_ref, w_ref, b_ref, o_ref, win_ref):
        i = pl.program_id(1)
        _conv_window(prev_ref, cur_ref, next_ref, win_ref, i, tb, rows, n_ctx)
        acc = b_ref[...] + win_ref[pl.ds(CONV_HALO - pad_left, tb), :] * w_ref[pl.ds(0, 1), :]
        for k in range(1, K):
            acc = acc + win_ref[pl.ds(CONV_HALO + k - pad_left, tb), :] * w_ref[pl.ds(k, 1), :]
        o_ref[...] = acc

    prev, cur, nxt = _conv_specs(rows, tb, cb)
    return pl.pallas_call(
        body, name=name, grid=(C // cb, rows // tb),
        in_specs=[prev, cur, nxt, pl.BlockSpec((K, cb), lambda c, i: (0, c)),
                  pl.BlockSpec((1, cb), lambda c, i: (0, c))],
        out_specs=pl.BlockSpec((tb, cb), lambda c, i: (i, c)),
        out_shape=jax.ShapeDtypeStruct((rows, C), F32),
        scratch_shapes=[pltpu.VMEM((tb + 2 * CONV_HALO, cb), F32)],
        compiler_params=_cparams(("parallel", "arbitrary")),
    )(x, x, x, w, b)


def _dwconv_wgrad(x, dy, K, pad_left, n_ctx, name):
    rows, C = x.shape
    tb = 256
    cb = _div_tile(C, 512, LANE)

    def body(prev_ref, cur_ref, next_ref, dy_ref, dw_ref, db_ref, win_ref):
        i = pl.program_id(1)
        _conv_window(prev_ref, cur_ref, next_ref, win_ref, i, tb, rows, n_ctx)
        dy = dy_ref[...]

        @pl.when(i == 0)
        def _():
            dw_ref[...] = jnp.zeros_like(dw_ref)
            db_ref[...] = jnp.zeros_like(db_ref)

        db_ref[...] += jnp.sum(dy, axis=0, keepdims=True)
        for k in range(K):
            tap = win_ref[pl.ds(CONV_HALO + k - pad_left, tb), :]
            dw_ref[pl.ds(k, 1), :] += jnp.sum(tap * dy, axis=0, keepdims=True)

    prev, cur, nxt = _conv_specs(rows, tb, cb)
    return pl.pallas_call(
        body, name=name, grid=(C // cb, rows // tb),
        in_specs=[prev, cur, nxt, pl.BlockSpec((tb, cb), lambda c, i: (i, c))],
        out_specs=[pl.BlockSpec((K, cb), lambda c, i: (0, c)), pl.BlockSpec((1, cb), lambda c, i: (0, c))],
        out_shape=[jax.ShapeDtypeStruct((K, C), F32), jax.ShapeDtypeStruct((1, C), F32)],
        scratch_shapes=[pltpu.VMEM((tb + 2 * CONV_HALO, cb), F32)],
        compiler_params=_cparams(("parallel", "arbitrary")),
    )(x, x, x, dy)


def dwconv(x, w, b, pad_left, n_ctx, name):
    K = w.shape[0]

    @jax.custom_vjp
    def op(x, w, b):
        return _dwconv_apply(x, w, b, pad_left, n_ctx, name + "_fwd")

    def fwd(x, w, b):
        return _dwconv_apply(x, w, b, pad_left, n_ctx, name + "_fwd"), (x, w)

    def bwd(res, dy):
        x, w = res
        dx = _dwconv_apply(dy, w[::-1], jnp.zeros((1, w.shape[1]), F32), K - 1 - pad_left, n_ctx, name + "_dx")
        dw, db = _dwconv_wgrad(x, dy, K, pad_left, n_ctx, name + "_dw")
        return dx, dw, db

    op.defvjp(fwd, bwd)
    return op(x, w, b)


GLU_HALO = 16


def _glu_specs(rows, tb, cb):
    hb = tb // GLU_HALO
    last = rows // GLU_HALO - 1
    prev = pl.BlockSpec((GLU_HALO, cb), lambda c, i: (jnp.maximum(i * hb - 1, 0), c))
    cur = pl.BlockSpec((tb, cb), lambda c, i: (i, c))
    nxt = pl.BlockSpec((GLU_HALO, cb), lambda c, i: (jnp.minimum((i + 1) * hb, last), c))
    return [prev, cur, nxt]


def _glu_window(prev_ref, cur_ref, next_ref, win_ref, i, tb, rows, n_ctx):
    starts = (i == 0) | (i * tb == n_ctx)
    ends = ((i + 1) * tb == rows) | ((i + 1) * tb == n_ctx)
    win_ref[pl.ds(0, GLU_HALO), :] = jnp.where(starts, 0.0, prev_ref[...].astype(F32))
    win_ref[pl.ds(GLU_HALO, tb), :] = cur_ref[...].astype(F32)
    win_ref[pl.ds(GLU_HALO + tb, GLU_HALO), :] = jnp.where(ends, 0.0, next_ref[...].astype(F32))


def _glu_conv(gwin_ref, w_ref, b_ref, start, size):
    acc = b_ref[...] + gwin_ref[pl.ds(start - 1, size), :] * w_ref[pl.ds(0, 1), :]
    for k in (1, 2):
        acc = acc + gwin_ref[pl.ds(start - 1 + k, size), :] * w_ref[pl.ds(k, 1), :]
    return acc


def _ffn_glu_fwd(g, v, w, b, n_ctx, name):
    rows, C = g.shape
    tb = ROW_BLOCK
    cb = _div_tile(C, 1408, LANE)
    assert rows % tb == 0 and n_ctx % tb == 0 and w.shape[0] == 3

    def body(gp_ref, gc_ref, gn_ref, v_ref, w_ref, b_ref, o_ref, gwin_ref):
        i = pl.program_id(1)
        _glu_window(gp_ref, gc_ref, gn_ref, gwin_ref, i, tb, rows, n_ctx)
        gate = _glu_conv(gwin_ref, w_ref, b_ref, GLU_HALO, tb)
        o_ref[...] = (jax.nn.silu(gate) * v_ref[...]).astype(BF16)

    blk = pl.BlockSpec((tb, cb), lambda c, i: (i, c))
    return pl.pallas_call(
        body, name=name, grid=(C // cb, rows // tb),
        in_specs=_glu_specs(rows, tb, cb) + [blk, pl.BlockSpec((3, cb), lambda c, i: (0, c)),
                                             pl.BlockSpec((1, cb), lambda c, i: (0, c))],
        out_specs=blk, out_shape=jax.ShapeDtypeStruct((rows, C), BF16),
        scratch_shapes=[pltpu.VMEM((tb + 2 * GLU_HALO, cb), F32)],
        compiler_params=_cparams(("parallel", "arbitrary")),
    )(g, g, g, v, w, b)


def _ffn_glu_bwd(g, v, w, b, da, n_ctx, name):
    rows, C = g.shape
    tb = ROW_BLOCK
    cb = _div_tile(C, 1408, LANE)
    ext = tb + GLU_HALO
    lo = GLU_HALO // 2

    def body(gp_ref, gc_ref, gn_ref, vp_ref, vc_ref, vn_ref, dp_ref, dc_ref, dn_ref, w_ref, b_ref,
             dg_ref, dv_ref, dw_ref, db_ref, gwin_ref, vwin_ref, dawin_ref, dgate_ref):
        i = pl.program_id(1)
        _glu_window(gp_ref, gc_ref, gn_ref, gwin_ref, i, tb, rows, n_ctx)
        _glu_window(vp_ref, vc_ref, vn_ref, vwin_ref, i, tb, rows, n_ctx)
        _glu_window(dp_ref, dc_ref, dn_ref, dawin_ref, i, tb, rows, n_ctx)
        gate = _glu_conv(gwin_ref, w_ref, b_ref, lo, ext)
        sig = jax.nn.sigmoid(gate)
        da = dawin_ref[pl.ds(lo, ext), :]
        dgate = da * vwin_ref[pl.ds(lo, ext), :] * (sig * (1.0 + gate * (1.0 - sig)))
        dgate_ref[...] = dgate
        dv_ref[...] = (da * (gate * sig))[lo:lo + tb]
        dg = dgate_ref[pl.ds(lo + 1, tb), :] * w_ref[pl.ds(0, 1), :]
        for k in (1, 2):
            dg = dg + dgate_ref[pl.ds(lo + 1 - k, tb), :] * w_ref[pl.ds(k, 1), :]
        dg_ref[...] = dg

        @pl.when(i == 0)
        def _():
            dw_ref[...] = jnp.zeros_like(dw_ref)
            db_ref[...] = jnp.zeros_like(db_ref)

        own = dgate[lo:lo + tb]
        db_ref[...] += jnp.sum(own, axis=0, keepdims=True)
        for k in range(3):
            dw_ref[pl.ds(k, 1), :] += jnp.sum(own * gwin_ref[pl.ds(GLU_HALO + k - 1, tb), :], axis=0, keepdims=True)

    blk = pl.BlockSpec((tb, cb), lambda c, i: (i, c))
    specs = _glu_specs(rows, tb, cb)
    win = pltpu.VMEM((tb + 2 * GLU_HALO, cb), F32)
    return pl.pallas_call(
        body, name=name, grid=(C // cb, rows // tb),
        in_specs=specs * 3 + [pl.BlockSpec((3, cb), lambda c, i: (0, c)), pl.BlockSpec((1, cb), lambda c, i: (0, c))],
        out_specs=[blk, blk, pl.BlockSpec((3, cb), lambda c, i: (0, c)), pl.BlockSpec((1, cb), lambda c, i: (0, c))],
        out_shape=[jax.ShapeDtypeStruct((rows, C), F32)] * 2 + [jax.ShapeDtypeStruct((3, C), F32),
                                                                jax.ShapeDtypeStruct((1, C), F32)],
        scratch_shapes=[win, win, win, pltpu.VMEM((ext, cb), F32)],
        compiler_params=_cparams(("parallel", "arbitrary")),
    )(g, g, g, v, v, v, da, da, da, w, b)


def ffn_glu(g, v, w, b, n_ctx, name):
    @jax.custom_vjp
    def op(g, v, w, b):
        return _ffn_glu_fwd(g, v, w, b, n_ctx, name + "_fwd")

    def fwd(g, v, w, b):
        return _ffn_glu_fwd(g, v, w, b, n_ctx, name + "_fwd"), (g, v, w, b)

    def bwd(res, da):
        g, v, w, b = res
        return tuple(_ffn_glu_bwd(g, v, w, b, da, n_ctx, name + "_bwd"))

    op.defvjp(fwd, bwd)
    return op(g, v, w, b)


def _block_scan(a, b, reverse):
    tb = a.shape[0]
    row = lax.broadcasted_iota(jnp.int32, (tb, 1), 0)
    s = 1
    while s < tb:
        if reverse:
            keep = row < tb - s
            a_sh = pltpu.roll(a, tb - s, 0)
            b_sh = pltpu.roll(b, tb - s, 0)
        else:
            keep = row >= s
            a_sh = pltpu.roll(a, s, 0)
            b_sh = pltpu.roll(b, s, 0)
        b = jnp.where(keep, a * b_sh + b, b)
        a = jnp.where(keep, a * a_sh, a)
        s *= 2
    return a, b


def _shift_in(h, carry, reverse):
    tb = h.shape[0]
    row = lax.broadcasted_iota(jnp.int32, (tb, 1), 0)
    if reverse:
        return jnp.where(row == tb - 1, carry, pltpu.roll(h, tb - 1, 0))
    return jnp.where(row == 0, carry, pltpu.roll(h, 1, 0))


def _scan_maps(nrb, rot, reverse):
    def phys(c, i):
        q = (nrb - 1 - i) if reverse else i
        return (lax.rem(q + rot, nrb), c)
    return phys


def _scan_fwd(a, b, rot, reverse, name):
    rows, C = a.shape
    tb = 256
    cb = _div_tile(C, 512, LANE)
    nrb = rows // tb
    last_row = 0 if reverse else tb - 1

    def body(a_ref, b_ref, h_ref, hp_ref, carry_ref):
        i = pl.program_id(1)

        @pl.when(i == 0)
        def _():
            carry_ref[...] = jnp.zeros_like(carry_ref)

        carry = carry_ref[pl.ds(0, 1), :]
        A, B = _block_scan(a_ref[...], b_ref[...], reverse)
        h = A * carry + B
        h_ref[...] = h
        hp_ref[...] = _shift_in(h, carry, reverse)
        carry_ref[pl.ds(0, 1), :] = h[last_row:last_row + 1, :]

    spec = pl.BlockSpec((tb, cb), _scan_maps(nrb, rot, reverse))
    return pl.pallas_call(
        body, name=name, grid=(C // cb, nrb),
        in_specs=[spec, spec], out_specs=[spec, spec],
        out_shape=[jax.ShapeDtypeStruct((rows, C), F32)] * 2,
        scratch_shapes=[pltpu.VMEM((SUBLANE, cb), F32)],
        compiler_params=_cparams(("parallel", "arbitrary")),
    )(a, b)


def _scan_bwd(a, dh, h_prev, rot, reverse, name):
    rows, C = a.shape
    tb = 256
    cb = _div_tile(C, 512, LANE)
    nrb = rows // tb
    adj = not reverse
    last_row = 0 if adj else tb - 1

    def body(a_ref, dh_ref, hp_ref, da_ref, db_ref, carry_ref):
        i = pl.program_id(1)

        @pl.when(i == 0)
        def _():
            carry_ref[...] = jnp.zeros_like(carry_ref)

        carry = carry_ref[pl.ds(0, 1), :]
        a = a_ref[...]
        dh = dh_ref[...]
        A, B = _block_scan(a, a * dh, adj)
        u = A * carry + B
        g = dh + _shift_in(u, carry, adj)
        db_ref[...] = g
        da_ref[...] = g * hp_ref[...]
        carry_ref[pl.ds(0, 1), :] = u[last_row:last_row + 1, :]

    spec = pl.BlockSpec((tb, cb), _scan_maps(nrb, rot, adj))
    return pl.pallas_call(
        body, name=name, grid=(C // cb, nrb),
        in_specs=[spec, spec, spec], out_specs=[spec, spec],
        out_shape=[jax.ShapeDtypeStruct((rows, C), F32)] * 2,
        scratch_shapes=[pltpu.VMEM((SUBLANE, cb), F32)],
        compiler_params=_cparams(("parallel", "arbitrary")),
    )(a, dh, h_prev)


def linear_scan(a, b, rot, reverse, name):
    @jax.custom_vjp
    def op(a, b):
        return _scan_fwd(a, b, rot, reverse, name + "_fwd")[0]

    def fwd(a, b):
        h, hp = _scan_fwd(a, b, rot, reverse, name + "_fwd")
        return h, (a, hp)

    def bwd(res, dh):
        a, hp = res
        da, db = _scan_bwd(a, dh, hp, rot, reverse, name + "_bwd")
        return da, db

    op.defvjp(fwd, bwd)
    return op(a, b)


def rope_tables(n_ctx, n_lat):
    t = jnp.arange(n_lat)
    pos = jnp.stack([t // GRID_W, t % GRID_W], axis=-1).astype(F32)
    freq = ROPE_THETA ** (-jnp.arange(ROPE_FREQS, dtype=F32) / ROPE_FREQS)
    ang = pos[:, :, None] * freq
    cos, sin = jnp.cos(ang), jnp.sin(ang)
    c = jnp.concatenate([cos[:, 0], cos[:, 0], cos[:, 1], cos[:, 1]], axis=-1)
    s = jnp.concatenate([-sin[:, 0], sin[:, 0], -sin[:, 1], sin[:, 1]], axis=-1)
    c = jnp.concatenate([jnp.ones((n_ctx, HEAD_DIM), F32), c], axis=0)
    s = jnp.concatenate([jnp.zeros((n_ctx, HEAD_DIM), F32), s], axis=0)
    return c, s


def _rope_apply(qkv, c_tab, s_tab, n_rot_heads, name):
    rows, cols = qkv.shape
    tb = _div_tile(rows, 768, SUBLANE)
    heads = cols // HEAD_DIM
    hb = max(h for h in (4, 2, 1) if heads % h == 0 and n_rot_heads % h == 0)
    wb = hb * HEAD_DIM

    def body(x_ref, c_ref, s_ref, o_ref):
        x = x_ref[...]
        lane = lax.broadcasted_iota(jnp.int32, x.shape, 1)
        swapped = jnp.where((lane & 63) < 32, pltpu.roll(x, wb - 32, 1), pltpu.roll(x, 32, 1))
        roped = x * jnp.tile(c_ref[...], (1, hb)) + swapped * jnp.tile(s_ref[...], (1, hb))
        o_ref[...] = jnp.where(pl.program_id(1) * hb < n_rot_heads, roped, x)

    tab = pl.BlockSpec((tb, HEAD_DIM), lambda i, j: (i, 0))
    blk = pl.BlockSpec((tb, wb), lambda i, j: (i, j))
    return pl.pallas_call(
        body, name=name, grid=(rows // tb, cols // wb),
        in_specs=[blk, tab, tab], out_specs=blk,
        out_shape=jax.ShapeDtypeStruct((rows, cols), F32),
        compiler_params=_cparams(("parallel", "arbitrary")),
    )(qkv, c_tab, s_tab)


def rope(qkv, c_tab, s_tab, n_rot_heads, name):
    @jax.custom_vjp
    def op(qkv):
        return _rope_apply(qkv, c_tab, s_tab, n_rot_heads, name + "_fwd")

    def fwd(qkv):
        return _rope_apply(qkv, c_tab, s_tab, n_rot_heads, name + "_fwd"), None

    def bwd(_, d):
        return (_rope_apply(d, c_tab, -s_tab, n_rot_heads, name + "_bwd"),)

    op.defvjp(fwd, bwd)
    return op(qkv)


def _attn_in_specs(H, KV, n_ctx, nqb):
    ncb = n_ctx // WINDOW_BLOCK
    G = GQA_GROUP

    def loc(delta, col0):
        return pl.BlockSpec((WINDOW_BLOCK, HEAD_DIM),
                            lambda g, i: (jnp.clip(i + delta, ncb, nqb - 1), col0 + g))

    q = pl.BlockSpec((WINDOW_BLOCK, G * HEAD_DIM), lambda g, i: (i, g))
    kc = pl.BlockSpec((n_ctx, HEAD_DIM), lambda g, i: (0, H + g))
    vc = pl.BlockSpec((n_ctx, HEAD_DIM), lambda g, i: (0, H + KV + g))
    sink = pl.BlockSpec((None, G * WINDOW_BLOCK, 1), lambda g, i: (g, 0, 0))
    bias = pl.BlockSpec((None, G * WINDOW_BLOCK, n_ctx + 3 * WINDOW_BLOCK),
                        lambda g, i: (jnp.where(i < ncb, 3, jnp.where(i == ncb, 1, jnp.where(i == nqb - 1, 2, 0))),
                                      0, 0))
    return [q, kc, loc(-1, H), loc(0, H), loc(1, H), vc, loc(-1, H + KV), loc(0, H + KV), loc(1, H + KV), sink, bias]


def _attn_bias(n_ctx):
    nq, nk = GQA_GROUP * WINDOW_BLOCK, n_ctx + 3 * WINDOW_BLOCK
    r = (jnp.arange(nq) % WINDOW_BLOCK)[:, None]
    col = jnp.arange(nk)[None, :]
    blk = (col - n_ctx) // WINDOW_BLOCK
    rk = (col - n_ctx) % WINDOW_BLOCK
    is_ctx = jnp.broadcast_to(col < n_ctx, (nq, nk))
    prev = (blk == 0) & (rk >= r)
    cur = jnp.broadcast_to(blk == 1, (nq, nk))
    nxt = (blk == 2) & (rk <= r)
    valid = jnp.stack([is_ctx | prev | cur | nxt, is_ctx | cur | nxt, is_ctx | prev | cur, is_ctx])
    return jnp.where(valid, 0.0, NEG_INF).astype(F32)


def _stack_heads(x):
    return jnp.concatenate([x[:, h * HEAD_DIM:(h + 1) * HEAD_DIM] for h in range(GQA_GROUP)], axis=0)


def _attn_probs(q_ref, kc_ref, kp_ref, kcur_ref, kn_ref, sink_ref, bias_ref):
    qs = _stack_heads(q_ref[...]).astype(BF16)
    k = jnp.concatenate([kc_ref[...], kp_ref[...], kcur_ref[...], kn_ref[...]], axis=0).astype(BF16)
    s = _dot(qs, k, ((1,), (1,))) * (HEAD_DIM ** -0.5) + bias_ref[...]
    sk = sink_ref[...]
    m = jnp.maximum(jnp.max(s, axis=1, keepdims=True), sk)
    e = jnp.exp(s - m)
    es = jnp.exp(sk - m)
    inv = 1.0 / (jnp.sum(e, axis=1, keepdims=True) + es)
    return qs, k, e * inv, es * inv


def _attn_fwd(qkv, sink_col, H, KV, n_ctx, name):
    rows = qkv.shape[0]
    nqb = rows // WINDOW_BLOCK
    assert (rows - n_ctx) // WINDOW_BLOCK >= 2
    G = GQA_GROUP

    def body(q_ref, kc_ref, kp_ref, kcur_ref, kn_ref, vc_ref, vp_ref, vcur_ref, vn_ref, sink_ref, bias_ref, o_ref):
        _, _, p, _ = _attn_probs(q_ref, kc_ref, kp_ref, kcur_ref, kn_ref, sink_ref, bias_ref)
        v = jnp.concatenate([vc_ref[...], vp_ref[...], vcur_ref[...], vn_ref[...]], axis=0).astype(BF16)
        o = _dot(p, v, ((1,), (0,)))
        for h in range(G):
            o_ref[:, h * HEAD_DIM:(h + 1) * HEAD_DIM] = o[h * WINDOW_BLOCK:(h + 1) * WINDOW_BLOCK, :].astype(BF16)

    return pl.pallas_call(
        body, name=name, grid=(KV, nqb),
        in_specs=_attn_in_specs(H, KV, n_ctx, nqb),
        out_specs=pl.BlockSpec((WINDOW_BLOCK, G * HEAD_DIM), lambda g, i: (i, g)),
        out_shape=jax.ShapeDtypeStruct((rows, H * HEAD_DIM), BF16),
        compiler_params=_cparams(("parallel", "arbitrary")),
    )(*([qkv] * 9), sink_col, _attn_bias(n_ctx))


def _attn_bwd(qkv, sink_col, o, do, H, KV, n_ctx, name):
    rows = qkv.shape[0]
    nqb = rows // WINDOW_BLOCK
    G = GQA_GROUP
    WB = WINDOW_BLOCK

    def body(q_ref, kc_ref, kp_ref, kcur_ref, kn_ref, vc_ref, vp_ref, vcur_ref, vn_ref, sink_ref, bias_ref, o_ref,
             do_ref, dq_ref, dkc_ref, dvc_ref, dkp_ref, dkcur_ref, dkn_ref, dvp_ref, dvcur_ref, dvn_ref, dsink_ref):
        i = pl.program_id(1)
        qs, k, p, ps = _attn_probs(q_ref, kc_ref, kp_ref, kcur_ref, kn_ref, sink_ref, bias_ref)
        v = jnp.concatenate([vc_ref[...], vp_ref[...], vcur_ref[...], vn_ref[...]], axis=0).astype(BF16)
        do_s = _stack_heads(do_ref[...]).astype(F32)
        o_s = _stack_heads(o_ref[...]).astype(F32)
        delta = jnp.sum(do_s * o_s, axis=1, keepdims=True)
        dp = _dot(do_s, v, ((1,), (1,)))
        ds = p * (dp - delta) * (HEAD_DIM ** -0.5)
        dq = _dot(ds, k, ((1,), (0,)))
        dk = _dot(ds, qs, ((0,), (0,)))
        dv = _dot(p, do_s, ((0,), (0,)))
        for h in range(G):
            dq_ref[:, h * HEAD_DIM:(h + 1) * HEAD_DIM] = dq[h * WB:(h + 1) * WB, :]

        @pl.when(i == 0)
        def _():
            dkc_ref[...] = jnp.zeros_like(dkc_ref)
            dvc_ref[...] = jnp.zeros_like(dvc_ref)
            dsink_ref[...] = jnp.zeros_like(dsink_ref)

        dkc_ref[...] += dk[:n_ctx]
        dvc_ref[...] += dv[:n_ctx]
        dsink_ref[...] += -ps * delta
        for j, (dk_ref, dv_ref) in enumerate(((dkp_ref, dvp_ref), (dkcur_ref, dvcur_ref), (dkn_ref, dvn_ref))):
            dk_ref[...] = dk[n_ctx + j * WB:n_ctx + (j + 1) * WB]
            dv_ref[...] = dv[n_ctx + j * WB:n_ctx + (j + 1) * WB]

    qblk = pl.BlockSpec((WB, G * HEAD_DIM), lambda g, i: (i, g))
    ctx = pl.BlockSpec((n_ctx, HEAD_DIM), lambda g, i: (0, g))
    piece = pl.BlockSpec((WB, HEAD_DIM), lambda g, i: (i, g))
    sink = pl.BlockSpec((None, G * WB, 1), lambda g, i: (g, 0, 0))
    kv_shape = jax.ShapeDtypeStruct((rows, KV * HEAD_DIM), F32)
    ctx_shape = jax.ShapeDtypeStruct((n_ctx, KV * HEAD_DIM), F32)
    return pl.pallas_call(
        body, name=name, grid=(KV, nqb),
        in_specs=_attn_in_specs(H, KV, n_ctx, nqb) + [qblk, qblk],
        out_specs=[qblk, ctx, ctx] + [piece] * 6 + [sink],
        out_shape=[jax.ShapeDtypeStruct((rows, H * HEAD_DIM), F32), ctx_shape, ctx_shape] + [kv_shape] * 6 +
                  [jax.ShapeDtypeStruct(sink_col.shape, F32)],
        compiler_params=_cparams(("parallel", "arbitrary")),
    )(*([qkv] * 9), sink_col, _attn_bias(n_ctx), o, do)


def _shift_blocks(x, n_ctx, delta):
    lat = x[n_ctx:]
    z = jnp.zeros((WINDOW_BLOCK, x.shape[1]), x.dtype)
    if delta == 1:
        lat = jnp.concatenate([z, lat[:-WINDOW_BLOCK]], axis=0)
    elif delta == -1:
        lat = jnp.concatenate([lat[WINDOW_BLOCK:], z], axis=0)
    return jnp.concatenate([jnp.zeros((n_ctx, x.shape[1]), x.dtype), lat], axis=0)


def attention(qkv, sink_col, H, KV, n_ctx, name):
    @jax.custom_vjp
    def op(qkv, sink_col):
        return _attn_fwd(qkv, sink_col, H, KV, n_ctx, name + "_fwd")

    def fwd(qkv, sink_col):
        o = _attn_fwd(qkv, sink_col, H, KV, n_ctx, name + "_fwd")
        return o, (qkv, sink_col, o)

    def bwd(res, do):
        qkv, sink_col, o = res
        dq, dkc, dvc, dkp, dkcur, dkn, dvp, dvcur, dvn, dsink = _attn_bwd(qkv, sink_col, o, do, H, KV, n_ctx,
                                                                          name + "_bwd")

        def gather_pieces(prev, cur, nxt, ctx):
            pad = jnp.concatenate([ctx, jnp.zeros((qkv.shape[0] - n_ctx, ctx.shape[1]), F32)], axis=0)
            return rowwise(name + "_kvsum", lambda a, b, c, d: (a + b + c + d,),
                           [cur, _shift_blocks(prev, n_ctx, -1), _shift_blocks(nxt, n_ctx, 1), pad], [], [], 1, 256)[0]

        dk = gather_pieces(dkp, dkcur, dkn, dkc)
        dv = gather_pieces(dvp, dvcur, dvn, dvc)
        return jnp.concatenate([dq, dk, dv], axis=1), dsink

    op.defvjp(fwd, bwd)
    return op(qkv, sink_col)


ROW_BLOCK = 256


def _rms_norm(x, g):
    return (x * lax.rsqrt(jnp.mean(x * x, axis=-1, keepdims=True) + NORM_EPS)) * g


def _modulate_f(x, g, shift, scale):
    return ((_rms_norm(x, g) * (1.0 + scale) + shift).astype(BF16),)


def _expm1(x):
    series = x * (1 + x / 2 * (1 + x / 3 * (1 + x / 4 * (1 + x / 5 * (1 + x / 6)))))
    return jnp.where(jnp.abs(x) < 0.1, series, jnp.exp(x) - 1.0)


def _lru_gates_f(ra, rx, uc, ba, bx, sp):
    r = jax.nn.sigmoid(ra + ba)
    ig = jax.nn.sigmoid(rx + bx)
    log_a = -LRU_C * r * sp
    return jnp.exp(log_a), jnp.sqrt(-_expm1(2.0 * log_a)) * (ig * uc)


def _ln_silu_f(z, g, b):
    mu = jnp.mean(z, axis=-1, keepdims=True)
    var = jnp.mean(jnp.square(z - mu), axis=-1, keepdims=True)
    return (jax.nn.silu((z - mu) * lax.rsqrt(var + NORM_EPS) * g + b).astype(BF16),)


def _loss_f(x, target, g):
    err = _rms_norm(x, g) - target
    return (jnp.sum(0.5 * err * err, axis=0, keepdims=True) / x.shape[1],)


def _modulate(x, g, shift, scale):
    return rowwise("modulate", _modulate_f, [x], [g, shift, scale], ["vec", "seg", "seg"], 1, ROW_BLOCK)[0]


def _residual(x, y, gate):
    return rowwise("residual", lambda x, y, g: (x + g * y,), [x, y], [gate], ["seg"], 1, ROW_BLOCK)[0]


def _split_nn(a, b, g0, name):
    M, K = a.shape
    n = b.shape[2]
    tm = _div_tile(M, 768, SUBLANE)
    tn = _div_tile(n, 1408, LANE)
    nn = n // tn

    def body(a_ref, b_ref, o_ref):
        o_ref[...] = _dot(a_ref[...], b_ref[...], ((1,), (0,)))

    return pl.pallas_call(
        body, name=name, grid=(M // tm, 2, nn),
        in_specs=[pl.BlockSpec((tm, K), lambda i, g, j: (i, 0)),
                  pl.BlockSpec((None, K, tn), lambda i, g, j: (g + g0, 0, j))],
        out_specs=pl.BlockSpec((tm, tn), lambda i, g, j: (i, g * nn + j)),
        out_shape=jax.ShapeDtypeStruct((M, 2 * n), F32),
        compiler_params=_cparams(("parallel", "parallel", "parallel")),
    )(a, b)


def _split_nt(dy_a, dy_b, b, out_dtype, name):
    M = dy_a.shape[0]
    _, K, n = b.shape
    tm = _div_tile(M, 768, SUBLANE)
    tko = _div_tile(K, 1408, LANE)
    tr = _div_tile(n, 2048, LANE)
    nr = n // tr
    half = 2 * nr

    def body(dya_ref, dyb_ref, b_ref, o_ref, acc_ref):
        r = pl.program_id(2)

        @pl.when(r == 0)
        def _():
            acc_ref[...] = _dot(dya_ref[...], b_ref[...], ((1,), (1,)))

        @pl.when((r > 0) & (r < half))
        def _():
            acc_ref[...] += _dot(dya_ref[...], b_ref[...], ((1,), (1,)))

        @pl.when(r >= half)
        def _():
            acc_ref[...] += _dot(dyb_ref[...], b_ref[...], ((1,), (1,)))

        @pl.when(r == 2 * half - 1)
        def _():
            o_ref[...] = acc_ref[...].astype(out_dtype)

    return pl.pallas_call(
        body, name=name, grid=(M // tm, K // tko, 2 * half),
        in_specs=[pl.BlockSpec((tm, tr), lambda i, kk, r: (i, jnp.minimum(r, half - 1))),
                  pl.BlockSpec((tm, tr), lambda i, kk, r: (i, jnp.maximum(r - half, 0))),
                  pl.BlockSpec((None, tko, tr), lambda i, kk, r: (r // nr, kk, lax.rem(r, nr)))],
        out_specs=pl.BlockSpec((tm, tko), lambda i, kk, r: (i, kk)),
        out_shape=jax.ShapeDtypeStruct((M, K), out_dtype),
        scratch_shapes=[pltpu.VMEM((tm, tko), F32)],
        compiler_params=_cparams(("parallel", "parallel", "arbitrary")),
    )(dy_a, dy_b, b)


def _split_tn(a, dy_a, dy_b, out_dtype, name):
    M, K = a.shape
    n = dy_a.shape[1] // 2
    tm = _div_tile(M, 768, SUBLANE)
    tk = _div_tile(K, 1408, LANE)
    tn = _div_tile(n, 1408, LANE)
    nkb, nn, nm = K // tk, n // tn, M // tm

    def body(a_ref, dya_ref, dyb_ref, o_ref, acc_ref):
        g, r = pl.program_id(0), pl.program_id(3)

        @pl.when(r == 0)
        def _():
            acc_ref[...] = jnp.zeros_like(acc_ref)

        @pl.when(g < 2)
        def _():
            acc_ref[...] += _dot(a_ref[...], dya_ref[...], ((0,), (0,)))

        @pl.when(g >= 2)
        def _():
            acc_ref[...] += _dot(a_ref[...], dyb_ref[...], ((0,), (0,)))

        @pl.when(r == nm - 1)
        def _():
            o_ref[...] = acc_ref[...].astype(out_dtype)

    return pl.pallas_call(
        body, name=name, grid=(4, nkb, nn, nm),
        in_specs=[pl.BlockSpec((tm, tk), lambda g, kk, j, r: (r, kk)),
                  pl.BlockSpec((tm, tn), lambda g, kk, j, r: (jnp.where(g < 2, r, 0), jnp.minimum(g, 1) * nn + j)),
                  pl.BlockSpec((tm, tn), lambda g, kk, j, r: (jnp.where(g >= 2, r, 0),
                                                              jnp.maximum(g - 2, 0) * nn + j))],
        out_specs=pl.BlockSpec((None, tk, tn), lambda g, kk, j, r: (g, kk, j)),
        out_shape=jax.ShapeDtypeStruct((4, K, n), out_dtype),
        scratch_shapes=[pltpu.VMEM((tk, tn), F32)],
        compiler_params=_cparams(("parallel", "parallel", "parallel", "arbitrary")),
    )(a, dy_a, dy_b)


def _split_linear(h, w, px, name):
    @jax.custom_vjp
    def op(h, w, px):
        return _split_nn(h, w, 0, name + "_a_fwd"), _split_nn(h, w, 2, name + "_b_fwd")

    def fwd(h, w, px):
        return (_split_nn(h, w, 0, name + "_a_fwd"), _split_nn(h, w, 2, name + "_b_fwd")), (h, w)

    def bwd(res, cts):
        h, w = res
        dh = _split_nt(cts[0], cts[1], w, h.dtype, name + "_dx")
        dw = _split_tn(h, cts[0], cts[1], BF16, name + "_dw")
        return dh, None, dw

    op.defvjp(fwd, bwd)
    return op(h, w, px)


def _attention_mixer(h, P, PX, j, n_ctx, tabs):
    H = P["attn_w_o"][j].shape[1] // HEAD_DIM
    KV = H // GQA_GROUP
    qkv = linear(h, P["attn_w_qkv"][j], PX["attn_w_qkv"][j], grad_dtype=BF16, name="attn_qkv")
    qkv = rope(qkv, tabs[0], tabs[1], H + KV, "rope")
    sink_col = jnp.repeat(P["attn_sink"][j].reshape(KV, GQA_GROUP), WINDOW_BLOCK, axis=1)[..., None]
    o = attention(qkv, sink_col, H, KV, n_ctx, "attn")
    return linear(o, P["attn_w_o"][j], PX["attn_w_o"][j], grad_dtype=BF16, name="attn_o"), None


def _rglru_mixer(h, P, PX, j, n_ctx):
    gate, xb = _split_linear(h, P["lru_w_in"][j], PX["lru_w_in"][j], "lru_in")
    R = xb.shape[1]
    cb = _div_tile(R, 512, LANE)
    sp = jax.nn.softplus(-P["lru_lambda"][j])
    hs = []
    for d in range(2):
        K = P["lru_conv_w"][j].shape[1]
        uc = dwconv(xb, P["lru_conv_w"][j][d], P["lru_conv_b"][j][d][None], 0 if d == 1 else K - 1, n_ctx,
                    "lru_conv")
        ra = blockdiag_linear(uc, P["lru_wa"][j][d], "lru_wa")
        rx = blockdiag_linear(uc, P["lru_wx"][j][d], "lru_wx")
        a, bt = rowwise("lru_gates", _lru_gates_f, [ra, rx, uc],
                        [P["lru_ba"][j][d][None], P["lru_bx"][j][d][None], sp[d][None]], ["vec"] * 3, 2,
                        ROW_BLOCK, cb=cb)
        hs.append(linear_scan(a, bt, d, d == 1, "lru_scan"))
    y_in = rowwise("lru_gelu", lambda g, h0, h1: ((jax.nn.gelu(g) * (h0 + h1)).astype(BF16),), [gate, hs[0], hs[1]], [], [], 1,
                   ROW_BLOCK, cb=cb)[0]
    return linear(y_in, P["lru_w_out"][j], PX["lru_w_out"][j], grad_dtype=BF16, name="lru_out"), None


def _conformer_mixer(h, P, PX, j, n_ctx):
    z1, z2 = _split_linear(h, P["conf_w_in"][j], PX["conf_w_in"][j], "conf_in")
    Dm = z1.shape[1]
    b_in = P["conf_b_in"][j]
    z = rowwise("conf_glu", lambda a, b, ba, bb: ((a + ba) * jax.nn.sigmoid(b + bb),), [z1, z2],
                [b_in[None, :Dm], b_in[None, Dm:]], ["vec", "vec"], 1, ROW_BLOCK, cb=_div_tile(Dm, 512, LANE))[0]
    K = P["conf_dw_w"][j].shape[0]
    zc = dwconv(z, P["conf_dw_w"][j], P["conf_dw_b"][j][None], K // 2, n_ctx, "conf_conv")
    zs = rowwise("conf_ln_silu", _ln_silu_f, [zc], [P["conf_ln_g"][j][None], P["conf_ln_b"][j][None]],
                 ["vec", "vec"], 1, ROW_BLOCK)[0]
    y = linear(zs, P["conf_w_out"][j], PX["conf_w_out"][j], grad_dtype=BF16, name="conf_out")
    return y, P["conf_b_out"][j][None]


def _conv_ffn(u, P, PX, i, n_ctx):
    g, v = _split_linear(u, P["ffn_w_up"][i], PX["ffn_w_up"][i], "ffn_up")
    a = ffn_glu(g, v, P["ffn_conv_w"][i], P["ffn_conv_b"][i][None], n_ctx, "ffn_glu")
    return linear(a, P["ffn_w_down"][i], PX["ffn_w_down"][i], grad_dtype=BF16, name="ffn_down")


def local_loss(x_all, mods, P, PX, target, n_ctx):
    assert n_ctx == ROW_BLOCK
    depth = len(mods)
    tabs = rope_tables(n_ctx, x_all.shape[0] - n_ctx)
    x = x_all
    for i in range(depth):
        kind, j = i % 3, i // 3
        sh1, sc1, g1, sh2, sc2, g2 = mods[i]
        h = _modulate(x, P["norm_mix_g"][i][None], sh1, sc1)
        if kind == 0:
            y, bias = _attention_mixer(h, P, PX, j, n_ctx, tabs)
        elif kind == 1:
            y, bias = _rglru_mixer(h, P, PX, j, n_ctx)
        else:
            y, bias = _conformer_mixer(h, P, PX, j, n_ctx)
        if bias is None:
            x = _residual(x, y, g1)
        else:
            x = rowwise("residual_bias", lambda x, y, b, g: (x + g * (y + b),), [x, y], [bias, g1],
                        ["vec", "seg"], 1, ROW_BLOCK)[0]
        u = _modulate(x, P["norm_ffn_g"][i][None], sh2, sc2)
        x = _residual(x, _conv_ffn(u, P, PX, i, n_ctx), g2)
    per_feature = rowwise("loss_head", _loss_f, [x[n_ctx:], target], [P["final_norm_g"][None]], ["vec"], 0,
                          ROW_BLOCK, nograd=(1,))[0]
    return jnp.sum(per_feature)


def _plane_peers():
    x, y, c = lax.axis_index("x"), lax.axis_index("y"), lax.axis_index("c")
    me = 2 * x + y
    peers = [((1 - x, y, c), 2 * (1 - x) + y),
             ((x, 1 - y, c), 2 * x + (1 - y)),
             ((1 - x, 1 - y, c), 2 * (1 - x) + (1 - y))]
    return me, peers


def plane_allgather(arrays, layers, name):
    flat = []
    for k, L in enumerate(layers):
        flat += [(k, None)] if L is None else [(k, l) for l in range(L)]
    n_in, n = len(arrays), len(flat)

    def body(*refs):
        ins, outs = refs[:n_in], refs[n_in:n_in + n]
        lsem, ssem, rsem = refs[n_in + n:]
        me, peers = _plane_peers()

        def src(t):
            k, l = flat[t]
            return ins[k] if l is None else ins[k].at[l]

        def remote(t, p, slot):
            return pltpu.make_async_remote_copy(src(t), outs[t].at[slot], ssem.at[3 * t + p], rsem.at[3 * t + p],
                                                device_id=peers[p][0], device_id_type=MESH)

        local = [pltpu.make_async_copy(src(t), outs[t].at[me], lsem.at[t]) for t in range(n)]
        for t in range(n):
            local[t].start()
            for p in range(3):
                remote(t, p, me).start()
        for t in range(n):
            local[t].wait()
            for p in range(3):
                remote(t, p, peers[p][1]).wait()

    out_shape = []
    for k, l in flat:
        shp = arrays[k].shape if l is None else arrays[k].shape[1:]
        out_shape.append(jax.ShapeDtypeStruct((4,) + tuple(shp), arrays[k].dtype))
    res = pl.pallas_call(
        body, name=name, in_specs=[ANY] * n_in, out_specs=[ANY] * n, out_shape=out_shape,
        scratch_shapes=[pltpu.SemaphoreType.DMA((n,)), pltpu.SemaphoreType.DMA((3 * n,)),
                        pltpu.SemaphoreType.DMA((3 * n,))],
    )(*arrays)
    out, t = [], 0
    for L in layers:
        if L is None:
            out.append(res[t])
            t += 1
        else:
            out.append(list(res[t:t + L]))
            t += L
    return out


def plane_alltoall(groups, name):
    flat = [(k, l) for k, grp in enumerate(groups) for l in range(len(grp))]
    arrays = [a for grp in groups for a in grp]
    n, ng = len(flat), len(groups)

    def body(*refs):
        ins, outs = refs[:n], refs[n:n + ng]
        lsem, ssem, rsem = refs[n + ng:]
        me, peers = _plane_peers()

        def remote(t, p, src_slot, dst_slot):
            k, l = flat[t]
            return pltpu.make_async_remote_copy(ins[t].at[src_slot], outs[k].at[dst_slot, l], ssem.at[3 * t + p],
                                                rsem.at[3 * t + p], device_id=peers[p][0], device_id_type=MESH)

        local = [pltpu.make_async_copy(ins[t].at[me], outs[flat[t][0]].at[me, flat[t][1]], lsem.at[t])
                 for t in range(n)]
        for t in range(n):
            local[t].start()
            for p in range(3):
                remote(t, p, peers[p][1], me).start()
        for t in range(n):
            local[t].wait()
            for p in range(3):
                remote(t, p, peers[p][1], peers[p][1]).wait()

    out_shape = [jax.ShapeDtypeStruct((4, len(grp)) + tuple(grp[0].shape[1:]), grp[0].dtype) for grp in groups]
    return pl.pallas_call(
        body, name=name, in_specs=[ANY] * n, out_specs=[ANY] * ng, out_shape=out_shape,
        scratch_shapes=[pltpu.SemaphoreType.DMA((n,)), pltpu.SemaphoreType.DMA((3 * n,)),
                        pltpu.SemaphoreType.DMA((3 * n,))],
    )(*arrays)


def sibling_exchange(arrays, name):
    n = len(arrays)

    def body(*refs):
        ins, outs = refs[:n], refs[n:2 * n]
        ssem, rsem = refs[2 * n:]
        sibling = (lax.axis_index("x"), lax.axis_index("y"), 1 - lax.axis_index("c"))
        copies = [pltpu.make_async_remote_copy(ins[t], outs[t], ssem.at[t], rsem.at[t], device_id=sibling,
                                               device_id_type=MESH) for t in range(n)]
        for cp in copies:
            cp.start()
        for cp in copies:
            cp.wait()

    return pl.pallas_call(
        body, name=name, in_specs=[ANY] * n, out_specs=[ANY] * n,
        out_shape=[jax.ShapeDtypeStruct(a.shape, a.dtype) for a in arrays],
        scratch_shapes=[pltpu.SemaphoreType.DMA((n,)), pltpu.SemaphoreType.DMA((n,))],
    )(*arrays)


def _sibling():
    return (lax.axis_index("x"), lax.axis_index("y"), 1 - lax.axis_index("c"))


def _rows_half(ref, h, rows, axis):
    idx = (slice(None),) * axis + (pl.ds(h * (rows // 2), rows // 2),)
    return ref.at[idx]


def plane_allgather_shared(arrays, layers, name):
    flat = []
    for k, L in enumerate(layers):
        flat += [(k, None)] if L is None else [(k, l) for l in range(L)]
    n_in, n = len(arrays), len(flat)
    shard_rows = [arrays[k].shape[0] if l is None else arrays[k].shape[1] for k, l in flat]
    assert all(r % 32 == 0 for r in shard_rows)

    def body(*refs):
        ins, outs = refs[:n_in], refs[n_in:n_in + n]
        isend, irecv, dsend, drecv = refs[n_in + n:]
        me, peers = _plane_peers()
        c = lax.axis_index("c")

        def src(t):
            k, l = flat[t]
            return ins[k] if l is None else ins[k].at[l]

        def over_ici(t, p, slot):
            return pltpu.make_async_remote_copy(
                _rows_half(src(t), c, shard_rows[t], 0), _rows_half(outs[t].at[slot], c, shard_rows[t], 0),
                isend.at[3 * t + p], irecv.at[3 * t + p], device_id=peers[p][0], device_id_type=MESH)

        def over_d2d(t, p, h):
            piece = _rows_half(outs[t].at[peers[p][1]], h, shard_rows[t], 0)
            return pltpu.make_async_remote_copy(piece, piece, dsend.at[3 * t + p], drecv.at[3 * t + p],
                                                device_id=_sibling(), device_id_type=MESH)

        for t in range(n):
            for p in range(3):
                over_ici(t, p, me).start()
        for t in range(n):
            for p in range(3):
                over_ici(t, p, peers[p][1]).wait_recv()
                over_d2d(t, p, c).start()
        for t in range(n):
            for p in range(3):
                over_d2d(t, p, 1 - c).wait_recv()
                over_ici(t, p, me).wait_send()
                over_d2d(t, p, c).wait_send()

    out_shape = []
    for k, l in flat:
        shp = arrays[k].shape if l is None else arrays[k].shape[1:]
        out_shape.append(jax.ShapeDtypeStruct((4,) + tuple(shp), arrays[k].dtype))
    res = pl.pallas_call(
        body, name=name, in_specs=[ANY] * n_in, out_specs=[ANY] * n, out_shape=out_shape,
        scratch_shapes=[pltpu.SemaphoreType.DMA((3 * n,))] * 4,
    )(*arrays)
    me = 2 * lax.axis_index("x") + lax.axis_index("y")
    res = [lax.dynamic_update_index_in_dim(r, arrays[k] if l is None else arrays[k][l], me, 0)
           for r, (k, l) in zip(res, flat)]
    out, t = [], 0
    for L in layers:
        if L is None:
            out.append(res[t])
            t += 1
        else:
            out.append(list(res[t:t + L]))
            t += L
    return out


def add_pair(a, b, out_dtype, name):
    shp = a.shape
    R, C = _shape2d(shp)
    tr = _row_tile(R, C, 1 << 20)

    def body(a_ref, b_ref, o_ref):
        o_ref[...] = (a_ref[...].astype(F32) + b_ref[...].astype(F32)).astype(out_dtype)

    spec = pl.BlockSpec((tr, C), lambda i: (i, 0))
    return pl.pallas_call(
        body, name=name, grid=(R // tr,), in_specs=[spec, spec], out_specs=spec,
        out_shape=jax.ShapeDtypeStruct((R, C), out_dtype), compiler_params=_cparams(("parallel",)),
    )(a.reshape(R, C), b.reshape(R, C)).reshape(shp)


def gather_all_devices(a, name):
    own = plane_allgather([a], [None], name + "_plane")[0]
    sib = sibling_exchange([own], name + "_sibling")[0]
    c = lax.axis_index("c")
    c0 = jnp.where(c == 0, own, sib)
    c1 = jnp.where(c == 0, sib, own)
    return jnp.stack([c0, c1], axis=1).reshape((8,) + a.shape)


def _shape2d(shape):
    if len(shape) >= 2 and shape[-1] % LANE == 0:
        return (math.prod(shape[:-1]), shape[-1])
    return tuple(shape) if len(shape) == 2 else (1, math.prod(shape))


def _as2d(a):
    return a.reshape(_shape2d(a.shape))


def _row_tile(R, C, budget_bytes):
    if R * C * 4 <= budget_bytes or R % SUBLANE:
        return R
    cap = max(SUBLANE, (budget_bytes // (C * 4)) // SUBLANE * SUBLANE)
    return _div_tile(R, cap, SUBLANE)


def sum_slots(x4, name):
    shp = x4.shape[1:]
    R, C = _shape2d(shp)
    v = x4.reshape(4, R, C)
    tr = _row_tile(R, C, 1 << 20)

    def body(x_ref, o_ref):
        f = lambda s: x_ref[s].astype(F32)
        o_ref[...] = ((f(0) + f(1)) + f(2)) + f(3)

    out = pl.pallas_call(
        body, name=name, grid=(R // tr,),
        in_specs=[pl.BlockSpec((4, tr, C), lambda i: (0, i, 0))],
        out_specs=pl.BlockSpec((tr, C), lambda i: (i, 0)),
        out_shape=jax.ShapeDtypeStruct((R, C), F32),
        compiler_params=_cparams(("parallel",)),
    )(v)
    return out.reshape(shp)


def adamw(w, m, v, terms, name):
    shp = w.shape
    w2, m2, v2 = _as2d(w), _as2d(m), _as2d(v)
    flat = [_as2d(t) for inner in terms for t in inner]
    sizes = [len(inner) for inner in terms]
    R, C = w2.shape
    tr = _row_tile(R, C, 1 << 19)

    def body(*refs):
        w_ref, m_ref, v_ref = refs[:3]
        t_refs = refs[3:3 + len(flat)]
        g_ref, d_ref, nm_ref, nv_ref = refs[3 + len(flat):]
        g, t = None, 0
        for sz in sizes:
            inner = t_refs[t][...].astype(F32)
            for q in range(1, sz):
                inner = inner + t_refs[t + q][...].astype(F32)
            t += sz
            g = inner if g is None else g + inner
        nm = ADAM_B1 * m_ref[...] + (1.0 - ADAM_B1) * g
        nv = ADAM_B2 * v_ref[...] + (1.0 - ADAM_B2) * jnp.square(g)
        m_hat = nm / (1.0 - ADAM_B1 ** ADAM_STEP)
        v_hat = nv / (1.0 - ADAM_B2 ** ADAM_STEP)
        g_ref[...] = g
        d_ref[...] = -ADAM_LR * (m_hat / (jnp.sqrt(v_hat) + ADAM_EPS) + ADAM_WD * w_ref[...])
        nm_ref[...] = nm
        nv_ref[...] = nv

    spec = pl.BlockSpec((tr, C), lambda i: (i, 0))
    outs = pl.pallas_call(
        body, name=name, grid=(R // tr,),
        in_specs=[spec] * (3 + len(flat)), out_specs=[spec] * 4,
        out_shape=[jax.ShapeDtypeStruct((R, C), F32)] * 4,
        compiler_params=_cparams(("parallel",)),
    )(w2, m2, v2, *flat)
    return tuple(o.reshape(shp) for o in outs)


def _pack(arrs):
    flat = jnp.concatenate([a.reshape(-1).astype(F32) for a in arrs])
    unit = 32 * LANE
    pad = (-flat.shape[0]) % unit
    return jnp.pad(flat, (0, pad)).reshape(-1, LANE)


def _unpack(pack, shapes, lead=()):
    flat = pack.reshape(tuple(lead) + (-1,))
    out, off = [], 0
    for s in shapes:
        n = math.prod(s)
        out.append(flat[..., off:off + n].reshape(tuple(lead) + tuple(s)))
        off += n
    return out


COL_SHARDED = ("attn_w_qkv", "lru_w_in", "conf_w_in", "ffn_w_up")
ROW_SHARDED = ("attn_w_o", "lru_w_out", "conf_w_out", "ffn_w_down")
SMALL_SHARDED = ("lru_conv_w", "lru_conv_b", "lru_ba", "lru_bx", "lru_lambda", "conf_b_in", "conf_dw_w", "conf_dw_b",
                 "conf_ln_g", "conf_ln_b", "conf_b_out", "ffn_conv_w")
REPLICATED = ("norm_mix_g", "norm_ffn_g", "attn_sink", "lru_wa", "lru_wx", "ffn_conv_b", "final_norm_g")
LOCAL_ONLY = ("c_ctx", "ada_b")
WEIGHTS = ("c_ctx", "ada_w", "ada_b", "norm_mix_g", "norm_ffn_g", "attn_w_qkv", "attn_w_o", "attn_sink", "lru_w_in",
           "lru_conv_w", "lru_conv_b", "lru_wa", "lru_ba", "lru_wx", "lru_bx", "lru_lambda", "lru_w_out", "conf_w_in",
           "conf_b_in", "conf_dw_w", "conf_dw_b", "conf_ln_g", "conf_ln_b", "conf_w_out", "conf_b_out", "ffn_w_up",
           "ffn_conv_w", "ffn_conv_b", "ffn_w_down", "final_norm_g")


def _full_last_axis(g4):
    moved = jnp.moveaxis(g4, 0, -2)
    return moved.reshape(moved.shape[:-2] + (-1,))


def _shards_last_axis(full):
    split = full.reshape(full.shape[:-1] + (4, full.shape[-1] // 4))
    return jnp.moveaxis(split, -2, 0)


def _train_step(W, M, V, x, c, ctx, loss_target):
    n_ctx = ctx.shape[1]
    depth = W["ada_w"].shape[0]
    D = x.shape[-1]
    my_c = lax.axis_index("c")
    my_s = 2 * lax.axis_index("x") + lax.axis_index("y")
    my_dev = 2 * my_s + my_c

    big = COL_SHARDED + ROW_SHARDED
    small_shapes = [W[k].shape for k in SMALL_SHARDED]
    small_pack = _pack([W[k] for k in SMALL_SHARDED])
    gathered = plane_allgather_shared([W[k].astype(BF16) for k in big] + [small_pack],
                                      [W[k].shape[0] for k in big] + [None], "gather_weights")
    P, PX = {}, {}
    for k, per_layer in zip(big, gathered[:len(big)]):
        if k in ROW_SHARDED:
            per_layer = [g.reshape(1, -1, g.shape[-1]) for g in per_layer]
        P[k] = per_layer
        PX[k] = [jnp.zeros(g.shape, BF16) for g in per_layer]
    for k, g4 in zip(SMALL_SHARDED, _unpack(gathered[-1], small_shapes, lead=(4,))):
        P[k] = _full_last_axis(g4)
    for k in REPLICATED:
        P[k] = W[k]

    c8 = gather_all_devices(c, "gather_c").reshape(8, D)
    cond16 = jnp.concatenate([c8, jnp.broadcast_to(W["c_ctx"][None], (8, D))], axis=0)
    act16, act_vjp = jax.vjp(jax.nn.silu, cond16)
    n_ada = W["ada_w"].shape[-1]
    m_shard = matmul_nn(act16, W["ada_w"], name="ada_fwd")
    m4 = plane_allgather([m_shard], [None], "gather_ada")[0].reshape(4, 16, depth, n_ada)
    m_full = m4.transpose(2, 1, 0, 3).reshape(depth, 16, 4 * n_ada) + W["ada_b"][:, None, :]
    m_lat = lax.dynamic_index_in_dim(m_full, my_dev, axis=1, keepdims=False)
    m_ctx = m_full[:, 8]
    mods = [tuple(jnp.stack([a, b])[:, None, :] for a, b in zip(jnp.split(m_ctx[i], 6), jnp.split(m_lat[i], 6)))
            for i in range(depth)]

    x_all = jnp.concatenate([ctx[0], x[0]], axis=0)
    loss, (gx, gmods, gP, gPX) = jax.value_and_grad(
        lambda xa, md, p, px: local_loss(xa, md, p, px, loss_target[0], n_ctx), argnums=(0, 1, 2, 3))(x_all, mods, P, PX)
    loss = lax.psum(loss, ("x", "y", "c"))
    grad_x = gx[n_ctx:][None]

    dm_mine = jnp.stack([jnp.stack([jnp.concatenate([g[r, 0] for g in gmods[i]]) for i in range(depth)])
                         for r in range(2)])
    dm8 = gather_all_devices(dm_mine, "gather_dmod")
    dm16 = jnp.concatenate([dm8[:, 1], dm8[:, 0]], axis=0).transpose(1, 0, 2)
    grad_ada_b = rowwise("ada_b_grad", lambda a: (jnp.sum(a, axis=0, keepdims=True),),
                         [dm16.transpose(1, 0, 2).reshape(16, -1)], [], [], 0, 16)[0].reshape(depth, -1)
    dm_cols = lax.dynamic_slice_in_dim(dm16, my_s * n_ada, n_ada, axis=2)
    dm_cols = dm_cols.transpose(1, 0, 2).reshape(16, depth * n_ada)
    grad_ada_w = matmul_tn(act16, dm_cols, depth, name="ada_dw")
    dact_part = matmul_nt(dm_cols, W["ada_w"], name="ada_dx")
    dact4 = plane_allgather([dact_part], [None], "gather_dact")[0]
    dact = ((dact4[0] + dact4[1]) + dact4[2]) + dact4[3]
    grad_c_ctx = jnp.sum(act_vjp(dact)[0][8:], axis=0)

    out = {}
    out["ada_w"] = adamw(W["ada_w"], M["ada_w"], V["ada_w"], [[grad_ada_w]], "adamw_ada_w")
    local_shapes = [W[k].shape for k in LOCAL_ONLY]
    res = adamw(_pack([W[k] for k in LOCAL_ONLY]), _pack([M[k] for k in LOCAL_ONLY]), _pack([V[k] for k in LOCAL_ONLY]),
                [[_pack([grad_c_ctx, grad_ada_b])]], "adamw_local")
    for k, vals in zip(LOCAL_ONLY, zip(*[_unpack(r, local_shapes) for r in res])):
        out[k] = vals

    groups = []
    for k in big:
        pieces = gPX[k]
        if k in ROW_SHARDED:
            pieces = [g.reshape((4, -1, g.shape[-1])) for g in pieces]
        groups.append(pieces)
    small_grads = _pack_shards([_shards_last_axis(gP[k]) for k in SMALL_SHARDED])
    groups.append([small_grads])
    def rows_half(a, h):
        return lax.dynamic_slice_in_dim(a, h * (a.shape[1] // 2), a.shape[1] // 2, axis=1)

    pieces = [a for grp in groups for a in grp]
    theirs = sibling_exchange([rows_half(a, 1 - my_c) for a in pieces], "swap_grad_halves")
    chip_sums = [add_pair(rows_half(a, my_c), t, a.dtype, "chip_sum") for a, t in zip(pieces, theirs)]
    chip_groups, t = [], 0
    for grp in groups:
        chip_groups.append(chip_sums[t:t + len(grp)])
        t += len(grp)
    received = plane_alltoall(chip_groups, "scatter_grads")
    half_sums = [sum_slots(r, "plane_sum") for r in received]
    other = sibling_exchange(half_sums, "merge_grad_halves")
    full = [jnp.concatenate([jnp.where(my_c == 0, mine, got), jnp.where(my_c == 0, got, mine)], axis=1)
            for mine, got in zip(half_sums, other)]
    for k, g in zip(big, full[:len(big)]):
        out[k] = adamw(W[k], M[k], V[k], [[g]], "adamw_" + k)
    res = adamw(small_pack, _pack([M[k] for k in SMALL_SHARDED]), _pack([V[k] for k in SMALL_SHARDED]),
                [[full[-1][0]]], "adamw_small")
    for k, vals in zip(SMALL_SHARDED, zip(*[_unpack(r, small_shapes) for r in res])):
        out[k] = vals

    rep_shapes = [W[k].shape for k in REPLICATED]
    own = plane_allgather([_pack([gP[k] for k in REPLICATED])], [None], "gather_rep_grads")[0]
    sib = sibling_exchange([own], "sibling_rep_grads")[0]
    res = adamw(_pack([W[k] for k in REPLICATED]), _pack([M[k] for k in REPLICATED]), _pack([V[k] for k in REPLICATED]),
                [[own[s], sib[s]] for s in range(4)], "adamw_rep")
    for k, vals in zip(REPLICATED, zip(*[_unpack(r, rep_shapes) for r in res])):
        out[k] = vals

    return (loss, grad_x) + tuple(out[k][j] for j in range(4) for k in WEIGHTS)


def _pack_shards(arrs4):
    return jnp.stack([_pack([a[s] for a in arrs4]) for s in range(4)])


def kernel(x, c, ctx, c_ctx, ada_w, ada_b, norm_mix_g, norm_ffn_g, attn_w_qkv, attn_w_o, attn_sink, lru_w_in, lru_conv_w, lru_conv_b, lru_wa, lru_ba, lru_wx, lru_bx, lru_lambda, lru_w_out, conf_w_in, conf_b_in, conf_dw_w, conf_dw_b, conf_ln_g, conf_ln_b, conf_w_out, conf_b_out, ffn_w_up, ffn_conv_w, ffn_conv_b, ffn_w_down, final_norm_g, loss_target, m_c_ctx, m_ada_w, m_ada_b, m_norm_mix_g, m_norm_ffn_g, m_attn_w_qkv, m_attn_w_o, m_attn_sink, m_lru_w_in, m_lru_conv_w, m_lru_conv_b, m_lru_wa, m_lru_ba, m_lru_wx, m_lru_bx, m_lru_lambda, m_lru_w_out, m_conf_w_in, m_conf_b_in, m_conf_dw_w, m_conf_dw_b, m_conf_ln_g, m_conf_ln_b, m_conf_w_out, m_conf_b_out, m_ffn_w_up, m_ffn_conv_w, m_ffn_conv_b, m_ffn_w_down, m_final_norm_g, v_c_ctx, v_ada_w, v_ada_b, v_norm_mix_g, v_norm_ffn_g, v_attn_w_qkv, v_attn_w_o, v_attn_sink, v_lru_w_in, v_lru_conv_w, v_lru_conv_b, v_lru_wa, v_lru_ba, v_lru_wx, v_lru_bx, v_lru_lambda, v_lru_w_out, v_conf_w_in, v_conf_b_in, v_conf_dw_w, v_conf_dw_b, v_conf_ln_g, v_conf_ln_b, v_conf_w_out, v_conf_b_out, v_ffn_w_up, v_ffn_conv_w, v_ffn_conv_b, v_ffn_w_down, v_final_norm_g):
    given = dict(locals())
    W = {k: given[k] for k in WEIGHTS}
    M = {k: given["m_" + k] for k in WEIGHTS}
    V = {k: given["v_" + k] for k in WEIGHTS}
    return _train_step(W, M, V, x, c, ctx, loss_target)
```

```python
import functools
import math

import jax
import jax.numpy as jnp
from jax import lax
from jax.experimental import pallas as pl
from jax.experimental.pallas import tpu as pltpu

F32 = jnp.float32
BF16 = jnp.bfloat16

VMEM_LIMIT_BYTES = 56 * 1024 * 1024
LANE = 128
SUBLANE = 8

HEAD_DIM = 128
GQA_GROUP = 4
WINDOW_BLOCK = 128
GRID_W = 64
ROPE_THETA = 10000.0
ROPE_FREQS = HEAD_DIM // 4
LRU_BLOCK = 128
LRU_C = 8.0
NORM_EPS = 1e-6
NEG_INF = -1e30
CONV_HALO = 32

ADAM_LR = 0.001
ADAM_B1 = 0.9
ADAM_B2 = 0.999
ADAM_EPS = 1e-08
ADAM_WD = 0.01
ADAM_STEP = 10

MESH = pl.DeviceIdType.MESH
ANY = pl.BlockSpec(memory_space=pl.ANY)


def _cparams(sem):
    return pltpu.CompilerParams(dimension_semantics=sem, vmem_limit_bytes=VMEM_LIMIT_BYTES)


def _div_tile(n, cap, unit):
    if n <= unit:
        return n
    best = None
    t = unit
    while t <= min(n, cap):
        if n % t == 0:
            best = t
        t += unit
    assert best is not None, (n, cap, unit)
    return best


def _dot(a, b, dims):
    return lax.dot_general(a.astype(BF16), b.astype(BF16), (dims, ((), ())), preferred_element_type=F32)


def matmul_nn(a, b, diag=False, name="mm_nn"):
    M = a.shape[0]
    G, K, n = b.shape
    tm = _div_tile(M, 768, SUBLANE)
    tn = _div_tile(n, 1408, LANE)
    tk = K if K <= 2048 else _div_tile(K, 1408, LANE)
    nk, nn = K // tk, n // tn
    assert a.shape[1] == (G * K if diag else K)

    def body(a_ref, b_ref, o_ref, acc_ref):
        r = pl.program_id(3)
        part = _dot(a_ref[...], b_ref[...], ((1,), (0,)))
        if nk == 1:
            o_ref[...] = part
        else:
            @pl.when(r == 0)
            def _():
                acc_ref[...] = part

            @pl.when(r > 0)
            def _():
                acc_ref[...] += part

            @pl.when(r == nk - 1)
            def _():
                o_ref[...] = acc_ref[...]

    a_map = (lambda i, g, j, r: (i, g * nk + r)) if diag else (lambda i, g, j, r: (i, r))
    return pl.pallas_call(
        body, name=name,
        grid=(M // tm, G, nn, nk),
        in_specs=[pl.BlockSpec((tm, tk), a_map),
                  pl.BlockSpec((None, tk, tn), lambda i, g, j, r: (g, r, j))],
        out_specs=pl.BlockSpec((tm, tn), lambda i, g, j, r: (i, g * nn + j)),
        out_shape=jax.ShapeDtypeStruct((M, G * n), F32),
        scratch_shapes=[pltpu.VMEM((tm, tn) if nk > 1 else (SUBLANE, LANE), F32)],
        compiler_params=_cparams(("parallel", "parallel", "parallel", "arbitrary")),
    )(a, b)


def matmul_nt(dy, b, diag=False, out_dtype=F32, name="mm_nt"):
    M = dy.shape[0]
    G, K, n = b.shape
    assert dy.shape[1] == G * n
    tm = _div_tile(M, 768, SUBLANE)
    tko = _div_tile(K, 1408, LANE)
    tr = _div_tile(n, 2048, LANE)
    nr, nko = n // tr, K // tko
    steps = nr if diag else G * nr

    def body(dy_ref, b_ref, o_ref, acc_ref):
        r = pl.program_id(3)
        part = _dot(dy_ref[...], b_ref[...], ((1,), (1,)))
        if steps == 1:
            o_ref[...] = part.astype(out_dtype)
        else:
            @pl.when(r == 0)
            def _():
                acc_ref[...] = part

            @pl.when(r > 0)
            def _():
                acc_ref[...] += part

            @pl.when(r == steps - 1)
            def _():
                o_ref[...] = acc_ref[...].astype(out_dtype)

    if diag:
        grid = (M // tm, G, nko, nr)
        dy_map = lambda i, g, kk, r: (i, g * nr + r)
        b_map = lambda i, g, kk, r: (g, kk, r)
        o_map = lambda i, g, kk, r: (i, g * nko + kk)
        out_cols = G * K
    else:
        grid = (M // tm, 1, nko, G * nr)
        dy_map = lambda i, g, kk, r: (i, r)
        b_map = lambda i, g, kk, r: (r // nr, kk, r % nr)
        o_map = lambda i, g, kk, r: (i, kk)
        out_cols = K
    return pl.pallas_call(
        body, name=name,
        grid=grid,
        in_specs=[pl.BlockSpec((tm, tr), dy_map), pl.BlockSpec((None, tko, tr), b_map)],
        out_specs=pl.BlockSpec((tm, tko), o_map),
        out_shape=jax.ShapeDtypeStruct((M, out_cols), out_dtype),
        scratch_shapes=[pltpu.VMEM((tm, tko) if steps > 1 else (SUBLANE, LANE), F32)],
        compiler_params=_cparams(("parallel", "parallel", "parallel", "arbitrary")),
    )(dy, b)


def matmul_tn(a, dy, G, diag=False, out_dtype=F32, name="mm_tn"):
    M = a.shape[0]
    n = dy.shape[1] // G
    K = a.shape[1] // G if diag else a.shape[1]
    tm = _div_tile(M, 768, SUBLANE)
    tk = _div_tile(K, 1408, LANE)
    tn = _div_tile(n, 1408, LANE)
    nkb, nn, nm = K // tk, n // tn, M // tm

    def body(a_ref, dy_ref, o_ref, acc_ref):
        r = pl.program_id(3)
        part = _dot(a_ref[...], dy_ref[...], ((0,), (0,)))
        if nm == 1:
            o_ref[...] = part.astype(out_dtype)
        else:
            @pl.when(r == 0)
            def _():
                acc_ref[...] = part

            @pl.when(r > 0)
            def _():
                acc_ref[...] += part

            @pl.when(r == nm - 1)
            def _():
                o_ref[...] = acc_ref[...].astype(out_dtype)

    a_map = (lambda g, kk, j, r: (r, g * nkb + kk)) if diag else (lambda g, kk, j, r: (r, kk))
    return pl.pallas_call(
        body, name=name,
        grid=(G, nkb, nn, nm),
        in_specs=[pl.BlockSpec((tm, tk), a_map),
                  pl.BlockSpec((tm, tn), lambda g, kk, j, r: (r, g * nn + j))],
        out_specs=pl.BlockSpec((None, tk, tn), lambda g, kk, j, r: (g, kk, j)),
        out_shape=jax.ShapeDtypeStruct((G, K, n), out_dtype),
        scratch_shapes=[pltpu.VMEM((tk, tn) if nm > 1 else (SUBLANE, LANE), F32)],
        compiler_params=_cparams(("parallel", "parallel", "parallel", "arbitrary")),
    )(a, dy)


def blockdiag_nn(a, b, transpose_b, name):
    M = a.shape[0]
    G, K, n = b.shape
    kin, kout = (n, K) if transpose_b else (K, n)
    tm = _div_tile(M, 768, SUBLANE)
    dims = ((1,), (1,)) if transpose_b else ((1,), (0,))

    def body(a_ref, b_ref, o_ref):
        for g in range(G):
            o_ref[:, g * kout:(g + 1) * kout] = _dot(a_ref[:, g * kin:(g + 1) * kin], b_ref[g], dims)

    return pl.pallas_call(
        body, name=name, grid=(M // tm,),
        in_specs=[pl.BlockSpec((tm, G * kin), lambda i: (i, 0)), pl.BlockSpec((G, K, n), lambda i: (0, 0, 0))],
        out_specs=pl.BlockSpec((tm, G * kout), lambda i: (i, 0)),
        out_shape=jax.ShapeDtypeStruct((M, G * kout), F32),
        compiler_params=_cparams(("parallel",)),
    )(a, b)


def blockdiag_tn(a, dy, G, name):
    M = a.shape[0]
    K, n = a.shape[1] // G, dy.shape[1] // G
    tm = _div_tile(M, 768, SUBLANE)

    def body(a_ref, dy_ref, o_ref):
        @pl.when(pl.program_id(0) == 0)
        def _():
            o_ref[...] = jnp.zeros_like(o_ref)

        for g in range(G):
            o_ref[g] += _dot(a_ref[:, g * K:(g + 1) * K], dy_ref[:, g * n:(g + 1) * n], ((0,), (0,)))

    return pl.pallas_call(
        body, name=name, grid=(M // tm,),
        in_specs=[pl.BlockSpec((tm, G * K), lambda i: (i, 0)), pl.BlockSpec((tm, G * n), lambda i: (i, 0))],
        out_specs=pl.BlockSpec((G, K, n), lambda i: (0, 0, 0)),
        out_shape=jax.ShapeDtypeStruct((G, K, n), F32),
        compiler_params=_cparams(("arbitrary",)),
    )(a, dy)


def blockdiag_linear(a, w, name):
    G = w.shape[0]

    @jax.custom_vjp
    def op(a, w):
        return blockdiag_nn(a, w, False, name + "_fwd")

    def fwd(a, w):
        return blockdiag_nn(a, w, False, name + "_fwd"), (a, w)

    def bwd(res, dy):
        a, w = res
        return blockdiag_nn(dy, w, True, name + "_dx"), blockdiag_tn(a, dy, G, name + "_dw")

    op.defvjp(fwd, bwd)
    return op(a, w)


def linear(a, w, w_grad_proxy, diag=False, grad_dtype=F32, name="lin"):
    G = w.shape[0]

    @jax.custom_vjp
    def op(a, w, proxy):
        return matmul_nn(a, w, diag, name + "_fwd")

    def fwd(a, w, proxy):
        return matmul_nn(a, w, diag, name + "_fwd"), (a, w)

    def bwd(res, dy):
        a, w = res
        da = matmul_nt(dy, w, diag, a.dtype, name + "_dx")
        dw = matmul_tn(a, dy, G, diag, grad_dtype, name + "_dw")
        return da, None, dw

    op.defvjp(fwd, bwd)
    return op(a, w, w_grad_proxy)


def rowwise(name, f, blocks, params, pkinds, n_out, tb, cb=None, nograd=()):
    rows = blocks[0].shape[0]
    tb = min(tb, rows)
    assert rows % tb == 0
    nrb = rows // tb
    nb, npar = len(blocks), len(params)
    widths = [b.shape[1] for b in blocks]
    if cb is None:
        ncb = 1
        bw = widths
    else:
        assert all(w == widths[0] for w in widths) and widths[0] % cb == 0
        ncb = widths[0] // cb
        bw = [cb] * nb
    pw = []
    for p, kind in zip(params, pkinds):
        full = p.shape[-1]
        pw.append(full if cb is None or full != widths[0] else cb)
    for p, kind in zip(params, pkinds):
        assert p.shape[:-1] == ((1,) if kind == "vec" else (2, 1)), (name, p.shape, kind)

    shapes = jax.eval_shape(
        f, *[jax.ShapeDtypeStruct((tb, w), F32) for w in bw],
        *[jax.ShapeDtypeStruct((1, w), p.dtype) for w, p in zip(pw, params)])
    shapes = tuple(shapes)
    out_sh, acc_sh = shapes[:n_out], shapes[n_out:]
    n_acc = len(acc_sh)
    for s in out_sh:
        assert s.shape[0] == tb
    for s in acc_sh:
        assert s.shape[0] == 1

    def blk_spec(w):
        return pl.BlockSpec((tb, w), lambda c, r: (r, c))

    def par_spec(w, kind, full):
        col = (lambda c: c) if (cb is not None and full == widths[0]) else (lambda c: 0)
        if kind == "vec":
            return pl.BlockSpec((1, w), lambda c, r: (0, col(c)))
        return pl.BlockSpec((None, 1, w), lambda c, r: (jnp.minimum(r, 1), 0, col(c)))

    blk_specs = [blk_spec(w) for w in bw]
    par_specs = [par_spec(w, k, p.shape[-1]) for w, k, p in zip(pw, pkinds, params)]
    sem = _cparams(("parallel", "arbitrary"))

    def run_fwd(blocks, params):
        def body(*refs):
            ins = [r[...].astype(F32) for r in refs[:nb + npar]]
            outs = refs[nb + npar:]
            res = f(*ins)
            r_id = pl.program_id(1)
            for o_ref, val in zip(outs[:n_out], res[:n_out]):
                o_ref[...] = val
            for o_ref, val in zip(outs[n_out:], res[n_out:]):
                @pl.when(r_id == 0)
                def _(o_ref=o_ref, val=val):
                    o_ref[...] = val

                @pl.when(r_id > 0)
                def _(o_ref=o_ref, val=val):
                    o_ref[...] += val

        out_specs = [blk_spec(s.shape[1]) for s in out_sh] + \
                    [pl.BlockSpec((1, s.shape[1]), lambda c, r: (0, c)) for s in acc_sh]
        out_shape = [jax.ShapeDtypeStruct((rows, s.shape[1] * ncb), s.dtype) for s in out_sh] + \
                    [jax.ShapeDtypeStruct((1, s.shape[1] * ncb), s.dtype) for s in acc_sh]
        return tuple(pl.pallas_call(
            body, name=name + "_fwd", grid=(ncb, nrb),
            in_specs=blk_specs + par_specs, out_specs=out_specs, out_shape=out_shape,
            compiler_params=sem)(*blocks, *params))

    def run_bwd(blocks, params, cts):
        d_outs, d_accs = list(cts[:n_out]), list(cts[n_out:])
        want = [i for i in range(nb) if i not in nograd]

        def body(*refs):
            k = nb + npar
            ins = [r[...].astype(F32) for r in refs[:k]]
            ct = tuple(r[...] for r in refs[k:k + n_out + n_acc])
            outs = refs[k + n_out + n_acc:]
            _, vjp = jax.vjp(lambda *a: tuple(f(*a)), *ins)
            grads = vjp(ct)
            r_id = pl.program_id(1)
            for o_ref, i in zip(outs[:len(want)], want):
                o_ref[...] = grads[i].astype(o_ref.dtype)
            for o_ref, g, kind in zip(outs[len(want):], grads[nb:], pkinds):
                first = (r_id == 0) if kind == "vec" else (r_id <= 1)

                @pl.when(first)
                def _(o_ref=o_ref, g=g):
                    o_ref[...] = g.astype(o_ref.dtype)

                @pl.when(jnp.logical_not(first))
                def _(o_ref=o_ref, g=g):
                    o_ref[...] += g.astype(o_ref.dtype)

        ct_specs = [blk_spec(s.shape[1]) for s in out_sh] + \
                   [pl.BlockSpec((1, s.shape[1]), lambda c, r: (0, c)) for s in acc_sh]
        out_specs = [blk_specs[i] for i in want] + par_specs
        out_shape = [jax.ShapeDtypeStruct(blocks[i].shape, blocks[i].dtype) for i in want] + \
                    [jax.ShapeDtypeStruct(p.shape, p.dtype) for p in params]
        res = pl.pallas_call(
            body, name=name + "_bwd", grid=(ncb, nrb),
            in_specs=blk_specs + par_specs + ct_specs, out_specs=out_specs, out_shape=out_shape,
            compiler_params=sem)(*blocks, *params, *d_outs, *d_accs)
        d_blocks = [None] * nb
        for i, g in zip(want, res[:len(want)]):
            d_blocks[i] = g
        return tuple(d_blocks), tuple(res[len(want):])

    @jax.custom_vjp
    def op(blocks, params):
        return run_fwd(blocks, params)

    def op_fwd(blocks, params):
        return run_fwd(blocks, params), (blocks, params)

    def op_bwd(res, cts):
        return run_bwd(res[0], res[1], cts)

    op.defvjp(op_fwd, op_bwd)
    return op(tuple(blocks), tuple(params))


def _conv_window(prev_ref, cur_ref, next_ref, win_ref, i, tb, rows, n_ctx):
    starts = (i == 0) | (i * tb == n_ctx)
    ends = ((i + 1) * tb == rows) | ((i + 1) * tb == n_ctx)
    win_ref[pl.ds(0, CONV_HALO), :] = jnp.where(starts, 0.0, prev_ref[...].astype(F32))
    win_ref[pl.ds(CONV_HALO, tb), :] = cur_ref[...].astype(F32)
    win_ref[pl.ds(CONV_HALO + tb, CONV_HALO), :] = jnp.where(ends, 0.0, next_ref[...].astype(F32))


def _conv_specs(rows, tb, cb):
    hb = tb // CONV_HALO
    last = rows // CONV_HALO - 1
    prev = pl.BlockSpec((CONV_HALO, cb), lambda c, i: (jnp.maximum(i * hb - 1, 0), c))
    cur = pl.BlockSpec((tb, cb), lambda c, i: (i, c))
    nxt = pl.BlockSpec((CONV_HALO, cb), lambda c, i: (jnp.minimum((i + 1) * hb, last), c))
    return prev, cur, nxt


def _dwconv_apply(x, w, b, pad_left, n_ctx, out_dtype, name):
    rows, C = x.shape
    K = w.shape[0]
    tb = 256
    cb = _div_tile(C, 512, LANE)
    assert rows % tb == 0 and n_ctx % tb == 0 and K - 1 <= CONV_HALO

    def body(prev_ref, cur_ref, next_ref, w_ref, b_ref, o_ref, win_ref):
        i = pl.program_id(1)
        _conv_window(prev_ref, cur_ref, next_ref, win_ref, i, tb, rows, n_ctx)
        acc = b_ref[...] + win_ref[pl.ds(CONV_HALO - pad_left, tb), :] * w_ref[pl.ds(0, 1), :]
        for k in range(1, K):
            acc = acc + win_ref[pl.ds(CONV_HALO + k - pad_left, tb), :] * w_ref[pl.ds(k, 1), :]
        o_ref[...] = acc.astype(out_dtype)

    prev, cur, nxt = _conv_specs(rows, tb, cb)
    return pl.pallas_call(
        body, name=name, grid=(C // cb, rows // tb),
        in_specs=[prev, cur, nxt, pl.BlockSpec((K, cb), lambda c, i: (0, c)),
                  pl.BlockSpec((1, cb), lambda c, i: (0, c))],
        out_specs=pl.BlockSpec((tb, cb), lambda c, i: (i, c)),
        out_shape=jax.ShapeDtypeStruct((rows, C), out_dtype),
        scratch_shapes=[pltpu.VMEM((tb + 2 * CONV_HALO, cb), F32)],
        compiler_params=_cparams(("parallel", "arbitrary")),
    )(x, x, x, w, b)


def _dwconv_wgrad(x, dy, K, pad_left, n_ctx, name):
    rows, C = x.shape
    tb = 256
    cb = _div_tile(C, 512, LANE)

    def body(prev_ref, cur_ref, next_ref, dy_ref, dw_ref, db_ref, win_ref):
        i = pl.program_id(1)
        _conv_window(prev_ref, cur_ref, next_ref, win_ref, i, tb, rows, n_ctx)
        dy = dy_ref[...].astype(F32)

        @pl.when(i == 0)
        def _():
            dw_ref[...] = jnp.zeros_like(dw_ref)
            db_ref[...] = jnp.zeros_like(db_ref)

        db_ref[...] += jnp.sum(dy, axis=0, keepdims=True)
        for k in range(K):
            tap = win_ref[pl.ds(CONV_HALO + k - pad_left, tb), :]
            dw_ref[pl.ds(k, 1), :] += jnp.sum(tap * dy, axis=0, keepdims=True)

    prev, cur, nxt = _conv_specs(rows, tb, cb)
    return pl.pallas_call(
        body, name=name, grid=(C // cb, rows // tb),
        in_specs=[prev, cur, nxt, pl.BlockSpec((tb, cb), lambda c, i: (i, c))],
        out_specs=[pl.BlockSpec((K, cb), lambda c, i: (0, c)), pl.BlockSpec((1, cb), lambda c, i: (0, c))],
        out_shape=[jax.ShapeDtypeStruct((K, C), F32), jax.ShapeDtypeStruct((1, C), F32)],
        scratch_shapes=[pltpu.VMEM((tb + 2 * CONV_HALO, cb), F32)],
        compiler_params=_cparams(("parallel", "arbitrary")),
    )(x, x, x, dy)


def dwconv(x, w, b, pad_left, n_ctx, name):
    K = w.shape[0]

    @jax.custom_vjp
    def op(x, w, b):
        return _dwconv_apply(x, w, b, pad_left, n_ctx, F32, name + "_fwd")

    def fwd(x, w, b):
        return _dwconv_apply(x, w, b, pad_left, n_ctx, F32, name + "_fwd"), (x, w)

    def bwd(res, dy):
        x, w = res
        dx = _dwconv_apply(dy, w[::-1], jnp.zeros((1, w.shape[1]), F32), K - 1 - pad_left, n_ctx, x.dtype,
                           name + "_dx")
        dw, db = _dwconv_wgrad(x, dy, K, pad_left, n_ctx, name + "_dw")
        return dx, dw, db

    op.defvjp(fwd, bwd)
    return op(x, w, b)


GLU_HALO = 16


def _glu_specs(rows, tb, cb):
    hb = tb // GLU_HALO
    last = rows // GLU_HALO - 1
    prev = pl.BlockSpec((GLU_HALO, cb), lambda c, i: (jnp.maximum(i * hb - 1, 0), c))
    cur = pl.BlockSpec((tb, cb), lambda c, i: (i, c))
    nxt = pl.BlockSpec((GLU_HALO, cb), lambda c, i: (jnp.minimum((i + 1) * hb, last), c))
    return [prev, cur, nxt]


def _glu_window(prev_ref, cur_ref, next_ref, win_ref, i, tb, rows, n_ctx):
    starts = (i == 0) | (i * tb == n_ctx)
    ends = ((i + 1) * tb == rows) | ((i + 1) * tb == n_ctx)
    win_ref[pl.ds(0, GLU_HALO), :] = jnp.where(starts, 0.0, prev_ref[...].astype(F32))
    win_ref[pl.ds(GLU_HALO, tb), :] = cur_ref[...].astype(F32)
    win_ref[pl.ds(GLU_HALO + tb, GLU_HALO), :] = jnp.where(ends, 0.0, next_ref[...].astype(F32))


def _glu_conv(gwin_ref, w_ref, b_ref, start, size):
    acc = b_ref[...] + gwin_ref[pl.ds(start - 1, size), :] * w_ref[pl.ds(0, 1), :]
    for k in (1, 2):
        acc = acc + gwin_ref[pl.ds(start - 1 + k, size), :] * w_ref[pl.ds(k, 1), :]
    return acc


def _ffn_glu_fwd(g, v, w, b, n_ctx, name):
    rows, C = g.shape
    tb = ROW_BLOCK
    cb = _div_tile(C, 1408, LANE)
    assert rows % tb == 0 and n_ctx % tb == 0 and w.shape[0] == 3

    def body(gp_ref, gc_ref, gn_ref, v_ref, w_ref, b_ref, o_ref, gwin_ref):
        i = pl.program_id(1)
        _glu_window(gp_ref, gc_ref, gn_ref, gwin_ref, i, tb, rows, n_ctx)
        gate = _glu_conv(gwin_ref, w_ref, b_ref, GLU_HALO, tb)
        o_ref[...] = (jax.nn.silu(gate) * v_ref[...].astype(F32)).astype(BF16)

    blk = pl.BlockSpec((tb, cb), lambda c, i: (i, c))
    return pl.pallas_call(
        body, name=name, grid=(C // cb, rows // tb),
        in_specs=_glu_specs(rows, tb, cb) + [blk, pl.BlockSpec((3, cb), lambda c, i: (0, c)),
                                             pl.BlockSpec((1, cb), lambda c, i: (0, c))],
        out_specs=blk, out_shape=jax.ShapeDtypeStruct((rows, C), BF16),
        scratch_shapes=[pltpu.VMEM((tb + 2 * GLU_HALO, cb), F32)],
        compiler_params=_cparams(("parallel", "arbitrary")),
    )(g, g, g, v, w, b)


def _ffn_glu_bwd(g, v, w, b, da, n_ctx, name):
    rows, C = g.shape
    tb = ROW_BLOCK
    cb = _div_tile(C, 1408, LANE)
    ext = tb + GLU_HALO
    lo = GLU_HALO // 2

    def body(gp_ref, gc_ref, gn_ref, vp_ref, vc_ref, vn_ref, dp_ref, dc_ref, dn_ref, w_ref, b_ref,
             dg_ref, dv_ref, dw_ref, db_ref, gwin_ref, vwin_ref, dawin_ref, dgate_ref):
        i = pl.program_id(1)
        _glu_window(gp_ref, gc_ref, gn_ref, gwin_ref, i, tb, rows, n_ctx)
        _glu_window(vp_ref, vc_ref, vn_ref, vwin_ref, i, tb, rows, n_ctx)
        _glu_window(dp_ref, dc_ref, dn_ref, dawin_ref, i, tb, rows, n_ctx)
        gate = _glu_conv(gwin_ref, w_ref, b_ref, lo, ext)
        sig = jax.nn.sigmoid(gate)
        da = dawin_ref[pl.ds(lo, ext), :]
        dgate = da * vwin_ref[pl.ds(lo, ext), :] * (sig * (1.0 + gate * (1.0 - sig)))
        dgate_ref[...] = dgate
        dv_ref[...] = (da * (gate * sig))[lo:lo + tb].astype(dv_ref.dtype)
        dg = dgate_ref[pl.ds(lo + 1, tb), :] * w_ref[pl.ds(0, 1), :]
        for k in (1, 2):
            dg = dg + dgate_ref[pl.ds(lo + 1 - k, tb), :] * w_ref[pl.ds(k, 1), :]
        dg_ref[...] = dg.astype(dg_ref.dtype)

        @pl.when(i == 0)
        def _():
            dw_ref[...] = jnp.zeros_like(dw_ref)
            db_ref[...] = jnp.zeros_like(db_ref)

        own = dgate[lo:lo + tb]
        db_ref[...] += jnp.sum(own, axis=0, keepdims=True)
        for k in range(3):
            dw_ref[pl.ds(k, 1), :] += jnp.sum(own * gwin_ref[pl.ds(GLU_HALO + k - 1, tb), :], axis=0, keepdims=True)

    blk = pl.BlockSpec((tb, cb), lambda c, i: (i, c))
    specs = _glu_specs(rows, tb, cb)
    win = pltpu.VMEM((tb + 2 * GLU_HALO, cb), F32)
    return pl.pallas_call(
        body, name=name, grid=(C // cb, rows // tb),
        in_specs=specs * 3 + [pl.BlockSpec((3, cb), lambda c, i: (0, c)), pl.BlockSpec((1, cb), lambda c, i: (0, c))],
        out_specs=[blk, blk, pl.BlockSpec((3, cb), lambda c, i: (0, c)), pl.BlockSpec((1, cb), lambda c, i: (0, c))],
        out_shape=[jax.ShapeDtypeStruct((rows, C), g.dtype), jax.ShapeDtypeStruct((rows, C), v.dtype),
                   jax.ShapeDtypeStruct((3, C), F32), jax.ShapeDtypeStruct((1, C), F32)],
        scratch_shapes=[win, win, win, pltpu.VMEM((ext, cb), F32)],
        compiler_params=_cparams(("parallel", "arbitrary")),
    )(g, g, g, v, v, v, da, da, da, w, b)


def ffn_glu(g, v, w, b, n_ctx, name):
    @jax.custom_vjp
    def op(g, v, w, b):
        return _ffn_glu_fwd(g, v, w, b, n_ctx, name + "_fwd")

    def fwd(g, v, w, b):
        return _ffn_glu_fwd(g, v, w, b, n_ctx, name + "_fwd"), (g, v, w, b)

    def bwd(res, da):
        g, v, w, b = res
        return tuple(_ffn_glu_bwd(g, v, w, b, da, n_ctx, name + "_bwd"))

    op.defvjp(fwd, bwd)
    return op(g, v, w, b)


def _block_scan(a, b, reverse):
    tb = a.shape[0]
    row = lax.broadcasted_iota(jnp.int32, (tb, 1), 0)
    s = 1
    while s < tb:
        if reverse:
            keep = row < tb - s
            a_sh = pltpu.roll(a, tb - s, 0)
            b_sh = pltpu.roll(b, tb - s, 0)
        else:
            keep = row >= s
            a_sh = pltpu.roll(a, s, 0)
            b_sh = pltpu.roll(b, s, 0)
        b = jnp.where(keep, a * b_sh + b, b)
        a = jnp.where(keep, a * a_sh, a)
        s *= 2
    return a, b


def _shift_in(h, carry, reverse):
    tb = h.shape[0]
    row = lax.broadcasted_iota(jnp.int32, (tb, 1), 0)
    if reverse:
        return jnp.where(row == tb - 1, carry, pltpu.roll(h, tb - 1, 0))
    return jnp.where(row == 0, carry, pltpu.roll(h, 1, 0))


def _scan_maps(nrb, rot, reverse):
    def phys(c, i):
        q = (nrb - 1 - i) if reverse else i
        return (lax.rem(q + rot, nrb), c)
    return phys


def _scan_fwd(a, b, rot, reverse, name):
    rows, C = a.shape
    tb = 256
    cb = _div_tile(C, 512, LANE)
    nrb = rows // tb
    last_row = 0 if reverse else tb - 1

    def body(a_ref, b_ref, h_ref, hp_ref, carry_ref):
        i = pl.program_id(1)

        @pl.when(i == 0)
        def _():
            carry_ref[...] = jnp.zeros_like(carry_ref)

        carry = carry_ref[pl.ds(0, 1), :]
        A, B = _block_scan(a_ref[...], b_ref[...], reverse)
        h = A * carry + B
        h_ref[...] = h
        hp_ref[...] = _shift_in(h, carry, reverse)
        carry_ref[pl.ds(0, 1), :] = h[last_row:last_row + 1, :]

    spec = pl.BlockSpec((tb, cb), _scan_maps(nrb, rot, reverse))
    return pl.pallas_call(
        body, name=name, grid=(C // cb, nrb),
        in_specs=[spec, spec], out_specs=[spec, spec],
        out_shape=[jax.ShapeDtypeStruct((rows, C), F32)] * 2,
        scratch_shapes=[pltpu.VMEM((SUBLANE, cb), F32)],
        compiler_params=_cparams(("parallel", "arbitrary")),
    )(a, b)


def _scan_bwd(a, dh, h_prev, rot, reverse, name):
    rows, C = a.shape
    tb = 256
    cb = _div_tile(C, 512, LANE)
    nrb = rows // tb
    adj = not reverse
    last_row = 0 if adj else tb - 1

    def body(a_ref, dh_ref, hp_ref, da_ref, db_ref, carry_ref):
        i = pl.program_id(1)

        @pl.when(i == 0)
        def _():
            carry_ref[...] = jnp.zeros_like(carry_ref)

        carry = carry_ref[pl.ds(0, 1), :]
        a = a_ref[...]
        dh = dh_ref[...]
        A, B = _block_scan(a, a * dh, adj)
        u = A * carry + B
        g = dh + _shift_in(u, carry, adj)
        db_ref[...] = g
        da_ref[...] = g * hp_ref[...]
        carry_ref[pl.ds(0, 1), :] = u[last_row:last_row + 1, :]

    spec = pl.BlockSpec((tb, cb), _scan_maps(nrb, rot, adj))
    return pl.pallas_call(
        body, name=name, grid=(C // cb, nrb),
        in_specs=[spec, spec, spec], out_specs=[spec, spec],
        out_shape=[jax.ShapeDtypeStruct((rows, C), F32)] * 2,
        scratch_shapes=[pltpu.VMEM((SUBLANE, cb), F32)],
        compiler_params=_cparams(("parallel", "arbitrary")),
    )(a, dh, h_prev)


def linear_scan(a, b, rot, reverse, name):
    @jax.custom_vjp
    def op(a, b):
        return _scan_fwd(a, b, rot, reverse, name + "_fwd")[0]

    def fwd(a, b):
        h, hp = _scan_fwd(a, b, rot, reverse, name + "_fwd")
        return h, (a, hp)

    def bwd(res, dh):
        a, hp = res
        da, db = _scan_bwd(a, dh, hp, rot, reverse, name + "_bwd")
        return da, db

    op.defvjp(fwd, bwd)
    return op(a, b)


def rope_tables(n_ctx, n_lat):
    t = jnp.arange(n_lat)
    pos = jnp.stack([t // GRID_W, t % GRID_W], axis=-1).astype(F32)
    freq = ROPE_THETA ** (-jnp.arange(ROPE_FREQS, dtype=F32) / ROPE_FREQS)
    ang = pos[:, :, None] * freq
    cos, sin = jnp.cos(ang), jnp.sin(ang)
    c = jnp.concatenate([cos[:, 0], cos[:, 0], cos[:, 1], cos[:, 1]], axis=-1)
    s = jnp.concatenate([-sin[:, 0], sin[:, 0], -sin[:, 1], sin[:, 1]], axis=-1)
    c = jnp.concatenate([jnp.ones((n_ctx, HEAD_DIM), F32), c], axis=0)
    s = jnp.concatenate([jnp.zeros((n_ctx, HEAD_DIM), F32), s], axis=0)
    return c, s


def _rope_apply(qkv, c_tab, s_tab, n_rot_heads, out_dtype, name):
    rows, cols = qkv.shape
    tb = _div_tile(rows, 768, SUBLANE)
    heads = cols // HEAD_DIM
    hb = max(h for h in (4, 2, 1) if heads % h == 0 and n_rot_heads % h == 0)
    wb = hb * HEAD_DIM

    def body(x_ref, c_ref, s_ref, o_ref):
        x = x_ref[...].astype(F32)
        lane = lax.broadcasted_iota(jnp.int32, x.shape, 1)
        swapped = jnp.where((lane & 63) < 32, pltpu.roll(x, wb - 32, 1), pltpu.roll(x, 32, 1))
        roped = x * jnp.tile(c_ref[...], (1, hb)) + swapped * jnp.tile(s_ref[...], (1, hb))
        o_ref[...] = jnp.where(pl.program_id(1) * hb < n_rot_heads, roped, x).astype(out_dtype)

    tab = pl.BlockSpec((tb, HEAD_DIM), lambda i, j: (i, 0))
    blk = pl.BlockSpec((tb, wb), lambda i, j: (i, j))
    return pl.pallas_call(
        body, name=name, grid=(rows // tb, cols // wb),
        in_specs=[blk, tab, tab], out_specs=blk,
        out_shape=jax.ShapeDtypeStruct((rows, cols), out_dtype),
        compiler_params=_cparams(("parallel", "arbitrary")),
    )(qkv, c_tab, s_tab)


def rope(qkv, c_tab, s_tab, n_rot_heads, name):
    @jax.custom_vjp
    def op(qkv):
        return _rope_apply(qkv, c_tab, s_tab, n_rot_heads, BF16, name + "_fwd")

    def fwd(qkv):
        return _rope_apply(qkv, c_tab, s_tab, n_rot_heads, BF16, name + "_fwd"), None

    def bwd(_, d):
        return (_rope_apply(d, c_tab, -s_tab, n_rot_heads, F32, name + "_bwd"),)

    op.defvjp(fwd, bwd)
    return op(qkv)


def _attn_in_specs(H, KV, n_ctx, nqb):
    ncb = n_ctx // WINDOW_BLOCK
    G = GQA_GROUP

    def loc(delta, col0):
        return pl.BlockSpec((WINDOW_BLOCK, HEAD_DIM),
                            lambda g, i: (jnp.clip(i + delta, ncb, nqb - 1), col0 + g))

    q = pl.BlockSpec((WINDOW_BLOCK, G * HEAD_DIM), lambda g, i: (i, g))
    kc = pl.BlockSpec((n_ctx, HEAD_DIM), lambda g, i: (0, H + g))
    vc = pl.BlockSpec((n_ctx, HEAD_DIM), lambda g, i: (0, H + KV + g))
    sink = pl.BlockSpec((None, G * WINDOW_BLOCK, 1), lambda g, i: (g, 0, 0))
    bias = pl.BlockSpec((None, G * WINDOW_BLOCK, n_ctx + 3 * WINDOW_BLOCK),
                        lambda g, i: (jnp.where(i < ncb, 3, jnp.where(i == ncb, 1, jnp.where(i == nqb - 1, 2, 0))),
                                      0, 0))
    return [q, kc, loc(-1, H), loc(0, H), loc(1, H), vc, loc(-1, H + KV), loc(0, H + KV), loc(1, H + KV), sink, bias]


def _attn_bias(n_ctx):
    nq, nk = GQA_GROUP * WINDOW_BLOCK, n_ctx + 3 * WINDOW_BLOCK
    r = (jnp.arange(nq) % WINDOW_BLOCK)[:, None]
    col = jnp.arange(nk)[None, :]
    blk = (col - n_ctx) // WINDOW_BLOCK
    rk = (col - n_ctx) % WINDOW_BLOCK
    is_ctx = jnp.broadcast_to(col < n_ctx, (nq, nk))
    prev = (blk == 0) & (rk >= r)
    cur = jnp.broadcast_to(blk == 1, (nq, nk))
    nxt = (blk == 2) & (rk <= r)
    valid = jnp.stack([is_ctx | prev | cur | nxt, is_ctx | cur | nxt, is_ctx | prev | cur, is_ctx])
    return jnp.where(valid, 0.0, NEG_INF).astype(F32)


def _stack_heads(x):
    return jnp.concatenate([x[:, h * HEAD_DIM:(h + 1) * HEAD_DIM] for h in range(GQA_GROUP)], axis=0)


def _attn_probs(q_ref, kc_ref, kp_ref, kcur_ref, kn_ref, sink_ref, bias_ref):
    qs = _stack_heads(q_ref[...]).astype(BF16)
    k = jnp.concatenate([kc_ref[...], kp_ref[...], kcur_ref[...], kn_ref[...]], axis=0).astype(BF16)
    s = _dot(qs, k, ((1,), (1,))) * (HEAD_DIM ** -0.5) + bias_ref[...]
    sk = sink_ref[...]
    m = jnp.maximum(jnp.max(s, axis=1, keepdims=True), sk)
    e = jnp.exp(s - m)
    es = jnp.exp(sk - m)
    inv = 1.0 / (jnp.sum(e, axis=1, keepdims=True) + es)
    return qs, k, e * inv, es * inv


def _attn_fwd(qkv, sink_col, H, KV, n_ctx, name):
    rows = qkv.shape[0]
    nqb = rows // WINDOW_BLOCK
    assert (rows - n_ctx) // WINDOW_BLOCK >= 2
    G = GQA_GROUP

    def body(q_ref, kc_ref, kp_ref, kcur_ref, kn_ref, vc_ref, vp_ref, vcur_ref, vn_ref, sink_ref, bias_ref, o_ref):
        _, _, p, _ = _attn_probs(q_ref, kc_ref, kp_ref, kcur_ref, kn_ref, sink_ref, bias_ref)
        v = jnp.concatenate([vc_ref[...], vp_ref[...], vcur_ref[...], vn_ref[...]], axis=0).astype(BF16)
        o = _dot(p, v, ((1,), (0,)))
        for h in range(G):
            o_ref[:, h * HEAD_DIM:(h + 1) * HEAD_DIM] = o[h * WINDOW_BLOCK:(h + 1) * WINDOW_BLOCK, :].astype(BF16)

    return pl.pallas_call(
        body, name=name, grid=(KV, nqb),
        in_specs=_attn_in_specs(H, KV, n_ctx, nqb),
        out_specs=pl.BlockSpec((WINDOW_BLOCK, G * HEAD_DIM), lambda g, i: (i, g)),
        out_shape=jax.ShapeDtypeStruct((rows, H * HEAD_DIM), BF16),
        compiler_params=_cparams(("parallel", "arbitrary")),
    )(*([qkv] * 9), sink_col, _attn_bias(n_ctx))


def _attn_bwd(qkv, sink_col, o, do, H, KV, n_ctx, name):
    rows = qkv.shape[0]
    nqb = rows // WINDOW_BLOCK
    G = GQA_GROUP
    WB = WINDOW_BLOCK

    def body(q_ref, kc_ref, kp_ref, kcur_ref, kn_ref, vc_ref, vp_ref, vcur_ref, vn_ref, sink_ref, bias_ref, o_ref,
             do_ref, dq_ref, dkc_ref, dvc_ref, dkp_ref, dkcur_ref, dkn_ref, dvp_ref, dvcur_ref, dvn_ref, dsink_ref):
        i = pl.program_id(1)
        qs, k, p, ps = _attn_probs(q_ref, kc_ref, kp_ref, kcur_ref, kn_ref, sink_ref, bias_ref)
        v = jnp.concatenate([vc_ref[...], vp_ref[...], vcur_ref[...], vn_ref[...]], axis=0).astype(BF16)
        do_s = _stack_heads(do_ref[...]).astype(F32)
        o_s = _stack_heads(o_ref[...]).astype(F32)
        delta = jnp.sum(do_s * o_s, axis=1, keepdims=True)
        dp = _dot(do_s, v, ((1,), (1,)))
        ds = p * (dp - delta) * (HEAD_DIM ** -0.5)
        dq = _dot(ds, k, ((1,), (0,)))
        dk = _dot(ds, qs, ((0,), (0,)))
        dv = _dot(p, do_s, ((0,), (0,)))
        for h in range(G):
            dq_ref[:, h * HEAD_DIM:(h + 1) * HEAD_DIM] = dq[h * WB:(h + 1) * WB, :]

        @pl.when(i == 0)
        def _():
            dkc_ref[...] = jnp.zeros_like(dkc_ref)
            dvc_ref[...] = jnp.zeros_like(dvc_ref)
            dsink_ref[...] = jnp.zeros_like(dsink_ref)

        dkc_ref[...] += dk[:n_ctx]
        dvc_ref[...] += dv[:n_ctx]
        dsink_ref[...] += -ps * delta
        for j, (dk_ref, dv_ref) in enumerate(((dkp_ref, dvp_ref), (dkcur_ref, dvcur_ref), (dkn_ref, dvn_ref))):
            dk_ref[...] = dk[n_ctx + j * WB:n_ctx + (j + 1) * WB]
            dv_ref[...] = dv[n_ctx + j * WB:n_ctx + (j + 1) * WB]

    qblk = pl.BlockSpec((WB, G * HEAD_DIM), lambda g, i: (i, g))
    ctx = pl.BlockSpec((n_ctx, HEAD_DIM), lambda g, i: (0, g))
    piece = pl.BlockSpec((WB, HEAD_DIM), lambda g, i: (i, g))
    sink = pl.BlockSpec((None, G * WB, 1), lambda g, i: (g, 0, 0))
    kv_shape = jax.ShapeDtypeStruct((rows, KV * HEAD_DIM), F32)
    ctx_shape = jax.ShapeDtypeStruct((n_ctx, KV * HEAD_DIM), F32)
    return pl.pallas_call(
        body, name=name, grid=(KV, nqb),
        in_specs=_attn_in_specs(H, KV, n_ctx, nqb) + [qblk, qblk],
        out_specs=[qblk, ctx, ctx] + [piece] * 6 + [sink],
        out_shape=[jax.ShapeDtypeStruct((rows, H * HEAD_DIM), F32), ctx_shape, ctx_shape] + [kv_shape] * 6 +
                  [jax.ShapeDtypeStruct(sink_col.shape, F32)],
        compiler_params=_cparams(("parallel", "arbitrary")),
    )(*([qkv] * 9), sink_col, _attn_bias(n_ctx), o, do)


def _shift_blocks(x, n_ctx, delta):
    lat = x[n_ctx:]
    z = jnp.zeros((WINDOW_BLOCK, x.shape[1]), x.dtype)
    if delta == 1:
        lat = jnp.concatenate([z, lat[:-WINDOW_BLOCK]], axis=0)
    elif delta == -1:
        lat = jnp.concatenate([lat[WINDOW_BLOCK:], z], axis=0)
    return jnp.concatenate([jnp.zeros((n_ctx, x.shape[1]), x.dtype), lat], axis=0)


def attention(qkv, sink_col, H, KV, n_ctx, name):
    @jax.custom_vjp
    def op(qkv, sink_col):
        return _attn_fwd(qkv, sink_col, H, KV, n_ctx, name + "_fwd")

    def fwd(qkv, sink_col):
        o = _attn_fwd(qkv, sink_col, H, KV, n_ctx, name + "_fwd")
        return o, (qkv, sink_col, o)

    def bwd(res, do):
        qkv, sink_col, o = res
        dq, dkc, dvc, dkp, dkcur, dkn, dvp, dvcur, dvn, dsink = _attn_bwd(qkv, sink_col, o, do, H, KV, n_ctx,
                                                                          name + "_bwd")

        def gather_pieces(prev, cur, nxt, ctx):
            pad = jnp.concatenate([ctx, jnp.zeros((qkv.shape[0] - n_ctx, ctx.shape[1]), F32)], axis=0)
            return rowwise(name + "_kvsum", lambda a, b, c, d: (a + b + c + d,),
                           [cur, _shift_blocks(prev, n_ctx, -1), _shift_blocks(nxt, n_ctx, 1), pad], [], [], 1, 256)[0]

        dk = gather_pieces(dkp, dkcur, dkn, dkc)
        dv = gather_pieces(dvp, dvcur, dvn, dvc)
        return jnp.concatenate([dq, dk, dv], axis=1).astype(qkv.dtype), dsink

    op.defvjp(fwd, bwd)
    return op(qkv, sink_col)


ROW_BLOCK = 256


def _rms_norm(x, g):
    return (x * lax.rsqrt(jnp.mean(x * x, axis=-1, keepdims=True) + NORM_EPS)) * g


def _modulate_f(x, g, shift, scale):
    return ((_rms_norm(x, g) * (1.0 + scale) + shift).astype(BF16),)


def _expm1(x):
    series = x * (1 + x / 2 * (1 + x / 3 * (1 + x / 4 * (1 + x / 5 * (1 + x / 6)))))
    return jnp.where(jnp.abs(x) < 0.1, series, jnp.exp(x) - 1.0)


def _lru_gates_f(ra, rx, uc, ba, bx, sp):
    r = jax.nn.sigmoid(ra + ba)
    ig = jax.nn.sigmoid(rx + bx)
    log_a = -LRU_C * r * sp
    return jnp.exp(log_a), jnp.sqrt(-_expm1(2.0 * log_a)) * (ig * uc)


def _ln_silu_f(z, g, b):
    mu = jnp.mean(z, axis=-1, keepdims=True)
    var = jnp.mean(jnp.square(z - mu), axis=-1, keepdims=True)
    return (jax.nn.silu((z - mu) * lax.rsqrt(var + NORM_EPS) * g + b).astype(BF16),)


def _loss_f(x, target, g):
    err = _rms_norm(x, g) - target
    return (jnp.sum(0.5 * err * err, axis=0, keepdims=True) / x.shape[1],)


def _modulate(x, g, shift, scale):
    return rowwise("modulate", _modulate_f, [x], [g, shift, scale], ["vec", "seg", "seg"], 1, ROW_BLOCK)[0]


def _residual_modulate(x, y, bias, gate, g, shift, scale):
    def f(x, y, *p):
        x_new = x + p[-4] * (y if bias is None else y + p[0])
        return x_new, _modulate_f(x_new, *p[-3:])[0]

    params = ([] if bias is None else [bias]) + [gate, g, shift, scale]
    kinds = ([] if bias is None else ["vec"]) + ["seg", "vec", "seg", "seg"]
    return rowwise("residual_modulate", f, [x, y], params, kinds, 2, ROW_BLOCK)


def _residual(x, y, gate):
    return rowwise("residual", lambda x, y, g: (x + g * y,), [x, y], [gate], ["seg"], 1, ROW_BLOCK)[0]


def _split_nn(a, b, g0, name):
    M, K = a.shape
    n = b.shape[2]
    tm = _div_tile(M, 768, SUBLANE)
    tn = _div_tile(n, 1408, LANE)
    nn = n // tn

    def body(a_ref, b_ref, o_ref):
        o_ref[...] = _dot(a_ref[...], b_ref[...], ((1,), (0,))).astype(BF16)

    return pl.pallas_call(
        body, name=name, grid=(M // tm, 2, nn),
        in_specs=[pl.BlockSpec((tm, K), lambda i, g, j: (i, 0)),
                  pl.BlockSpec((None, K, tn), lambda i, g, j: (g + g0, 0, j))],
        out_specs=pl.BlockSpec((tm, tn), lambda i, g, j: (i, g * nn + j)),
        out_shape=jax.ShapeDtypeStruct((M, 2 * n), BF16),
        compiler_params=_cparams(("parallel", "parallel", "parallel")),
    )(a, b)


def _split_nt(dy_a, dy_b, b, out_dtype, name):
    M = dy_a.shape[0]
    _, K, n = b.shape
    tm = _div_tile(M, 768, SUBLANE)
    tko = _div_tile(K, 1408, LANE)
    tr = _div_tile(n, 2048, LANE)
    nr = n // tr
    half = 2 * nr

    def body(dya_ref, dyb_ref, b_ref, o_ref, acc_ref):
        r = pl.program_id(2)

        @pl.when(r == 0)
        def _():
            acc_ref[...] = _dot(dya_ref[...], b_ref[...], ((1,), (1,)))

        @pl.when((r > 0) & (r < half))
        def _():
            acc_ref[...] += _dot(dya_ref[...], b_ref[...], ((1,), (1,)))

        @pl.when(r >= half)
        def _():
            acc_ref[...] += _dot(dyb_ref[...], b_ref[...], ((1,), (1,)))

        @pl.when(r == 2 * half - 1)
        def _():
            o_ref[...] = acc_ref[...].astype(out_dtype)

    return pl.pallas_call(
        body, name=name, grid=(M // tm, K // tko, 2 * half),
        in_specs=[pl.BlockSpec((tm, tr), lambda i, kk, r: (i, jnp.minimum(r, half - 1))),
                  pl.BlockSpec((tm, tr), lambda i, kk, r: (i, jnp.maximum(r - half, 0))),
                  pl.BlockSpec((None, tko, tr), lambda i, kk, r: (r // nr, kk, lax.rem(r, nr)))],
        out_specs=pl.BlockSpec((tm, tko), lambda i, kk, r: (i, kk)),
        out_shape=jax.ShapeDtypeStruct((M, K), out_dtype),
        scratch_shapes=[pltpu.VMEM((tm, tko), F32)],
        compiler_params=_cparams(("parallel", "parallel", "arbitrary")),
    )(dy_a, dy_b, b)


def _split_tn(a, dy_a, dy_b, out_dtype, name):
    M, K = a.shape
    n = dy_a.shape[1] // 2
    tm = _div_tile(M, 768, SUBLANE)
    tk = _div_tile(K, 1408, LANE)
    tn = _div_tile(n, 1408, LANE)
    nkb, nn, nm = K // tk, n // tn, M // tm

    def body(a_ref, dya_ref, dyb_ref, o_ref, acc_ref):
        g, r = pl.program_id(0), pl.program_id(3)

        @pl.when(r == 0)
        def _():
            acc_ref[...] = jnp.zeros_like(acc_ref)

        @pl.when(g < 2)
        def _():
            acc_ref[...] += _dot(a_ref[...], dya_ref[...], ((0,), (0,)))

        @pl.when(g >= 2)
        def _():
            acc_ref[...] += _dot(a_ref[...], dyb_ref[...], ((0,), (0,)))

        @pl.when(r == nm - 1)
        def _():
            o_ref[...] = acc_ref[...].astype(out_dtype)

    return pl.pallas_call(
        body, name=name, grid=(4, nkb, nn, nm),
        in_specs=[pl.BlockSpec((tm, tk), lambda g, kk, j, r: (r, kk)),
                  pl.BlockSpec((tm, tn), lambda g, kk, j, r: (jnp.where(g < 2, r, 0), jnp.minimum(g, 1) * nn + j)),
                  pl.BlockSpec((tm, tn), lambda g, kk, j, r: (jnp.where(g >= 2, r, 0),
                                                              jnp.maximum(g - 2, 0) * nn + j))],
        out_specs=pl.BlockSpec((None, tk, tn), lambda g, kk, j, r: (g, kk, j)),
        out_shape=jax.ShapeDtypeStruct((4, K, n), out_dtype),
        scratch_shapes=[pltpu.VMEM((tk, tn), F32)],
        compiler_params=_cparams(("parallel", "parallel", "parallel", "arbitrary")),
    )(a, dy_a, dy_b)


def _split_linear(h, w, px, name):
    @jax.custom_vjp
    def op(h, w, px):
        return _split_nn(h, w, 0, name + "_a_fwd"), _split_nn(h, w, 2, name + "_b_fwd")

    def fwd(h, w, px):
        return (_split_nn(h, w, 0, name + "_a_fwd"), _split_nn(h, w, 2, name + "_b_fwd")), (h, w)

    def bwd(res, cts):
        h, w = res
        dh = _split_nt(cts[0], cts[1], w, h.dtype, name + "_dx")
        dw = _split_tn(h, cts[0], cts[1], BF16, name + "_dw")
        return dh, None, dw

    op.defvjp(fwd, bwd)
    return op(h, w, px)


def _attention_mixer(h, P, PX, j, n_ctx, tabs):
    H = P["attn_w_o"][j].shape[1] // HEAD_DIM
    KV = H // GQA_GROUP
    qkv = linear(h, P["attn_w_qkv"][j], PX["attn_w_qkv"][j], grad_dtype=BF16, name="attn_qkv")
    qkv = rope(qkv, tabs[0], tabs[1], H + KV, "rope")
    sink_col = jnp.repeat(P["attn_sink"][j].reshape(KV, GQA_GROUP), WINDOW_BLOCK, axis=1)[..., None]
    o = attention(qkv, sink_col, H, KV, n_ctx, "attn")
    return linear(o, P["attn_w_o"][j], PX["attn_w_o"][j], grad_dtype=BF16, name="attn_o"), None


def _rglru_mixer(h, P, PX, j, n_ctx):
    gate, xb = _split_linear(h, P["lru_w_in"][j], PX["lru_w_in"][j], "lru_in")
    R = xb.shape[1]
    cb = _div_tile(R, 512, LANE)
    sp = jax.nn.softplus(-P["lru_lambda"][j])
    hs = []
    for d in range(2):
        K = P["lru_conv_w"][j].shape[1]
        uc = dwconv(xb, P["lru_conv_w"][j][d], P["lru_conv_b"][j][d][None], 0 if d == 1 else K - 1, n_ctx,
                    "lru_conv")
        ra = blockdiag_linear(uc, P["lru_wa"][j][d], "lru_wa")
        rx = blockdiag_linear(uc, P["lru_wx"][j][d], "lru_wx")
        a, bt = rowwise("lru_gates", _lru_gates_f, [ra, rx, uc],
                        [P["lru_ba"][j][d][None], P["lru_bx"][j][d][None], sp[d][None]], ["vec"] * 3, 2,
                        ROW_BLOCK, cb=cb)
        hs.append(linear_scan(a, bt, d, d == 1, "lru_scan"))
    y_in = rowwise("lru_gelu", lambda g, h0, h1: ((jax.nn.gelu(g) * (h0 + h1)).astype(BF16),), [gate, hs[0], hs[1]], [], [], 1,
                   ROW_BLOCK, cb=cb)[0]
    return linear(y_in, P["lru_w_out"][j], PX["lru_w_out"][j], grad_dtype=BF16, name="lru_out"), None


def _conformer_mixer(h, P, PX, j, n_ctx):
    z1, z2 = _split_linear(h, P["conf_w_in"][j], PX["conf_w_in"][j], "conf_in")
    Dm = z1.shape[1]
    b_in = P["conf_b_in"][j]
    z = rowwise("conf_glu", lambda a, b, ba, bb: ((a + ba) * jax.nn.sigmoid(b + bb),), [z1, z2],
                [b_in[None, :Dm], b_in[None, Dm:]], ["vec", "vec"], 1, ROW_BLOCK, cb=_div_tile(Dm, 512, LANE))[0]
    K = P["conf_dw_w"][j].shape[0]
    zc = dwconv(z, P["conf_dw_w"][j], P["conf_dw_b"][j][None], K // 2, n_ctx, "conf_conv")
    zs = rowwise("conf_ln_silu", _ln_silu_f, [zc], [P["conf_ln_g"][j][None], P["conf_ln_b"][j][None]],
                 ["vec", "vec"], 1, ROW_BLOCK)[0]
    y = linear(zs, P["conf_w_out"][j], PX["conf_w_out"][j], grad_dtype=BF16, name="conf_out")
    return y, P["conf_b_out"][j][None]


def _conv_ffn(u, P, PX, i, n_ctx):
    g, v = _split_linear(u, P["ffn_w_up"][i], PX["ffn_w_up"][i], "ffn_up")
    a = ffn_glu(g, v, P["ffn_conv_w"][i], P["ffn_conv_b"][i][None], n_ctx, "ffn_glu")
    return linear(a, P["ffn_w_down"][i], PX["ffn_w_down"][i], grad_dtype=BF16, name="ffn_down")


def local_loss(x_all, mods, P, PX, target, n_ctx):
    assert n_ctx == ROW_BLOCK
    depth = len(mods)
    tabs = rope_tables(n_ctx, x_all.shape[0] - n_ctx)
    x = x_all
    h = _modulate(x, P["norm_mix_g"][0][None], mods[0][0], mods[0][1])
    for i in range(depth):
        kind, j = i % 3, i // 3
        _, _, g1, sh2, sc2, g2 = mods[i]
        if kind == 0:
            y, bias = _attention_mixer(h, P, PX, j, n_ctx, tabs)
        elif kind == 1:
            y, bias = _rglru_mixer(h, P, PX, j, n_ctx)
        else:
            y, bias = _conformer_mixer(h, P, PX, j, n_ctx)
        x, u = _residual_modulate(x, y, bias, g1, P["norm_ffn_g"][i][None], sh2, sc2)
        f = _conv_ffn(u, P, PX, i, n_ctx)
        if i + 1 < depth:
            x, h = _residual_modulate(x, f, None, g2, P["norm_mix_g"][i + 1][None], mods[i + 1][0], mods[i + 1][1])
        else:
            x = _residual(x, f, g2)
    per_feature = rowwise("loss_head", _loss_f, [x[n_ctx:], target], [P["final_norm_g"][None]], ["vec"], 0,
                          ROW_BLOCK, nograd=(1,))[0]
    return jnp.sum(per_feature)


def _plane_peers():
    x, y, c = lax.axis_index("x"), lax.axis_index("y"), lax.axis_index("c")
    me = 2 * x + y
    peers = [((1 - x, y, c), 2 * (1 - x) + y),
             ((x, 1 - y, c), 2 * x + (1 - y)),
             ((1 - x, 1 - y, c), 2 * (1 - x) + (1 - y))]
    return me, peers


def plane_allgather(arrays, layers, name):
    flat = []
    for k, L in enumerate(layers):
        flat += [(k, None)] if L is None else [(k, l) for l in range(L)]
    n_in, n = len(arrays), len(flat)

    def body(*refs):
        ins, outs = refs[:n_in], refs[n_in:n_in + n]
        lsem, ssem, rsem = refs[n_in + n:]
        me, peers = _plane_peers()

        def src(t):
            k, l = flat[t]
            return ins[k] if l is None else ins[k].at[l]

        def remote(t, p, slot):
            return pltpu.make_async_remote_copy(src(t), outs[t].at[slot], ssem.at[3 * t + p], rsem.at[3 * t + p],
                                                device_id=peers[p][0], device_id_type=MESH)

        local = [pltpu.make_async_copy(src(t), outs[t].at[me], lsem.at[t]) for t in range(n)]
        for t in range(n):
            local[t].start()
            for p in range(3):
                remote(t, p, me).start()
        for t in range(n):
            local[t].wait()
            for p in range(3):
                remote(t, p, peers[p][1]).wait()

    out_shape = []
    for k, l in flat:
        shp = arrays[k].shape if l is None else arrays[k].shape[1:]
        out_shape.append(jax.ShapeDtypeStruct((4,) + tuple(shp), arrays[k].dtype))
    res = pl.pallas_call(
        body, name=name, in_specs=[ANY] * n_in, out_specs=[ANY] * n, out_shape=out_shape,
        scratch_shapes=[pltpu.SemaphoreType.DMA((n,)), pltpu.SemaphoreType.DMA((3 * n,)),
                        pltpu.SemaphoreType.DMA((3 * n,))],
    )(*arrays)
    out, t = [], 0
    for L in layers:
        if L is None:
            out.append(res[t])
            t += 1
        else:
            out.append(list(res[t:t + L]))
            t += L
    return out


def plane_alltoall(groups, name):
    flat = [(k, l) for k, grp in enumerate(groups) for l in range(len(grp))]
    arrays = [a for grp in groups for a in grp]
    n, ng = len(flat), len(groups)

    def body(*refs):
        ins, outs = refs[:n], refs[n:n + ng]
        lsem, ssem, rsem = refs[n + ng:]
        me, peers = _plane_peers()

        def remote(t, p, src_slot, dst_slot):
            k, l = flat[t]
            return pltpu.make_async_remote_copy(ins[t].at[src_slot], outs[k].at[dst_slot, l], ssem.at[3 * t + p],
                                                rsem.at[3 * t + p], device_id=peers[p][0], device_id_type=MESH)

        local = [pltpu.make_async_copy(ins[t].at[me], outs[flat[t][0]].at[me, flat[t][1]], lsem.at[t])
                 for t in range(n)]
        for t in range(n):
            local[t].start()
            for p in range(3):
                remote(t, p, peers[p][1], me).start()
        for t in range(n):
            local[t].wait()
            for p in range(3):
                remote(t, p, peers[p][1], peers[p][1]).wait()

    out_shape = [jax.ShapeDtypeStruct((4, len(grp)) + tuple(grp[0].shape[1:]), grp[0].dtype) for grp in groups]
    return pl.pallas_call(
        body, name=name, in_specs=[ANY] * n, out_specs=[ANY] * ng, out_shape=out_shape,
        scratch_shapes=[pltpu.SemaphoreType.DMA((n,)), pltpu.SemaphoreType.DMA((3 * n,)),
                        pltpu.SemaphoreType.DMA((3 * n,))],
    )(*arrays)


def sibling_exchange(arrays, name):
    n = len(arrays)

    def body(*refs):
        ins, outs = refs[:n], refs[n:2 * n]
        ssem, rsem = refs[2 * n:]
        sibling = (lax.axis_index("x"), lax.axis_index("y"), 1 - lax.axis_index("c"))
        copies = [pltpu.make_async_remote_copy(ins[t], outs[t], ssem.at[t], rsem.at[t], device_id=sibling,
                                               device_id_type=MESH) for t in range(n)]
        for cp in copies:
            cp.start()
        for cp in copies:
            cp.wait()

    return pl.pallas_call(
        body, name=name, in_specs=[ANY] * n, out_specs=[ANY] * n,
        out_shape=[jax.ShapeDtypeStruct(a.shape, a.dtype) for a in arrays],
        scratch_shapes=[pltpu.SemaphoreType.DMA((n,)), pltpu.SemaphoreType.DMA((n,))],
    )(*arrays)


def _sibling():
    return (lax.axis_index("x"), lax.axis_index("y"), 1 - lax.axis_index("c"))


def _rows_half(ref, h, rows, axis):
    idx = (slice(None),) * axis + (pl.ds(h * (rows // 2), rows // 2),)
    return ref.at[idx]


def plane_allgather_shared(arrays, layers, name):
    flat = []
    for k, L in enumerate(layers):
        flat += [(k, None)] if L is None else [(k, l) for l in range(L)]
    n_in, n = len(arrays), len(flat)
    shard_rows = [arrays[k].shape[0] if l is None else arrays[k].shape[1] for k, l in flat]
    assert all(r % 32 == 0 for r in shard_rows)

    def body(*refs):
        ins, outs = refs[:n_in], refs[n_in:n_in + n]
        isend, irecv, dsend, drecv = refs[n_in + n:]
        me, peers = _plane_peers()
        c = lax.axis_index("c")

        def src(t):
            k, l = flat[t]
            return ins[k] if l is None else ins[k].at[l]

        def over_ici(t, p, slot):
            return pltpu.make_async_remote_copy(
                _rows_half(src(t), c, shard_rows[t], 0), _rows_half(outs[t].at[slot], c, shard_rows[t], 0),
                isend.at[3 * t + p], irecv.at[3 * t + p], device_id=peers[p][0], device_id_type=MESH)

        def over_d2d(t, p, h):
            piece = _rows_half(outs[t].at[peers[p][1]], h, shard_rows[t], 0)
            return pltpu.make_async_remote_copy(piece, piece, dsend.at[3 * t + p], drecv.at[3 * t + p],
                                                device_id=_sibling(), device_id_type=MESH)

        for t in range(n):
            for p in range(3):
                over_ici(t, p, me).start()
        for t in range(n):
            for p in range(3):
                over_ici(t, p, peers[p][1]).wait_recv()
                over_d2d(t, p, c).start()
        for t in range(n):
            for p in range(3):
                over_d2d(t, p, 1 - c).wait_recv()
                over_ici(t, p, me).wait_send()
                over_d2d(t, p, c).wait_send()

    out_shape = []
    for k, l in flat:
        shp = arrays[k].shape if l is None else arrays[k].shape[1:]
        out_shape.append(jax.ShapeDtypeStruct((4,) + tuple(shp), arrays[k].dtype))
    res = pl.pallas_call(
        body, name=name, in_specs=[ANY] * n_in, out_specs=[ANY] * n, out_shape=out_shape,
        scratch_shapes=[pltpu.SemaphoreType.DMA((3 * n,))] * 4,
    )(*arrays)
    me = 2 * lax.axis_index("x") + lax.axis_index("y")
    res = [lax.dynamic_update_index_in_dim(r, arrays[k] if l is None else arrays[k][l], me, 0)
           for r, (k, l) in zip(res, flat)]
    out, t = [], 0
    for L in layers:
        if L is None:
            out.append(res[t])
            t += 1
        else:
            out.append(list(res[t:t + L]))
            t += L
    return out


def add_pair(a, b, out_dtype, name):
    shp = a.shape
    R, C = _shape2d(shp)
    tr = _row_tile(R, C, 1 << 20)

    def body(a_ref, b_ref, o_ref):
        o_ref[...] = (a_ref[...].astype(F32) + b_ref[...].astype(F32)).astype(out_dtype)

    spec = pl.BlockSpec((tr, C), lambda i: (i, 0))
    return pl.pallas_call(
        body, name=name, grid=(R // tr,), in_specs=[spec, spec], out_specs=spec,
        out_shape=jax.ShapeDtypeStruct((R, C), out_dtype), compiler_params=_cparams(("parallel",)),
    )(a.reshape(R, C), b.reshape(R, C)).reshape(shp)


def gather_all_devices(a, name):
    own = plane_allgather([a], [None], name + "_plane")[0]
    sib = sibling_exchange([own], name + "_sibling")[0]
    c = lax.axis_index("c")
    c0 = jnp.where(c == 0, own, sib)
    c1 = jnp.where(c == 0, sib, own)
    return jnp.stack([c0, c1], axis=1).reshape((8,) + a.shape)


def _shape2d(shape):
    if len(shape) >= 2 and shape[-1] % LANE == 0:
        return (math.prod(shape[:-1]), shape[-1])
    return tuple(shape) if len(shape) == 2 else (1, math.prod(shape))


def _as2d(a):
    return a.reshape(_shape2d(a.shape))


def _row_tile(R, C, budget_bytes):
    if R * C * 4 <= budget_bytes or R % SUBLANE:
        return R
    cap = max(SUBLANE, (budget_bytes // (C * 4)) // SUBLANE * SUBLANE)
    return _div_tile(R, cap, SUBLANE)


def sum_slots(x4, name):
    shp = x4.shape[1:]
    R, C = _shape2d(shp)
    v = x4.reshape(4, R, C)
    tr = _row_tile(R, C, 1 << 20)

    def body(x_ref, o_ref):
        f = lambda s: x_ref[s].astype(F32)
        o_ref[...] = ((f(0) + f(1)) + f(2)) + f(3)

    out = pl.pallas_call(
        body, name=name, grid=(R // tr,),
        in_specs=[pl.BlockSpec((4, tr, C), lambda i: (0, i, 0))],
        out_specs=pl.BlockSpec((tr, C), lambda i: (i, 0)),
        out_shape=jax.ShapeDtypeStruct((R, C), F32),
        compiler_params=_cparams(("parallel",)),
    )(v)
    return out.reshape(shp)


def adamw(w, m, v, terms, name):
    shp = w.shape
    w2, m2, v2 = _as2d(w), _as2d(m), _as2d(v)
    flat = [_as2d(t) for inner in terms for t in inner]
    sizes = [len(inner) for inner in terms]
    R, C = w2.shape
    tr = _row_tile(R, C, 1 << 20)

    def body(*refs):
        w_ref, m_ref, v_ref = refs[:3]
        t_refs = refs[3:3 + len(flat)]
        g_ref, d_ref, nm_ref, nv_ref = refs[3 + len(flat):]
        g, t = None, 0
        for sz in sizes:
            inner = t_refs[t][...].astype(F32)
            for q in range(1, sz):
                inner = inner + t_refs[t + q][...].astype(F32)
            t += sz
            g = inner if g is None else g + inner
        nm = ADAM_B1 * m_ref[...] + (1.0 - ADAM_B1) * g
        nv = ADAM_B2 * v_ref[...] + (1.0 - ADAM_B2) * jnp.square(g)
        m_hat = nm / (1.0 - ADAM_B1 ** ADAM_STEP)
        v_hat = nv / (1.0 - ADAM_B2 ** ADAM_STEP)
        g_ref[...] = g
        d_ref[...] = -ADAM_LR * (m_hat / (jnp.sqrt(v_hat) + ADAM_EPS) + ADAM_WD * w_ref[...])
        nm_ref[...] = nm
        nv_ref[...] = nv

    spec = pl.BlockSpec((tr, C), lambda i: (i, 0))
    outs = pl.pallas_call(
        body, name=name, grid=(R // tr,),
        in_specs=[spec] * (3 + len(flat)), out_specs=[spec] * 4,
        out_shape=[jax.ShapeDtypeStruct((R, C), F32)] * 4,
        compiler_params=_cparams(("parallel",)),
    )(w2, m2, v2, *flat)
    return tuple(o.reshape(shp) for o in outs)


def _pack(arrs):
    flat = jnp.concatenate([a.reshape(-1).astype(F32) for a in arrs])
    unit = 32 * LANE
    pad = (-flat.shape[0]) % unit
    return jnp.pad(flat, (0, pad)).reshape(-1, LANE)


def _unpack(pack, shapes, lead=()):
    flat = pack.reshape(tuple(lead) + (-1,))
    out, off = [], 0
    for s in shapes:
        n = math.prod(s)
        out.append(flat[..., off:off + n].reshape(tuple(lead) + tuple(s)))
        off += n
    return out


COL_SHARDED = ("attn_w_qkv", "lru_w_in", "conf_w_in", "ffn_w_up")
ROW_SHARDED = ("attn_w_o", "lru_w_out", "conf_w_out", "ffn_w_down")
SMALL_SHARDED = ("lru_conv_w", "lru_conv_b", "lru_ba", "lru_bx", "lru_lambda", "conf_b_in", "conf_dw_w", "conf_dw_b",
                 "conf_ln_g", "conf_ln_b", "conf_b_out", "ffn_conv_w")
REPLICATED = ("norm_mix_g", "norm_ffn_g", "attn_sink", "lru_wa", "lru_wx", "ffn_conv_b", "final_norm_g")
LOCAL_ONLY = ("c_ctx", "ada_b")
WEIGHTS = ("c_ctx", "ada_w", "ada_b", "norm_mix_g", "norm_ffn_g", "attn_w_qkv", "attn_w_o", "attn_sink", "lru_w_in",
           "lru_conv_w", "lru_conv_b", "lru_wa", "lru_ba", "lru_wx", "lru_bx", "lru_lambda", "lru_w_out", "conf_w_in",
           "conf_b_in", "conf_dw_w", "conf_dw_b", "conf_ln_g", "conf_ln_b", "conf_w_out", "conf_b_out", "ffn_w_up",
           "ffn_conv_w", "ffn_conv_b", "ffn_w_down", "final_norm_g")


def _full_last_axis(g4):
    moved = jnp.moveaxis(g4, 0, -2)
    return moved.reshape(moved.shape[:-2] + (-1,))


def _shards_last_axis(full):
    split = full.reshape(full.shape[:-1] + (4, full.shape[-1] // 4))
    return jnp.moveaxis(split, -2, 0)


def _train_step(W, M, V, x, c, ctx, loss_target):
    n_ctx = ctx.shape[1]
    depth = W["ada_w"].shape[0]
    D = x.shape[-1]
    my_c = lax.axis_index("c")
    my_s = 2 * lax.axis_index("x") + lax.axis_index("y")
    my_dev = 2 * my_s + my_c

    big = COL_SHARDED + ROW_SHARDED
    small_shapes = [W[k].shape for k in SMALL_SHARDED]
    small_pack = _pack([W[k] for k in SMALL_SHARDED])
    gathered = plane_allgather_shared([W[k].astype(BF16) for k in big] + [small_pack],
                                      [W[k].shape[0] for k in big] + [None], "gather_weights")
    P, PX = {}, {}
    for k, per_layer in zip(big, gathered[:len(big)]):
        if k in ROW_SHARDED:
            per_layer = [g.reshape(1, -1, g.shape[-1]) for g in per_layer]
        P[k] = per_layer
        PX[k] = [jnp.zeros(g.shape, BF16) for g in per_layer]
    for k, g4 in zip(SMALL_SHARDED, _unpack(gathered[-1], small_shapes, lead=(4,))):
        P[k] = _full_last_axis(g4)
    for k in REPLICATED:
        P[k] = W[k]

    c8 = gather_all_devices(c, "gather_c").reshape(8, D)
    cond16 = jnp.concatenate([c8, jnp.broadcast_to(W["c_ctx"][None], (8, D))], axis=0)
    act16, act_vjp = jax.vjp(jax.nn.silu, cond16)
    n_ada = W["ada_w"].shape[-1]
    m_shard = matmul_nn(act16, W["ada_w"], name="ada_fwd")
    m4 = plane_allgather([m_shard], [None], "gather_ada")[0].reshape(4, 16, depth, n_ada)
    m_full = m4.transpose(2, 1, 0, 3).reshape(depth, 16, 4 * n_ada) + W["ada_b"][:, None, :]
    m_lat = lax.dynamic_index_in_dim(m_full, my_dev, axis=1, keepdims=False)
    m_ctx = m_full[:, 8]
    mods = [tuple(jnp.stack([a, b])[:, None, :] for a, b in zip(jnp.split(m_ctx[i], 6), jnp.split(m_lat[i], 6)))
            for i in range(depth)]

    x_all = jnp.concatenate([ctx[0], x[0]], axis=0)
    loss, (gx, gmods, gP, gPX) = jax.value_and_grad(
        lambda xa, md, p, px: local_loss(xa, md, p, px, loss_target[0], n_ctx), argnums=(0, 1, 2, 3))(x_all, mods, P, PX)
    loss = lax.psum(loss, ("x", "y", "c"))
    grad_x = gx[n_ctx:][None]

    dm_mine = jnp.stack([jnp.stack([jnp.concatenate([g[r, 0] for g in gmods[i]]) for i in range(depth)])
                         for r in range(2)])
    dm8 = gather_all_devices(dm_mine, "gather_dmod")
    dm16 = jnp.concatenate([dm8[:, 1], dm8[:, 0]], axis=0).transpose(1, 0, 2)
    grad_ada_b = rowwise("ada_b_grad", lambda a: (jnp.sum(a, axis=0, keepdims=True),),
                         [dm16.transpose(1, 0, 2).reshape(16, -1)], [], [], 0, 16)[0].reshape(depth, -1)
    dm_cols = lax.dynamic_slice_in_dim(dm16, my_s * n_ada, n_ada, axis=2)
    dm_cols = dm_cols.transpose(1, 0, 2).reshape(16, depth * n_ada)
    grad_ada_w = matmul_tn(act16, dm_cols, depth, name="ada_dw")
    dact_part = matmul_nt(dm_cols, W["ada_w"], name="ada_dx")
    dact4 = plane_allgather([dact_part], [None], "gather_dact")[0]
    dact = ((dact4[0] + dact4[1]) + dact4[2]) + dact4[3]
    grad_c_ctx = jnp.sum(act_vjp(dact)[0][8:], axis=0)

    out = {}
    out["ada_w"] = adamw(W["ada_w"], M["ada_w"], V["ada_w"], [[grad_ada_w]], "adamw_ada_w")
    local_shapes = [W[k].shape for k in LOCAL_ONLY]
    res = adamw(_pack([W[k] for k in LOCAL_ONLY]), _pack([M[k] for k in LOCAL_ONLY]), _pack([V[k] for k in LOCAL_ONLY]),
                [[_pack([grad_c_ctx, grad_ada_b])]], "adamw_local")
    for k, vals in zip(LOCAL_ONLY, zip(*[_unpack(r, local_shapes) for r in res])):
        out[k] = vals

    groups = []
    for k in big:
        pieces = gPX[k]
        if k in ROW_SHARDED:
            pieces = [g.reshape((4, -1, g.shape[-1])) for g in pieces]
        groups.append(pieces)
    small_grads = _pack_shards([_shards_last_axis(gP[k]) for k in SMALL_SHARDED])
    groups.append([small_grads])
    def rows_half(a, h):
        return lax.dynamic_slice_in_dim(a, h * (a.shape[1] // 2), a.shape[1] // 2, axis=1)

    pieces = [a for grp in groups for a in grp]
    theirs = sibling_exchange([rows_half(a, 1 - my_c) for a in pieces], "swap_grad_halves")
    chip_sums = [add_pair(rows_half(a, my_c), t, a.dtype, "chip_sum") for a, t in zip(pieces, theirs)]
    chip_groups, t = [], 0
    for grp in groups:
        chip_groups.append(chip_sums[t:t + len(grp)])
        t += len(grp)
    received = plane_alltoall(chip_groups, "scatter_grads")
    half_sums = [sum_slots(r, "plane_sum") for r in received]
    other = sibling_exchange(half_sums, "merge_grad_halves")
    full = [jnp.concatenate([jnp.where(my_c == 0, mine, got), jnp.where(my_c == 0, got, mine)], axis=1)
            for mine, got in zip(half_sums, other)]
    for k, g in zip(big, full[:len(big)]):
        out[k] = adamw(W[k], M[k], V[k], [[g]], "adamw_" + k)
    res = adamw(small_pack, _pack([M[k] for k in SMALL_SHARDED]), _pack([V[k] for k in SMALL_SHARDED]),
                [[full[-1][0]]], "adamw_small")
    for k, vals in zip(SMALL_SHARDED, zip(*[_unpack(r, small_shapes) for r in res])):
        out[k] = vals

    rep_shapes = [W[k].shape for k in REPLICATED]
    own = plane_allgather([_pack([gP[k] for k in REPLICATED])], [None], "gather_rep_grads")[0]
    sib = sibling_exchange([own], "sibling_rep_grads")[0]
    res = adamw(_pack([W[k] for k in REPLICATED]), _pack([M[k] for k in REPLICATED]), _pack([V[k] for k in REPLICATED]),
                [[own[s], sib[s]] for s in range(4)], "adamw_rep")
    for k, vals in zip(REPLICATED, zip(*[_unpack(r, rep_shapes) for r in res])):
        out[k] = vals

    return (loss, grad_x) + tuple(out[k][j] for j in range(4) for k in WEIGHTS)


def _pack_shards(arrs4):
    return jnp.stack([_pack([a[s] for a in arrs4]) for s in range(4)])


def kernel(x, c, ctx, c_ctx, ada_w, ada_b, norm_mix_g, norm_ffn_g, attn_w_qkv, attn_w_o, attn_sink, lru_w_in, lru_conv_w, lru_conv_b, lru_wa, lru_ba, lru_wx, lru_bx, lru_lambda, lru_w_out, conf_w_in, conf_b_in, conf_dw_w, conf_dw_b, conf_ln_g, conf_ln_b, conf_w_out, conf_b_out, ffn_w_up, ffn_conv_w, ffn_conv_b, ffn_w_down, final_norm_g, loss_target, m_c_ctx, m_ada_w, m_ada_b, m_norm_mix_g, m_norm_ffn_g, m_attn_w_qkv, m_attn_w_o, m_attn_sink, m_lru_w_in, m_lru_conv_w, m_lru_conv_b, m_lru_wa, m_lru_ba, m_lru_wx, m_lru_bx, m_lru_lambda, m_lru_w_out, m_conf_w_in, m_conf_b_in, m_conf_dw_w, m_conf_dw_b, m_conf_ln_g, m_conf_ln_b, m_conf_w_out, m_conf_b_out, m_ffn_w_up, m_ffn_conv_w, m_ffn_conv_b, m_ffn_w_down, m_final_norm_g, v_c_ctx, v_ada_w, v_ada_b, v_norm_mix_g, v_norm_ffn_g, v_attn_w_qkv, v_attn_w_o, v_attn_sink, v_lru_w_in, v_lru_conv_w, v_lru_conv_b, v_lru_wa, v_lru_ba, v_lru_wx, v_lru_bx, v_lru_lambda, v_lru_w_out, v_conf_w_in, v_conf_b_in, v_conf_dw_w, v_conf_dw_b, v_conf_ln_g, v_conf_ln_b, v_conf_w_out, v_conf_b_out, v_ffn_w_up, v_ffn_conv_w, v_ffn_conv_b, v_ffn_w_down, v_final_norm_g):
    given = dict(locals())
    W = {k: given[k] for k in WEIGHTS}
    M = {k: given["m_" + k] for k in WEIGHTS}
    V = {k: given["v_" + k] for k in WEIGHTS}
    return _train_step(W, M, V, x, c, ctx, loss_target)
```

```python
import functools
import math

import jax
import jax.numpy as jnp
from jax import lax
from jax.experimental import pallas as pl
from jax.experimental.pallas import tpu as pltpu

F32 = jnp.float32
BF16 = jnp.bfloat16

VMEM_LIMIT_BYTES = 56 * 1024 * 1024
LANE = 128
SUBLANE = 8

HEAD_DIM = 128
GQA_GROUP = 4
WINDOW_BLOCK = 128
GRID_W = 64
ROPE_THETA = 10000.0
ROPE_FREQS = HEAD_DIM // 4
LRU_BLOCK = 128
LRU_C = 8.0
NORM_EPS = 1e-6
NEG_INF = -1e30
CONV_HALO = 32

ADAM_LR = 0.001
ADAM_B1 = 0.9
ADAM_B2 = 0.999
ADAM_EPS = 1e-08
ADAM_WD = 0.01
ADAM_STEP = 10

MESH = pl.DeviceIdType.MESH
ANY = pl.BlockSpec(memory_space=pl.ANY)


def _cparams(sem):
    return pltpu.CompilerParams(dimension_semantics=sem, vmem_limit_bytes=VMEM_LIMIT_BYTES)


def _div_tile(n, cap, unit):
    if n <= unit:
        return n
    best = None
    t = unit
    while t <= min(n, cap):
        if n % t == 0:
            best = t
        t += unit
    assert best is not None, (n, cap, unit)
    return best


def _dot(a, b, dims):
    return lax.dot_general(a.astype(BF16), b.astype(BF16), (dims, ((), ())), preferred_element_type=F32)


def matmul_nn(a, b, diag=False, name="mm_nn"):
    M = a.shape[0]
    G, K, n = b.shape
    tm = _div_tile(M, 768, SUBLANE)
    tn = _div_tile(n, 1408, LANE)
    tk = K if K <= 2048 else _div_tile(K, 1408, LANE)
    nk, nn = K // tk, n // tn
    assert a.shape[1] == (G * K if diag else K)

    def body(a_ref, b_ref, o_ref, acc_ref):
        r = pl.program_id(3)
        part = _dot(a_ref[...], b_ref[...], ((1,), (0,)))
        if nk == 1:
            o_ref[...] = part
        else:
            @pl.when(r == 0)
            def _():
                acc_ref[...] = part

            @pl.when(r > 0)
            def _():
                acc_ref[...] += part

            @pl.when(r == nk - 1)
            def _():
                o_ref[...] = acc_ref[...]

    a_map = (lambda i, g, j, r: (i, g * nk + r)) if diag else (lambda i, g, j, r: (i, r))
    return pl.pallas_call(
        body, name=name,
        grid=(M // tm, G, nn, nk),
        in_specs=[pl.BlockSpec((tm, tk), a_map),
                  pl.BlockSpec((None, tk, tn), lambda i, g, j, r: (g, r, j))],
        out_specs=pl.BlockSpec((tm, tn), lambda i, g, j, r: (i, g * nn + j)),
        out_shape=jax.ShapeDtypeStruct((M, G * n), F32),
        scratch_shapes=[pltpu.VMEM((tm, tn) if nk > 1 else (SUBLANE, LANE), F32)],
        compiler_params=_cparams(("parallel", "parallel", "parallel", "arbitrary")),
    )(a, b)


def matmul_nt(dy, b, diag=False, out_dtype=F32, name="mm_nt"):
    M = dy.shape[0]
    G, K, n = b.shape
    assert dy.shape[1] == G * n
    tm = _div_tile(M, 768, SUBLANE)
    tko = _div_tile(K, 1408, LANE)
    tr = _div_tile(n, 2048, LANE)
    nr, nko = n // tr, K // tko
    steps = nr if diag else G * nr

    def body(dy_ref, b_ref, o_ref, acc_ref):
        r = pl.program_id(3)
        part = _dot(dy_ref[...], b_ref[...], ((1,), (1,)))
        if steps == 1:
            o_ref[...] = part.astype(out_dtype)
        else:
            @pl.when(r == 0)
            def _():
                acc_ref[...] = part

            @pl.when(r > 0)
            def _():
                acc_ref[...] += part

            @pl.when(r == steps - 1)
            def _():
                o_ref[...] = acc_ref[...].astype(out_dtype)

    if diag:
        grid = (M // tm, G, nko, nr)
        dy_map = lambda i, g, kk, r: (i, g * nr + r)
        b_map = lambda i, g, kk, r: (g, kk, r)
        o_map = lambda i, g, kk, r: (i, g * nko + kk)
        out_cols = G * K
    else:
        grid = (M // tm, 1, nko, G * nr)
        dy_map = lambda i, g, kk, r: (i, r)
        b_map = lambda i, g, kk, r: (r // nr, kk, r % nr)
        o_map = lambda i, g, kk, r: (i, kk)
        out_cols = K
    return pl.pallas_call(
        body, name=name,
        grid=grid,
        in_specs=[pl.BlockSpec((tm, tr), dy_map), pl.BlockSpec((None, tko, tr), b_map)],
        out_specs=pl.BlockSpec((tm, tko), o_map),
        out_shape=jax.ShapeDtypeStruct((M, out_cols), out_dtype),
        scratch_shapes=[pltpu.VMEM((tm, tko) if steps > 1 else (SUBLANE, LANE), F32)],
        compiler_params=_cparams(("parallel", "parallel", "parallel", "arbitrary")),
    )(dy, b)


def matmul_tn(a, dy, G, diag=False, out_dtype=F32, name="mm_tn"):
    M = a.shape[0]
    n = dy.shape[1] // G
    K = a.shape[1] // G if diag else a.shape[1]
    tm = _div_tile(M, 1408, SUBLANE)
    tk = _div_tile(K, 1408, LANE)
    tn = _div_tile(n, 1408, LANE)
    nkb, nn, nm = K // tk, n // tn, M // tm

    def body(a_ref, dy_ref, o_ref, acc_ref):
        r = pl.program_id(3)
        part = _dot(a_ref[...], dy_ref[...], ((0,), (0,)))
        if nm == 1:
            o_ref[...] = part.astype(out_dtype)
        else:
            @pl.when(r == 0)
            def _():
                acc_ref[...] = part

            @pl.when(r > 0)
            def _():
                acc_ref[...] += part

            @pl.when(r == nm - 1)
            def _():
                o_ref[...] = acc_ref[...].astype(out_dtype)

    a_map = (lambda g, kk, j, r: (r, g * nkb + kk)) if diag else (lambda g, kk, j, r: (r, kk))
    return pl.pallas_call(
        body, name=name,
        grid=(G, nkb, nn, nm),
        in_specs=[pl.BlockSpec((tm, tk), a_map),
                  pl.BlockSpec((tm, tn), lambda g, kk, j, r: (r, g * nn + j))],
        out_specs=pl.BlockSpec((None, tk, tn), lambda g, kk, j, r: (g, kk, j)),
        out_shape=jax.ShapeDtypeStruct((G, K, n), out_dtype),
        scratch_shapes=[pltpu.VMEM((tk, tn) if nm > 1 else (SUBLANE, LANE), F32)],
        compiler_params=_cparams(("parallel", "parallel", "parallel", "arbitrary")),
    )(a, dy)


def blockdiag_nn(a, b, transpose_b, name):
    M = a.shape[0]
    G, K, n = b.shape
    kin, kout = (n, K) if transpose_b else (K, n)
    tm = _div_tile(M, 768, SUBLANE)
    dims = ((1,), (1,)) if transpose_b else ((1,), (0,))

    def body(a_ref, b_ref, o_ref):
        for g in range(G):
            o_ref[:, g * kout:(g + 1) * kout] = _dot(a_ref[:, g * kin:(g + 1) * kin], b_ref[g], dims)

    return pl.pallas_call(
        body, name=name, grid=(M // tm,),
        in_specs=[pl.BlockSpec((tm, G * kin), lambda i: (i, 0)), pl.BlockSpec((G, K, n), lambda i: (0, 0, 0))],
        out_specs=pl.BlockSpec((tm, G * kout), lambda i: (i, 0)),
        out_shape=jax.ShapeDtypeStruct((M, G * kout), F32),
        compiler_params=_cparams(("parallel",)),
    )(a, b)


def blockdiag_tn(a, dy, G, name):
    M = a.shape[0]
    K, n = a.shape[1] // G, dy.shape[1] // G
    tm = _div_tile(M, 768, SUBLANE)

    def body(a_ref, dy_ref, o_ref):
        @pl.when(pl.program_id(0) == 0)
        def _():
            o_ref[...] = jnp.zeros_like(o_ref)

        for g in range(G):
            o_ref[g] += _dot(a_ref[:, g * K:(g + 1) * K], dy_ref[:, g * n:(g + 1) * n], ((0,), (0,)))

    return pl.pallas_call(
        body, name=name, grid=(M // tm,),
        in_specs=[pl.BlockSpec((tm, G * K), lambda i: (i, 0)), pl.BlockSpec((tm, G * n), lambda i: (i, 0))],
        out_specs=pl.BlockSpec((G, K, n), lambda i: (0, 0, 0)),
        out_shape=jax.ShapeDtypeStruct((G, K, n), F32),
        compiler_params=_cparams(("arbitrary",)),
    )(a, dy)


def blockdiag_linear(a, w, name):
    G = w.shape[0]

    @jax.custom_vjp
    def op(a, w):
        return blockdiag_nn(a, w, False, name + "_fwd")

    def fwd(a, w):
        return blockdiag_nn(a, w, False, name + "_fwd"), (a, w)

    def bwd(res, dy):
        a, w = res
        return blockdiag_nn(dy, w, True, name + "_dx"), blockdiag_tn(a, dy, G, name + "_dw")

    op.defvjp(fwd, bwd)
    return op(a, w)


def linear(a, w, w_grad_proxy, diag=False, grad_dtype=F32, name="lin"):
    G = w.shape[0]

    @jax.custom_vjp
    def op(a, w, proxy):
        return matmul_nn(a, w, diag, name + "_fwd")

    def fwd(a, w, proxy):
        return matmul_nn(a, w, diag, name + "_fwd"), (a, w)

    def bwd(res, dy):
        a, w = res
        da = matmul_nt(dy, w, diag, a.dtype, name + "_dx")
        dw = matmul_tn(a, dy, G, diag, grad_dtype, name + "_dw")
        return da, None, dw

    op.defvjp(fwd, bwd)
    return op(a, w, w_grad_proxy)


def rowwise(name, f, blocks, params, pkinds, n_out, tb, cb=None, nograd=()):
    rows = blocks[0].shape[0]
    tb = min(tb, rows)
    assert rows % tb == 0
    nrb = rows // tb
    nb, npar = len(blocks), len(params)
    widths = [b.shape[1] for b in blocks]
    if cb is None:
        ncb = 1
        bw = widths
    else:
        assert all(w == widths[0] for w in widths) and widths[0] % cb == 0
        ncb = widths[0] // cb
        bw = [cb] * nb
    pw = []
    for p, kind in zip(params, pkinds):
        full = p.shape[-1]
        pw.append(full if cb is None or full != widths[0] else cb)
    for p, kind in zip(params, pkinds):
        assert p.shape[:-1] == ((1,) if kind == "vec" else (2, 1)), (name, p.shape, kind)

    shapes = jax.eval_shape(
        f, *[jax.ShapeDtypeStruct((tb, w), F32) for w in bw],
        *[jax.ShapeDtypeStruct((1, w), p.dtype) for w, p in zip(pw, params)])
    shapes = tuple(shapes)
    out_sh, acc_sh = shapes[:n_out], shapes[n_out:]
    n_acc = len(acc_sh)
    for s in out_sh:
        assert s.shape[0] == tb
    for s in acc_sh:
        assert s.shape[0] == 1

    def blk_spec(w):
        return pl.BlockSpec((tb, w), lambda c, r: (r, c))

    def par_spec(w, kind, full):
        col = (lambda c: c) if (cb is not None and full == widths[0]) else (lambda c: 0)
        if kind == "vec":
            return pl.BlockSpec((1, w), lambda c, r: (0, col(c)))
        return pl.BlockSpec((None, 1, w), lambda c, r: (jnp.minimum(r, 1), 0, col(c)))

    blk_specs = [blk_spec(w) for w in bw]
    par_specs = [par_spec(w, k, p.shape[-1]) for w, k, p in zip(pw, pkinds, params)]
    sem = _cparams(("parallel", "arbitrary"))

    def run_fwd(blocks, params):
        def body(*refs):
            ins = [r[...].astype(F32) for r in refs[:nb + npar]]
            outs = refs[nb + npar:]
            res = f(*ins)
            r_id = pl.program_id(1)
            for o_ref, val in zip(outs[:n_out], res[:n_out]):
                o_ref[...] = val
            for o_ref, val in zip(outs[n_out:], res[n_out:]):
                @pl.when(r_id == 0)
                def _(o_ref=o_ref, val=val):
                    o_ref[...] = val

                @pl.when(r_id > 0)
                def _(o_ref=o_ref, val=val):
                    o_ref[...] += val

        out_specs = [blk_spec(s.shape[1]) for s in out_sh] + \
                    [pl.BlockSpec((1, s.shape[1]), lambda c, r: (0, c)) for s in acc_sh]
        out_shape = [jax.ShapeDtypeStruct((rows, s.shape[1] * ncb), s.dtype) for s in out_sh] + \
                    [jax.ShapeDtypeStruct((1, s.shape[1] * ncb), s.dtype) for s in acc_sh]
        return tuple(pl.pallas_call(
            body, name=name + "_fwd", grid=(ncb, nrb),
            in_specs=blk_specs + par_specs, out_specs=out_specs, out_shape=out_shape,
            compiler_params=sem)(*blocks, *params))

    def run_bwd(blocks, params, cts):
        d_outs, d_accs = list(cts[:n_out]), list(cts[n_out:])
        want = [i for i in range(nb) if i not in nograd]

        def body(*refs):
            k = nb + npar
            ins = [r[...].astype(F32) for r in refs[:k]]
            ct = tuple(r[...] for r in refs[k:k + n_out + n_acc])
            outs = refs[k + n_out + n_acc:]
            _, vjp = jax.vjp(lambda *a: tuple(f(*a)), *ins)
            grads = vjp(ct)
            r_id = pl.program_id(1)
            for o_ref, i in zip(outs[:len(want)], want):
                o_ref[...] = grads[i].astype(o_ref.dtype)
            for o_ref, g, kind in zip(outs[len(want):], grads[nb:], pkinds):
                first = (r_id == 0) if kind == "vec" else (r_id <= 1)

                @pl.when(first)
                def _(o_ref=o_ref, g=g):
                    o_ref[...] = g.astype(o_ref.dtype)

                @pl.when(jnp.logical_not(first))
                def _(o_ref=o_ref, g=g):
                    o_ref[...] += g.astype(o_ref.dtype)

        ct_specs = [blk_spec(s.shape[1]) for s in out_sh] + \
                   [pl.BlockSpec((1, s.shape[1]), lambda c, r: (0, c)) for s in acc_sh]
        out_specs = [blk_specs[i] for i in want] + par_specs
        out_shape = [jax.ShapeDtypeStruct(blocks[i].shape, blocks[i].dtype) for i in want] + \
                    [jax.ShapeDtypeStruct(p.shape, p.dtype) for p in params]
        res = pl.pallas_call(
            body, name=name + "_bwd", grid=(ncb, nrb),
            in_specs=blk_specs + par_specs + ct_specs, out_specs=out_specs, out_shape=out_shape,
            compiler_params=sem)(*blocks, *params, *d_outs, *d_accs)
        d_blocks = [None] * nb
        for i, g in zip(want, res[:len(want)]):
            d_blocks[i] = g
        return tuple(d_blocks), tuple(res[len(want):])

    @jax.custom_vjp
    def op(blocks, params):
        return run_fwd(blocks, params)

    def op_fwd(blocks, params):
        return run_fwd(blocks, params), (blocks, params)

    def op_bwd(res, cts):
        return run_bwd(res[0], res[1], cts)

    op.defvjp(op_fwd, op_bwd)
    return op(tuple(blocks), tuple(params))


def _conv_window(prev_ref, cur_ref, next_ref, win_ref, i, tb, rows, n_ctx):
    starts = (i == 0) | (i * tb == n_ctx)
    ends = ((i + 1) * tb == rows) | ((i + 1) * tb == n_ctx)
    win_ref[pl.ds(0, CONV_HALO), :] = jnp.where(starts, 0.0, prev_ref[...].astype(F32))
    win_ref[pl.ds(CONV_HALO, tb), :] = cur_ref[...].astype(F32)
    win_ref[pl.ds(CONV_HALO + tb, CONV_HALO), :] = jnp.where(ends, 0.0, next_ref[...].astype(F32))


def _conv_specs(rows, tb, cb):
    hb = tb // CONV_HALO
    last = rows // CONV_HALO - 1
    prev = pl.BlockSpec((CONV_HALO, cb), lambda c, i: (jnp.maximum(i * hb - 1, 0), c))
    cur = pl.BlockSpec((tb, cb), lambda c, i: (i, c))
    nxt = pl.BlockSpec((CONV_HALO, cb), lambda c, i: (jnp.minimum((i + 1) * hb, last), c))
    return prev, cur, nxt


CONV_CHUNK = 32


def _conv_taps(win_ref, phase_ref, K, pad_left, tb):
    phased = K > SUBLANE
    if phased:
        span = tb + 2 * CONV_HALO - SUBLANE
        for p in range(SUBLANE):
            phase_ref[p] = win_ref[pl.ds(p, span), :]

    def tap(k, r0):
        off = CONV_HALO + k - pad_left
        if phased:
            return phase_ref[off % SUBLANE, pl.ds(r0 + off - off % SUBLANE, CONV_CHUNK), :]
        return win_ref[pl.ds(r0 + off, CONV_CHUNK), :]

    return tap


def _conv_scratch(K, tb, cb):
    phases = (SUBLANE, tb + 2 * CONV_HALO - SUBLANE, cb) if K > SUBLANE else (1, SUBLANE, LANE)
    return [pltpu.VMEM((tb + 2 * CONV_HALO, cb), F32), pltpu.VMEM(phases, F32)]


def _dwconv_apply(x, w, b, pad_left, n_ctx, out_dtype, name):
    rows, C = x.shape
    K = w.shape[0]
    tb = 256
    cb = _div_tile(C, 512, LANE)
    assert rows % tb == 0 and n_ctx % tb == 0 and K - 1 <= CONV_HALO and tb % CONV_CHUNK == 0

    def body(prev_ref, cur_ref, next_ref, w_ref, b_ref, o_ref, win_ref, phase_ref):
        i = pl.program_id(1)
        _conv_window(prev_ref, cur_ref, next_ref, win_ref, i, tb, rows, n_ctx)
        tap = _conv_taps(win_ref, phase_ref, K, pad_left, tb)
        wk = [w_ref[pl.ds(k, 1), :] for k in range(K)]
        for r0 in range(0, tb, CONV_CHUNK):
            acc = b_ref[...] + tap(0, r0) * wk[0]
            for k in range(1, K):
                acc = acc + tap(k, r0) * wk[k]
            o_ref[pl.ds(r0, CONV_CHUNK), :] = acc.astype(out_dtype)

    prev, cur, nxt = _conv_specs(rows, tb, cb)
    return pl.pallas_call(
        body, name=name, grid=(C // cb, rows // tb),
        in_specs=[prev, cur, nxt, pl.BlockSpec((K, cb), lambda c, i: (0, c)),
                  pl.BlockSpec((1, cb), lambda c, i: (0, c))],
        out_specs=pl.BlockSpec((tb, cb), lambda c, i: (i, c)),
        out_shape=jax.ShapeDtypeStruct((rows, C), out_dtype),
        scratch_shapes=_conv_scratch(K, tb, cb),
        compiler_params=_cparams(("parallel", "arbitrary")),
    )(x, x, x, w, b)


def _dwconv_wgrad(x, dy, K, pad_left, n_ctx, name):
    rows, C = x.shape
    tb = 256
    cb = _div_tile(C, 512, LANE)

    def body(prev_ref, cur_ref, next_ref, dy_ref, dw_ref, db_ref, win_ref, phase_ref):
        i = pl.program_id(1)
        _conv_window(prev_ref, cur_ref, next_ref, win_ref, i, tb, rows, n_ctx)
        tap = _conv_taps(win_ref, phase_ref, K, pad_left, tb)

        @pl.when(i == 0)
        def _():
            dw_ref[...] = jnp.zeros_like(dw_ref)
            db_ref[...] = jnp.zeros_like(db_ref)

        chunks = range(0, tb, CONV_CHUNK)
        dy = lambda r0: dy_ref[pl.ds(r0, CONV_CHUNK), :].astype(F32)
        db_ref[...] += jnp.sum(sum(dy(r0) for r0 in chunks), axis=0, keepdims=True)
        for k in range(K):
            acc = sum(tap(k, r0) * dy(r0) for r0 in chunks)
            dw_ref[pl.ds(k, 1), :] += jnp.sum(acc, axis=0, keepdims=True)

    prev, cur, nxt = _conv_specs(rows, tb, cb)
    return pl.pallas_call(
        body, name=name, grid=(C // cb, rows // tb),
        in_specs=[prev, cur, nxt, pl.BlockSpec((tb, cb), lambda c, i: (i, c))],
        out_specs=[pl.BlockSpec((K, cb), lambda c, i: (0, c)), pl.BlockSpec((1, cb), lambda c, i: (0, c))],
        out_shape=[jax.ShapeDtypeStruct((K, C), F32), jax.ShapeDtypeStruct((1, C), F32)],
        scratch_shapes=_conv_scratch(K, tb, cb),
        compiler_params=_cparams(("parallel", "arbitrary")),
    )(x, x, x, dy)


def dwconv(x, w, b, pad_left, n_ctx, name):
    K = w.shape[0]

    @jax.custom_vjp
    def op(x, w, b):
        return _dwconv_apply(x, w, b, pad_left, n_ctx, F32, name + "_fwd")

    def fwd(x, w, b):
        return _dwconv_apply(x, w, b, pad_left, n_ctx, F32, name + "_fwd"), (x, w)

    def bwd(res, dy):
        x, w = res
        dx = _dwconv_apply(dy, w[::-1], jnp.zeros((1, w.shape[1]), F32), K - 1 - pad_left, n_ctx, x.dtype,
                           name + "_dx")
        dw, db = _dwconv_wgrad(x, dy, K, pad_left, n_ctx, name + "_dw")
        return dx, dw, db

    op.defvjp(fwd, bwd)
    return op(x, w, b)


GLU_HALO = 16


def _glu_specs(rows, tb, cb):
    hb = tb // GLU_HALO
    last = rows // GLU_HALO - 1
    prev = pl.BlockSpec((GLU_HALO, cb), lambda c, i: (jnp.maximum(i * hb - 1, 0), c))
    cur = pl.BlockSpec((tb, cb), lambda c, i: (i, c))
    nxt = pl.BlockSpec((GLU_HALO, cb), lambda c, i: (jnp.minimum((i + 1) * hb, last), c))
    return [prev, cur, nxt]


GLU_CHUNK = 16
GLU_PAD = 8


def _glu_window(prev_ref, cur_ref, next_ref, win_ref, i, tb, rows, n_ctx, pad):
    starts = (i == 0) | (i * tb == n_ctx)
    ends = ((i + 1) * tb == rows) | ((i + 1) * tb == n_ctx)
    win_ref[pl.ds(pad, GLU_HALO), :] = jnp.where(starts, 0.0, prev_ref[...].astype(F32))
    win_ref[pl.ds(pad + GLU_HALO, tb), :] = cur_ref[...].astype(F32)
    win_ref[pl.ds(pad + GLU_HALO + tb, GLU_HALO), :] = jnp.where(ends, 0.0, next_ref[...].astype(F32))
    return starts, ends


def _glu_taps(gwin_ref, row):
    return [gwin_ref[pl.ds(row - 1 + k, GLU_CHUNK), :] for k in range(3)]


def _ffn_glu_fwd(g, v, w, b, n_ctx, name):
    rows, C = g.shape
    tb = ROW_BLOCK
    cb = _div_tile(C, 1408, LANE)
    assert rows % tb == 0 and n_ctx % tb == 0 and w.shape[0] == 3 and tb % GLU_CHUNK == 0

    def body(gp_ref, gc_ref, gn_ref, v_ref, w_ref, b_ref, o_ref, gwin_ref):
        i = pl.program_id(1)
        _glu_window(gp_ref, gc_ref, gn_ref, gwin_ref, i, tb, rows, n_ctx, 0)
        w0, w1, w2, bias = w_ref[pl.ds(0, 1), :], w_ref[pl.ds(1, 1), :], w_ref[pl.ds(2, 1), :], b_ref[...]
        for j in range(tb // GLU_CHUNK):
            own = pl.ds(j * GLU_CHUNK, GLU_CHUNK)
            t = _glu_taps(gwin_ref, GLU_HALO + j * GLU_CHUNK)
            gate = bias + t[0] * w0 + t[1] * w1 + t[2] * w2
            o_ref[own, :] = (gate * jax.nn.sigmoid(gate) * v_ref[own, :].astype(F32)).astype(BF16)

    blk = pl.BlockSpec((tb, cb), lambda c, i: (i, c))
    return pl.pallas_call(
        body, name=name, grid=(C // cb, rows // tb),
        in_specs=_glu_specs(rows, tb, cb) + [blk, pl.BlockSpec((3, cb), lambda c, i: (0, c)),
                                             pl.BlockSpec((1, cb), lambda c, i: (0, c))],
        out_specs=blk, out_shape=jax.ShapeDtypeStruct((rows, C), BF16),
        scratch_shapes=[pltpu.VMEM((tb + 2 * GLU_HALO, cb), F32)],
        compiler_params=_cparams(("parallel", "arbitrary")),
    )(g, g, g, v, w, b)


def _ffn_glu_bwd(g, v, w, b, da, n_ctx, name):
    rows, C = g.shape
    tb = ROW_BLOCK
    cb = _div_tile(C, 1408, LANE)
    n_chunks = (tb + 2 * GLU_HALO) // GLU_CHUNK
    assert GLU_HALO == GLU_CHUNK

    def body(gp_ref, gc_ref, gn_ref, vp_ref, vc_ref, vn_ref, dp_ref, dc_ref, dn_ref, w_ref, b_ref,
             dg_ref, dv_ref, dw_ref, db_ref, gwin_ref, dgate_ref):
        i = pl.program_id(1)
        gwin_ref[pl.ds(0, GLU_PAD), :] = jnp.zeros((GLU_PAD, cb), F32)
        gwin_ref[pl.ds(GLU_PAD + tb + 2 * GLU_HALO, GLU_PAD), :] = jnp.zeros((GLU_PAD, cb), F32)
        starts, ends = _glu_window(gp_ref, gc_ref, gn_ref, gwin_ref, i, tb, rows, n_ctx, GLU_PAD)
        w0, w1, w2, bias = w_ref[pl.ds(0, 1), :], w_ref[pl.ds(1, 1), :], w_ref[pl.ds(2, 1), :], b_ref[...]
        db_acc = jnp.zeros((GLU_CHUNK, cb), F32)
        dw_acc = [jnp.zeros((GLU_CHUNK, cb), F32) for _ in range(3)]
        for j in range(n_chunks):
            t = _glu_taps(gwin_ref, GLU_PAD + j * GLU_CHUNK)
            gate = bias + t[0] * w0 + t[1] * w1 + t[2] * w2
            sig = jax.nn.sigmoid(gate)
            own = pl.ds((j - 1) * GLU_CHUNK, GLU_CHUNK)
            if j == 0:
                dout, val = jnp.where(starts, 0.0, dp_ref[...].astype(F32)), vp_ref[...].astype(F32)
            elif j == n_chunks - 1:
                dout, val = jnp.where(ends, 0.0, dn_ref[...].astype(F32)), vn_ref[...].astype(F32)
            else:
                dout, val = dc_ref[own, :].astype(F32), vc_ref[own, :].astype(F32)
            dgate = dout * val * (sig * (1.0 + gate * (1.0 - sig)))
            dgate_ref[pl.ds(j * GLU_CHUNK, GLU_CHUNK), :] = dgate
            if 0 < j < n_chunks - 1:
                dv_ref[own, :] = (dout * (gate * sig)).astype(dv_ref.dtype)
                db_acc = db_acc + dgate
                dw_acc = [acc + dgate * tap for acc, tap in zip(dw_acc, t)]
        for j in range(tb // GLU_CHUNK):
            r = GLU_HALO + j * GLU_CHUNK
            dg = (dgate_ref[pl.ds(r + 1, GLU_CHUNK), :] * w0 + dgate_ref[pl.ds(r, GLU_CHUNK), :] * w1
                  + dgate_ref[pl.ds(r - 1, GLU_CHUNK), :] * w2)
            dg_ref[pl.ds(j * GLU_CHUNK, GLU_CHUNK), :] = dg.astype(dg_ref.dtype)

        @pl.when(i == 0)
        def _():
            dw_ref[...] = jnp.zeros_like(dw_ref)
            db_ref[...] = jnp.zeros_like(db_ref)

        db_ref[...] += jnp.sum(db_acc, axis=0, keepdims=True)
        for k in range(3):
            dw_ref[pl.ds(k, 1), :] += jnp.sum(dw_acc[k], axis=0, keepdims=True)

    blk = pl.BlockSpec((tb, cb), lambda c, i: (i, c))
    specs = _glu_specs(rows, tb, cb)
    return pl.pallas_call(
        body, name=name, grid=(C // cb, rows // tb),
        in_specs=specs * 3 + [pl.BlockSpec((3, cb), lambda c, i: (0, c)), pl.BlockSpec((1, cb), lambda c, i: (0, c))],
        out_specs=[blk, blk, pl.BlockSpec((3, cb), lambda c, i: (0, c)), pl.BlockSpec((1, cb), lambda c, i: (0, c))],
        out_shape=[jax.ShapeDtypeStruct((rows, C), g.dtype), jax.ShapeDtypeStruct((rows, C), v.dtype),
                   jax.ShapeDtypeStruct((3, C), F32), jax.ShapeDtypeStruct((1, C), F32)],
        scratch_shapes=[pltpu.VMEM((tb + 2 * GLU_HALO + 2 * GLU_PAD, cb), F32),
                        pltpu.VMEM((tb + 2 * GLU_HALO, cb), F32)],
        compiler_params=_cparams(("parallel", "arbitrary")),
    )(g, g, g, v, v, v, da, da, da, w, b)


def ffn_glu(g, v, w, b, n_ctx, name):
    @jax.custom_vjp
    def op(g, v, w, b):
        return _ffn_glu_fwd(g, v, w, b, n_ctx, name + "_fwd")

    def fwd(g, v, w, b):
        return _ffn_glu_fwd(g, v, w, b, n_ctx, name + "_fwd"), (g, v, w, b)

    def bwd(res, da):
        g, v, w, b = res
        return tuple(_ffn_glu_bwd(g, v, w, b, da, n_ctx, name + "_bwd"))

    op.defvjp(fwd, bwd)
    return op(g, v, w, b)


def _block_scan(a, b, reverse):
    tb = a.shape[0]
    row = lax.broadcasted_iota(jnp.int32, (tb, 1), 0)
    s = 1
    while s < tb:
        if reverse:
            keep = row < tb - s
            a_sh = pltpu.roll(a, tb - s, 0)
            b_sh = pltpu.roll(b, tb - s, 0)
        else:
            keep = row >= s
            a_sh = pltpu.roll(a, s, 0)
            b_sh = pltpu.roll(b, s, 0)
        b = jnp.where(keep, a * b_sh + b, b)
        a = jnp.where(keep, a * a_sh, a)
        s *= 2
    return a, b


def _shift_in(h, carry, reverse):
    tb = h.shape[0]
    row = lax.broadcasted_iota(jnp.int32, (tb, 1), 0)
    if reverse:
        return jnp.where(row == tb - 1, carry, pltpu.roll(h, tb - 1, 0))
    return jnp.where(row == 0, carry, pltpu.roll(h, 1, 0))


def _scan_maps(nrb, rot, reverse):
    def phys(c, i):
        q = (nrb - 1 - i) if reverse else i
        return (lax.rem(q + rot, nrb), c)
    return phys


def _scan_fwd(a, b, rot, reverse, name):
    rows, C = a.shape
    tb = 256
    cb = _div_tile(C, 512, LANE)
    nrb = rows // tb
    last_row = 0 if reverse else tb - 1

    def body(a_ref, b_ref, h_ref, hp_ref, carry_ref):
        i = pl.program_id(1)

        @pl.when(i == 0)
        def _():
            carry_ref[...] = jnp.zeros_like(carry_ref)

        carry = carry_ref[pl.ds(0, 1), :]
        A, B = _block_scan(a_ref[...], b_ref[...], reverse)
        h = A * carry + B
        h_ref[...] = h
        hp_ref[...] = _shift_in(h, carry, reverse)
        carry_ref[pl.ds(0, 1), :] = h[last_row:last_row + 1, :]

    spec = pl.BlockSpec((tb, cb), _scan_maps(nrb, rot, reverse))
    return pl.pallas_call(
        body, name=name, grid=(C // cb, nrb),
        in_specs=[spec, spec], out_specs=[spec, spec],
        out_shape=[jax.ShapeDtypeStruct((rows, C), F32)] * 2,
        scratch_shapes=[pltpu.VMEM((SUBLANE, cb), F32)],
        compiler_params=_cparams(("parallel", "arbitrary")),
    )(a, b)


def _scan_bwd(a, dh, h_prev, rot, reverse, name):
    rows, C = a.shape
    tb = 256
    cb = _div_tile(C, 512, LANE)
    nrb = rows // tb
    adj = not reverse
    last_row = 0 if adj else tb - 1

    def body(a_ref, dh_ref, hp_ref, da_ref, db_ref, carry_ref):
        i = pl.program_id(1)

        @pl.when(i == 0)
        def _():
            carry_ref[...] = jnp.zeros_like(carry_ref)

        carry = carry_ref[pl.ds(0, 1), :]
        a = a_ref[...]
        dh = dh_ref[...]
        A, B = _block_scan(a, a * dh, adj)
        u = A * carry + B
        g = dh + _shift_in(u, carry, adj)
        db_ref[...] = g
        da_ref[...] = g * hp_ref[...]
        carry_ref[pl.ds(0, 1), :] = u[last_row:last_row + 1, :]

    spec = pl.BlockSpec((tb, cb), _scan_maps(nrb, rot, adj))
    return pl.pallas_call(
        body, name=name, grid=(C // cb, nrb),
        in_specs=[spec, spec, spec], out_specs=[spec, spec],
        out_shape=[jax.ShapeDtypeStruct((rows, C), F32)] * 2,
        scratch_shapes=[pltpu.VMEM((SUBLANE, cb), F32)],
        compiler_params=_cparams(("parallel", "arbitrary")),
    )(a, dh, h_prev)


def linear_scan(a, b, rot, reverse, name):
    @jax.custom_vjp
    def op(a, b):
        return _scan_fwd(a, b, rot, reverse, name + "_fwd")[0]

    def fwd(a, b):
        h, hp = _scan_fwd(a, b, rot, reverse, name + "_fwd")
        return h, (a, hp)

    def bwd(res, dh):
        a, hp = res
        da, db = _scan_bwd(a, dh, hp, rot, reverse, name + "_bwd")
        return da, db

    op.defvjp(fwd, bwd)
    return op(a, b)


def rope_tables(n_ctx, n_lat):
    t = jnp.arange(n_lat)
    pos = jnp.stack([t // GRID_W, t % GRID_W], axis=-1).astype(F32)
    freq = ROPE_THETA ** (-jnp.arange(ROPE_FREQS, dtype=F32) / ROPE_FREQS)
    ang = pos[:, :, None] * freq
    cos, sin = jnp.cos(ang), jnp.sin(ang)
    c = jnp.concatenate([cos[:, 0], cos[:, 0], cos[:, 1], cos[:, 1]], axis=-1)
    s = jnp.concatenate([-sin[:, 0], sin[:, 0], -sin[:, 1], sin[:, 1]], axis=-1)
    c = jnp.concatenate([jnp.ones((n_ctx, HEAD_DIM), F32), c], axis=0)
    s = jnp.concatenate([jnp.zeros((n_ctx, HEAD_DIM), F32), s], axis=0)
    return c, s


def _rope_apply(qkv, c_tab, s_tab, n_rot_heads, out_dtype, name):
    rows, cols = qkv.shape
    tb = _div_tile(rows, 768, SUBLANE)
    heads = cols // HEAD_DIM
    hb = max(h for h in (4, 2, 1) if heads % h == 0 and n_rot_heads % h == 0)
    wb = hb * HEAD_DIM

    def body(x_ref, c_ref, s_ref, o_ref):
        x = x_ref[...].astype(F32)
        lane = lax.broadcasted_iota(jnp.int32, x.shape, 1)
        swapped = jnp.where((lane & 63) < 32, pltpu.roll(x, wb - 32, 1), pltpu.roll(x, 32, 1))
        roped = x * jnp.tile(c_ref[...], (1, hb)) + swapped * jnp.tile(s_ref[...], (1, hb))
        o_ref[...] = jnp.where(pl.program_id(1) * hb < n_rot_heads, roped, x).astype(out_dtype)

    tab = pl.BlockSpec((tb, HEAD_DIM), lambda i, j: (i, 0))
    blk = pl.BlockSpec((tb, wb), lambda i, j: (i, j))
    return pl.pallas_call(
        body, name=name, grid=(rows // tb, cols // wb),
        in_specs=[blk, tab, tab], out_specs=blk,
        out_shape=jax.ShapeDtypeStruct((rows, cols), out_dtype),
        compiler_params=_cparams(("parallel", "arbitrary")),
    )(qkv, c_tab, s_tab)


def rope(qkv, c_tab, s_tab, n_rot_heads, name):
    @jax.custom_vjp
    def op(qkv):
        return _rope_apply(qkv, c_tab, s_tab, n_rot_heads, BF16, name + "_fwd")

    def fwd(qkv):
        return _rope_apply(qkv, c_tab, s_tab, n_rot_heads, BF16, name + "_fwd"), None

    def bwd(_, d):
        return (_rope_apply(d, c_tab, -s_tab, n_rot_heads, F32, name + "_bwd"),)

    op.defvjp(fwd, bwd)
    return op(qkv)


def _attn_in_specs(H, KV, n_ctx, nqb):
    ncb = n_ctx // WINDOW_BLOCK
    G = GQA_GROUP

    def loc(delta, col0):
        return pl.BlockSpec((WINDOW_BLOCK, HEAD_DIM),
                            lambda g, i: (jnp.clip(i + delta, ncb, nqb - 1), col0 + g))

    q = pl.BlockSpec((WINDOW_BLOCK, G * HEAD_DIM), lambda g, i: (i, g))
    kc = pl.BlockSpec((n_ctx, HEAD_DIM), lambda g, i: (0, H + g))
    vc = pl.BlockSpec((n_ctx, HEAD_DIM), lambda g, i: (0, H + KV + g))
    sink = pl.BlockSpec((None, G * WINDOW_BLOCK, 1), lambda g, i: (g, 0, 0))
    bias = pl.BlockSpec((None, G * WINDOW_BLOCK, n_ctx + 3 * WINDOW_BLOCK),
                        lambda g, i: (jnp.where(i < ncb, 3, jnp.where(i == ncb, 1, jnp.where(i == nqb - 1, 2, 0))),
                                      0, 0))
    return [q, kc, loc(-1, H), loc(0, H), loc(1, H), vc, loc(-1, H + KV), loc(0, H + KV), loc(1, H + KV), sink, bias]


def _attn_bias(n_ctx):
    nq, nk = GQA_GROUP * WINDOW_BLOCK, n_ctx + 3 * WINDOW_BLOCK
    r = (jnp.arange(nq) % WINDOW_BLOCK)[:, None]
    col = jnp.arange(nk)[None, :]
    blk = (col - n_ctx) // WINDOW_BLOCK
    rk = (col - n_ctx) % WINDOW_BLOCK
    is_ctx = jnp.broadcast_to(col < n_ctx, (nq, nk))
    prev = (blk == 0) & (rk >= r)
    cur = jnp.broadcast_to(blk == 1, (nq, nk))
    nxt = (blk == 2) & (rk <= r)
    valid = jnp.stack([is_ctx | prev | cur | nxt, is_ctx | cur | nxt, is_ctx | prev | cur, is_ctx])
    return jnp.where(valid, 0.0, NEG_INF).astype(F32)


def _stack_heads(x):
    return jnp.concatenate([x[:, h * HEAD_DIM:(h + 1) * HEAD_DIM] for h in range(GQA_GROUP)], axis=0)


def _attn_probs(q_ref, kc_ref, kp_ref, kcur_ref, kn_ref, sink_ref, bias_ref):
    qs = _stack_heads(q_ref[...]).astype(BF16)
    k = jnp.concatenate([kc_ref[...], kp_ref[...], kcur_ref[...], kn_ref[...]], axis=0).astype(BF16)
    s = _dot(qs, k, ((1,), (1,))) * (HEAD_DIM ** -0.5) + bias_ref[...]
    sk = sink_ref[...]
    m = jnp.maximum(jnp.max(s, axis=1, keepdims=True), sk)
    e = jnp.exp(s - m)
    es = jnp.exp(sk - m)
    inv = 1.0 / (jnp.sum(e, axis=1, keepdims=True) + es)
    return qs, k, e * inv, es * inv


def _attn_fwd(qkv, sink_col, H, KV, n_ctx, name):
    rows = qkv.shape[0]
    nqb = rows // WINDOW_BLOCK
    assert (rows - n_ctx) // WINDOW_BLOCK >= 2
    G = GQA_GROUP

    def body(q_ref, kc_ref, kp_ref, kcur_ref, kn_ref, vc_ref, vp_ref, vcur_ref, vn_ref, sink_ref, bias_ref, o_ref):
        _, _, p, _ = _attn_probs(q_ref, kc_ref, kp_ref, kcur_ref, kn_ref, sink_ref, bias_ref)
        v = jnp.concatenate([vc_ref[...], vp_ref[...], vcur_ref[...], vn_ref[...]], axis=0).astype(BF16)
        o = _dot(p, v, ((1,), (0,)))
        for h in range(G):
            o_ref[:, h * HEAD_DIM:(h + 1) * HEAD_DIM] = o[h * WINDOW_BLOCK:(h + 1) * WINDOW_BLOCK, :].astype(BF16)

    return pl.pallas_call(
        body, name=name, grid=(KV, nqb),
        in_specs=_attn_in_specs(H, KV, n_ctx, nqb),
        out_specs=pl.BlockSpec((WINDOW_BLOCK, G * HEAD_DIM), lambda g, i: (i, g)),
        out_shape=jax.ShapeDtypeStruct((rows, H * HEAD_DIM), BF16),
        compiler_params=_cparams(("parallel", "arbitrary")),
    )(*([qkv] * 9), sink_col, _attn_bias(n_ctx))


def _attn_bwd(qkv, sink_col, o, do, H, KV, n_ctx, name):
    rows = qkv.shape[0]
    nqb = rows // WINDOW_BLOCK
    G = GQA_GROUP
    WB = WINDOW_BLOCK

    def body(q_ref, kc_ref, kp_ref, kcur_ref, kn_ref, vc_ref, vp_ref, vcur_ref, vn_ref, sink_ref, bias_ref, o_ref,
             do_ref, dq_ref, dkc_ref, dvc_ref, dkp_ref, dkcur_ref, dkn_ref, dvp_ref, dvcur_ref, dvn_ref, dsink_ref):
        i = pl.program_id(1)
        qs, k, p, ps = _attn_probs(q_ref, kc_ref, kp_ref, kcur_ref, kn_ref, sink_ref, bias_ref)
        v = jnp.concatenate([vc_ref[...], vp_ref[...], vcur_ref[...], vn_ref[...]], axis=0).astype(BF16)
        do_s = _stack_heads(do_ref[...]).astype(F32)
        o_s = _stack_heads(o_ref[...]).astype(F32)
        delta = jnp.sum(do_s * o_s, axis=1, keepdims=True)
        dp = _dot(do_s, v, ((1,), (1,)))
        ds = p * (dp - delta) * (HEAD_DIM ** -0.5)
        dq = _dot(ds, k, ((1,), (0,)))
        dk = _dot(ds, qs, ((0,), (0,)))
        dv = _dot(p, do_s, ((0,), (0,)))
        for h in range(G):
            dq_ref[:, h * HEAD_DIM:(h + 1) * HEAD_DIM] = dq[h * WB:(h + 1) * WB, :]

        @pl.when(i == 0)
        def _():
            dkc_ref[...] = jnp.zeros_like(dkc_ref)
            dvc_ref[...] = jnp.zeros_like(dvc_ref)
            dsink_ref[...] = jnp.zeros_like(dsink_ref)

        dkc_ref[...] += dk[:n_ctx]
        dvc_ref[...] += dv[:n_ctx]
        dsink_ref[...] += -ps * delta
        for j, (dk_ref, dv_ref) in enumerate(((dkp_ref, dvp_ref), (dkcur_ref, dvcur_ref), (dkn_ref, dvn_ref))):
            dk_ref[...] = dk[n_ctx + j * WB:n_ctx + (j + 1) * WB]
            dv_ref[...] = dv[n_ctx + j * WB:n_ctx + (j + 1) * WB]

    qblk = pl.BlockSpec((WB, G * HEAD_DIM), lambda g, i: (i, g))
    ctx = pl.BlockSpec((n_ctx, HEAD_DIM), lambda g, i: (0, g))
    piece = pl.BlockSpec((WB, HEAD_DIM), lambda g, i: (i, g))
    sink = pl.BlockSpec((None, G * WB, 1), lambda g, i: (g, 0, 0))
    kv_shape = jax.ShapeDtypeStruct((rows, KV * HEAD_DIM), F32)
    ctx_shape = jax.ShapeDtypeStruct((n_ctx, KV * HEAD_DIM), F32)
    return pl.pallas_call(
        body, name=name, grid=(KV, nqb),
        in_specs=_attn_in_specs(H, KV, n_ctx, nqb) + [qblk, qblk],
        out_specs=[qblk, ctx, ctx] + [piece] * 6 + [sink],
        out_shape=[jax.ShapeDtypeStruct((rows, H * HEAD_DIM), F32), ctx_shape, ctx_shape] + [kv_shape] * 6 +
                  [jax.ShapeDtypeStruct(sink_col.shape, F32)],
        compiler_params=_cparams(("parallel", "arbitrary")),
    )(*([qkv] * 9), sink_col, _attn_bias(n_ctx), o, do)


def _shift_blocks(x, n_ctx, delta):
    lat = x[n_ctx:]
    z = jnp.zeros((WINDOW_BLOCK, x.shape[1]), x.dtype)
    if delta == 1:
        lat = jnp.concatenate([z, lat[:-WINDOW_BLOCK]], axis=0)
    elif delta == -1:
        lat = jnp.concatenate([lat[WINDOW_BLOCK:], z], axis=0)
    return jnp.concatenate([jnp.zeros((n_ctx, x.shape[1]), x.dtype), lat], axis=0)


def attention(qkv, sink_col, H, KV, n_ctx, name):
    @jax.custom_vjp
    def op(qkv, sink_col):
        return _attn_fwd(qkv, sink_col, H, KV, n_ctx, name + "_fwd")

    def fwd(qkv, sink_col):
        o = _attn_fwd(qkv, sink_col, H, KV, n_ctx, name + "_fwd")
        return o, (qkv, sink_col, o)

    def bwd(res, do):
        qkv, sink_col, o = res
        dq, dkc, dvc, dkp, dkcur, dkn, dvp, dvcur, dvn, dsink = _attn_bwd(qkv, sink_col, o, do, H, KV, n_ctx,
                                                                          name + "_bwd")

        def gather_pieces(prev, cur, nxt, ctx):
            pad = jnp.concatenate([ctx, jnp.zeros((qkv.shape[0] - n_ctx, ctx.shape[1]), F32)], axis=0)
            return rowwise(name + "_kvsum", lambda a, b, c, d: (a + b + c + d,),
                           [cur, _shift_blocks(prev, n_ctx, -1), _shift_blocks(nxt, n_ctx, 1), pad], [], [], 1, 256)[0]

        dk = gather_pieces(dkp, dkcur, dkn, dkc)
        dv = gather_pieces(dvp, dvcur, dvn, dvc)
        return jnp.concatenate([dq, dk, dv], axis=1).astype(qkv.dtype), dsink

    op.defvjp(fwd, bwd)
    return op(qkv, sink_col)


ROW_BLOCK = 256


def _rms_norm(x, g):
    return (x * lax.rsqrt(jnp.mean(x * x, axis=-1, keepdims=True) + NORM_EPS)) * g


def _modulate_f(x, g, shift, scale):
    return ((_rms_norm(x, g) * (1.0 + scale) + shift).astype(BF16),)


def _expm1(x):
    series = x * (1 + x / 2 * (1 + x / 3 * (1 + x / 4 * (1 + x / 5 * (1 + x / 6)))))
    return jnp.where(jnp.abs(x) < 0.1, series, jnp.exp(x) - 1.0)


def _lru_gates_f(ra, rx, uc, ba, bx, sp):
    r = jax.nn.sigmoid(ra + ba)
    ig = jax.nn.sigmoid(rx + bx)
    log_a = -LRU_C * r * sp
    return jnp.exp(log_a), jnp.sqrt(-_expm1(2.0 * log_a)) * (ig * uc)


def _ln_silu_f(z, g, b):
    mu = jnp.mean(z, axis=-1, keepdims=True)
    var = jnp.mean(jnp.square(z - mu), axis=-1, keepdims=True)
    return (jax.nn.silu((z - mu) * lax.rsqrt(var + NORM_EPS) * g + b).astype(BF16),)


def _loss_f(x, target, g):
    err = _rms_norm(x, g) - target
    return (jnp.sum(0.5 * err * err, axis=0, keepdims=True) / x.shape[1],)


def _modulate(x, g, shift, scale):
    return rowwise("modulate", _modulate_f, [x], [g, shift, scale], ["vec", "seg", "seg"], 1, ROW_BLOCK)[0]


def _residual_modulate(x, y, bias, gate, g, shift, scale):
    def f(x, y, *p):
        x_new = x + p[-4] * (y if bias is None else y + p[0])
        return x_new, _modulate_f(x_new, *p[-3:])[0]

    params = ([] if bias is None else [bias]) + [gate, g, shift, scale]
    kinds = ([] if bias is None else ["vec"]) + ["seg", "vec", "seg", "seg"]
    return rowwise("residual_modulate", f, [x, y], params, kinds, 2, ROW_BLOCK)


def _residual(x, y, gate):
    return rowwise("residual", lambda x, y, g: (x + g * y,), [x, y], [gate], ["seg"], 1, ROW_BLOCK)[0]


def _split_nn(a, b, g0, name):
    M, K = a.shape
    n = b.shape[2]
    tm = _div_tile(M, 768, SUBLANE)
    tn = _div_tile(n, 1408, LANE)
    nn = n // tn

    def body(a_ref, b_ref, o_ref):
        o_ref[...] = _dot(a_ref[...], b_ref[...], ((1,), (0,))).astype(BF16)

    return pl.pallas_call(
        body, name=name, grid=(M // tm, 2, nn),
        in_specs=[pl.BlockSpec((tm, K), lambda i, g, j: (i, 0)),
                  pl.BlockSpec((None, K, tn), lambda i, g, j: (g + g0, 0, j))],
        out_specs=pl.BlockSpec((tm, tn), lambda i, g, j: (i, g * nn + j)),
        out_shape=jax.ShapeDtypeStruct((M, 2 * n), BF16),
        compiler_params=_cparams(("parallel", "parallel", "parallel")),
    )(a, b)


def _split_nt(dy_a, dy_b, b, out_dtype, name):
    M = dy_a.shape[0]
    _, K, n = b.shape
    tm = _div_tile(M, 768, SUBLANE)
    tko = _div_tile(K, 1408, LANE)
    tr = _div_tile(n, 2816, LANE)
    nr = n // tr
    half = 2 * nr

    def body(dya_ref, dyb_ref, b_ref, o_ref, acc_ref):
        r = pl.program_id(2)

        @pl.when(r == 0)
        def _():
            acc_ref[...] = _dot(dya_ref[...], b_ref[...], ((1,), (1,)))

        @pl.when((r > 0) & (r < half))
        def _():
            acc_ref[...] += _dot(dya_ref[...], b_ref[...], ((1,), (1,)))

        @pl.when(r >= half)
        def _():
            acc_ref[...] += _dot(dyb_ref[...], b_ref[...], ((1,), (1,)))

        @pl.when(r == 2 * half - 1)
        def _():
            o_ref[...] = acc_ref[...].astype(out_dtype)

    return pl.pallas_call(
        body, name=name, grid=(M // tm, K // tko, 2 * half),
        in_specs=[pl.BlockSpec((tm, tr), lambda i, kk, r: (i, jnp.minimum(r, half - 1))),
                  pl.BlockSpec((tm, tr), lambda i, kk, r: (i, jnp.maximum(r - half, 0))),
                  pl.BlockSpec((None, tko, tr), lambda i, kk, r: (r // nr, kk, lax.rem(r, nr)))],
        out_specs=pl.BlockSpec((tm, tko), lambda i, kk, r: (i, kk)),
        out_shape=jax.ShapeDtypeStruct((M, K), out_dtype),
        scratch_shapes=[pltpu.VMEM((tm, tko), F32)],
        compiler_params=_cparams(("parallel", "parallel", "arbitrary")),
    )(dy_a, dy_b, b)


def _split_tn(a, dy_a, dy_b, out_dtype, name):
    M, K = a.shape
    n = dy_a.shape[1] // 2
    tm = _div_tile(M, 1408, SUBLANE)
    tk = _div_tile(K, 1408, LANE)
    tn = _div_tile(n, 1408, LANE)
    nkb, nn, nm = K // tk, n // tn, M // tm

    def body(a_ref, dya_ref, dyb_ref, o_ref, acc_ref):
        g, r = pl.program_id(0), pl.program_id(3)

        @pl.when(r == 0)
        def _():
            acc_ref[...] = jnp.zeros_like(acc_ref)

        @pl.when(g < 2)
        def _():
            acc_ref[...] += _dot(a_ref[...], dya_ref[...], ((0,), (0,)))

        @pl.when(g >= 2)
        def _():
            acc_ref[...] += _dot(a_ref[...], dyb_ref[...], ((0,), (0,)))

        @pl.when(r == nm - 1)
        def _():
            o_ref[...] = acc_ref[...].astype(out_dtype)

    return pl.pallas_call(
        body, name=name, grid=(4, nkb, nn, nm),
        in_specs=[pl.BlockSpec((tm, tk), lambda g, kk, j, r: (r, kk)),
                  pl.BlockSpec((tm, tn), lambda g, kk, j, r: (jnp.where(g < 2, r, 0), jnp.minimum(g, 1) * nn + j)),
                  pl.BlockSpec((tm, tn), lambda g, kk, j, r: (jnp.where(g >= 2, r, 0),
                                                              jnp.maximum(g - 2, 0) * nn + j))],
        out_specs=pl.BlockSpec((None, tk, tn), lambda g, kk, j, r: (g, kk, j)),
        out_shape=jax.ShapeDtypeStruct((4, K, n), out_dtype),
        scratch_shapes=[pltpu.VMEM((tk, tn), F32)],
        compiler_params=_cparams(("parallel", "parallel", "parallel", "arbitrary")),
    )(a, dy_a, dy_b)


def _split_linear(h, w, px, name):
    @jax.custom_vjp
    def op(h, w, px):
        return _split_nn(h, w, 0, name + "_a_fwd"), _split_nn(h, w, 2, name + "_b_fwd")

    def fwd(h, w, px):
        return (_split_nn(h, w, 0, name + "_a_fwd"), _split_nn(h, w, 2, name + "_b_fwd")), (h, w)

    def bwd(res, cts):
        h, w = res
        dh = _split_nt(cts[0], cts[1], w, h.dtype, name + "_dx")
        dw = _split_tn(h, cts[0], cts[1], BF16, name + "_dw")
        return dh, None, dw

    op.defvjp(fwd, bwd)
    return op(h, w, px)


def _attention_mixer(h, P, PX, j, n_ctx, tabs):
    H = P["attn_w_o"][j].shape[1] // HEAD_DIM
    KV = H // GQA_GROUP
    qkv = linear(h, P["attn_w_qkv"][j], PX["attn_w_qkv"][j], grad_dtype=BF16, name="attn_qkv")
    qkv = rope(qkv, tabs[0], tabs[1], H + KV, "rope")
    sink_col = jnp.repeat(P["attn_sink"][j].reshape(KV, GQA_GROUP), WINDOW_BLOCK, axis=1)[..., None]
    o = attention(qkv, sink_col, H, KV, n_ctx, "attn")
    return linear(o, P["attn_w_o"][j], PX["attn_w_o"][j], grad_dtype=BF16, name="attn_o"), None


def _rglru_mixer(h, P, PX, j, n_ctx):
    gate, xb = _split_linear(h, P["lru_w_in"][j], PX["lru_w_in"][j], "lru_in")
    R = xb.shape[1]
    cb = _div_tile(R, 512, LANE)
    sp = jax.nn.softplus(-P["lru_lambda"][j])
    hs = []
    for d in range(2):
        K = P["lru_conv_w"][j].shape[1]
        uc = dwconv(xb, P["lru_conv_w"][j][d], P["lru_conv_b"][j][d][None], 0 if d == 1 else K - 1, n_ctx,
                    "lru_conv")
        ra = blockdiag_linear(uc, P["lru_wa"][j][d], "lru_wa")
        rx = blockdiag_linear(uc, P["lru_wx"][j][d], "lru_wx")
        a, bt = rowwise("lru_gates", _lru_gates_f, [ra, rx, uc],
                        [P["lru_ba"][j][d][None], P["lru_bx"][j][d][None], sp[d][None]], ["vec"] * 3, 2,
                        ROW_BLOCK, cb=cb)
        hs.append(linear_scan(a, bt, d, d == 1, "lru_scan"))
    y_in = rowwise("lru_gelu", lambda g, h0, h1: ((jax.nn.gelu(g) * (h0 + h1)).astype(BF16),), [gate, hs[0], hs[1]], [], [], 1,
                   ROW_BLOCK, cb=cb)[0]
    return linear(y_in, P["lru_w_out"][j], PX["lru_w_out"][j], grad_dtype=BF16, name="lru_out"), None


def _conformer_mixer(h, P, PX, j, n_ctx):
    z1, z2 = _split_linear(h, P["conf_w_in"][j], PX["conf_w_in"][j], "conf_in")
    Dm = z1.shape[1]
    b_in = P["conf_b_in"][j]
    z = rowwise("conf_glu", lambda a, b, ba, bb: ((a + ba) * jax.nn.sigmoid(b + bb),), [z1, z2],
                [b_in[None, :Dm], b_in[None, Dm:]], ["vec", "vec"], 1, ROW_BLOCK, cb=_div_tile(Dm, 512, LANE))[0]
    K = P["conf_dw_w"][j].shape[0]
    zc = dwconv(z, P["conf_dw_w"][j], P["conf_dw_b"][j][None], K // 2, n_ctx, "conf_conv")
    zs = rowwise("conf_ln_silu", _ln_silu_f, [zc], [P["conf_ln_g"][j][None], P["conf_ln_b"][j][None]],
                 ["vec", "vec"], 1, ROW_BLOCK)[0]
    y = linear(zs, P["conf_w_out"][j], PX["conf_w_out"][j], grad_dtype=BF16, name="conf_out")
    return y, P["conf_b_out"][j][None]


def _conv_ffn(u, P, PX, i, n_ctx):
    g, v = _split_linear(u, P["ffn_w_up"][i], PX["ffn_w_up"][i], "ffn_up")
    a = ffn_glu(g, v, P["ffn_conv_w"][i], P["ffn_conv_b"][i][None], n_ctx, "ffn_glu")
    return linear(a, P["ffn_w_down"][i], PX["ffn_w_down"][i], grad_dtype=BF16, name="ffn_down")


def local_loss(x_all, mods, P, PX, target, n_ctx):
    assert n_ctx == ROW_BLOCK
    depth = len(mods)
    tabs = rope_tables(n_ctx, x_all.shape[0] - n_ctx)
    x = x_all
    h = _modulate(x, P["norm_mix_g"][0][None], mods[0][0], mods[0][1])
    for i in range(depth):
        kind, j = i % 3, i // 3
        _, _, g1, sh2, sc2, g2 = mods[i]
        if kind == 0:
            y, bias = _attention_mixer(h, P, PX, j, n_ctx, tabs)
        elif kind == 1:
            y, bias = _rglru_mixer(h, P, PX, j, n_ctx)
        else:
            y, bias = _conformer_mixer(h, P, PX, j, n_ctx)
        x, u = _residual_modulate(x, y, bias, g1, P["norm_ffn_g"][i][None], sh2, sc2)
        f = _conv_ffn(u, P, PX, i, n_ctx)
        if i + 1 < depth:
            x, h = _residual_modulate(x, f, None, g2, P["norm_mix_g"][i + 1][None], mods[i + 1][0], mods[i + 1][1])
        else:
            x = _residual(x, f, g2)
    per_feature = rowwise("loss_head", _loss_f, [x[n_ctx:], target], [P["final_norm_g"][None]], ["vec"], 0,
                          ROW_BLOCK, nograd=(1,))[0]
    return jnp.sum(per_feature)


def _plane_peers():
    x, y, c = lax.axis_index("x"), lax.axis_index("y"), lax.axis_index("c")
    me = 2 * x + y
    peers = [((1 - x, y, c), 2 * (1 - x) + y),
             ((x, 1 - y, c), 2 * x + (1 - y)),
             ((1 - x, 1 - y, c), 2 * (1 - x) + (1 - y))]
    return me, peers


def plane_allgather(arrays, layers, name):
    flat = []
    for k, L in enumerate(layers):
        flat += [(k, None)] if L is None else [(k, l) for l in range(L)]
    n_in, n = len(arrays), len(flat)

    def body(*refs):
        ins, outs = refs[:n_in], refs[n_in:n_in + n]
        lsem, ssem, rsem = refs[n_in + n:]
        me, peers = _plane_peers()

        def src(t):
            k, l = flat[t]
            return ins[k] if l is None else ins[k].at[l]

        def remote(t, p, slot):
            return pltpu.make_async_remote_copy(src(t), outs[t].at[slot], ssem.at[3 * t + p], rsem.at[3 * t + p],
                                                device_id=peers[p][0], device_id_type=MESH)

        local = [pltpu.make_async_copy(src(t), outs[t].at[me], lsem.at[t]) for t in range(n)]
        for t in range(n):
            local[t].start()
            for p in range(3):
                remote(t, p, me).start()
        for t in range(n):
            local[t].wait()
            for p in range(3):
                remote(t, p, peers[p][1]).wait()

    out_shape = []
    for k, l in flat:
        shp = arrays[k].shape if l is None else arrays[k].shape[1:]
        out_shape.append(jax.ShapeDtypeStruct((4,) + tuple(shp), arrays[k].dtype))
    res = pl.pallas_call(
        body, name=name, in_specs=[ANY] * n_in, out_specs=[ANY] * n, out_shape=out_shape,
        scratch_shapes=[pltpu.SemaphoreType.DMA((n,)), pltpu.SemaphoreType.DMA((3 * n,)),
                        pltpu.SemaphoreType.DMA((3 * n,))],
    )(*arrays)
    out, t = [], 0
    for L in layers:
        if L is None:
            out.append(res[t])
            t += 1
        else:
            out.append(list(res[t:t + L]))
            t += L
    return out


def plane_alltoall(groups, name):
    flat = [(k, l) for k, grp in enumerate(groups) for l in range(len(grp))]
    arrays = [a for grp in groups for a in grp]
    n, ng = len(flat), len(groups)

    def body(*refs):
        ins, outs = refs[:n], refs[n:n + ng]
        lsem, ssem, rsem = refs[n + ng:]
        me, peers = _plane_peers()

        def remote(t, p, src_slot, dst_slot):
            k, l = flat[t]
            return pltpu.make_async_remote_copy(ins[t].at[src_slot], outs[k].at[dst_slot, l], ssem.at[3 * t + p],
                                                rsem.at[3 * t + p], device_id=peers[p][0], device_id_type=MESH)

        local = [pltpu.make_async_copy(ins[t].at[me], outs[flat[t][0]].at[me, flat[t][1]], lsem.at[t])
                 for t in range(n)]
        for t in range(n):
            local[t].start()
            for p in range(3):
                remote(t, p, peers[p][1], me).start()
        for t in range(n):
            local[t].wait()
            for p in range(3):
                remote(t, p, peers[p][1], peers[p][1]).wait()

    out_shape = [jax.ShapeDtypeStruct((4, len(grp)) + tuple(grp[0].shape[1:]), grp[0].dtype) for grp in groups]
    return pl.pallas_call(
        body, name=name, in_specs=[ANY] * n, out_specs=[ANY] * ng, out_shape=out_shape,
        scratch_shapes=[pltpu.SemaphoreType.DMA((n,)), pltpu.SemaphoreType.DMA((3 * n,)),
                        pltpu.SemaphoreType.DMA((3 * n,))],
    )(*arrays)


def sibling_exchange(arrays, name):
    n = len(arrays)

    def body(*refs):
        ins, outs = refs[:n], refs[n:2 * n]
        ssem, rsem = refs[2 * n:]
        sibling = (lax.axis_index("x"), lax.axis_index("y"), 1 - lax.axis_index("c"))
        copies = [pltpu.make_async_remote_copy(ins[t], outs[t], ssem.at[t], rsem.at[t], device_id=sibling,
                                               device_id_type=MESH) for t in range(n)]
        for cp in copies:
            cp.start()
        for cp in copies:
            cp.wait()

    return pl.pallas_call(
        body, name=name, in_specs=[ANY] * n, out_specs=[ANY] * n,
        out_shape=[jax.ShapeDtypeStruct(a.shape, a.dtype) for a in arrays],
        scratch_shapes=[pltpu.SemaphoreType.DMA((n,)), pltpu.SemaphoreType.DMA((n,))],
    )(*arrays)


def _sibling():
    return (lax.axis_index("x"), lax.axis_index("y"), 1 - lax.axis_index("c"))


def _rows_half(ref, h, rows, axis):
    idx = (slice(None),) * axis + (pl.ds(h * (rows // 2), rows // 2),)
    return ref.at[idx]


def plane_allgather_shared(arrays, layers, name):
    flat = []
    for k, L in enumerate(layers):
        flat += [(k, None)] if L is None else [(k, l) for l in range(L)]
    n_in, n = len(arrays), len(flat)
    shard_rows = [arrays[k].shape[0] if l is None else arrays[k].shape[1] for k, l in flat]
    assert all(r % 32 == 0 for r in shard_rows)

    def body(*refs):
        ins, outs = refs[:n_in], refs[n_in:n_in + n]
        isend, irecv, dsend, drecv = refs[n_in + n:]
        me, peers = _plane_peers()
        c = lax.axis_index("c")

        def src(t):
            k, l = flat[t]
            return ins[k] if l is None else ins[k].at[l]

        def over_ici(t, p, slot):
            return pltpu.make_async_remote_copy(
                _rows_half(src(t), c, shard_rows[t], 0), _rows_half(outs[t].at[slot], c, shard_rows[t], 0),
                isend.at[3 * t + p], irecv.at[3 * t + p], device_id=peers[p][0], device_id_type=MESH)

        def over_d2d(t, p, h):
            piece = _rows_half(outs[t].at[peers[p][1]], h, shard_rows[t], 0)
            return pltpu.make_async_remote_copy(piece, piece, dsend.at[3 * t + p], drecv.at[3 * t + p],
                                                device_id=_sibling(), device_id_type=MESH)

        for t in range(n):
            for p in range(3):
                over_ici(t, p, me).start()
        for t in range(n):
            for p in range(3):
                over_ici(t, p, peers[p][1]).wait_recv()
                over_d2d(t, p, c).start()
        for t in range(n):
            for p in range(3):
                over_d2d(t, p, 1 - c).wait_recv()
                over_ici(t, p, me).wait_send()
                over_d2d(t, p, c).wait_send()

    out_shape = []
    for k, l in flat:
        shp = arrays[k].shape if l is None else arrays[k].shape[1:]
        out_shape.append(jax.ShapeDtypeStruct((4,) + tuple(shp), arrays[k].dtype))
    res = pl.pallas_call(
        body, name=name, in_specs=[ANY] * n_in, out_specs=[ANY] * n, out_shape=out_shape,
        scratch_shapes=[pltpu.SemaphoreType.DMA((3 * n,))] * 4,
    )(*arrays)
    me = 2 * lax.axis_index("x") + lax.axis_index("y")
    res = [lax.dynamic_update_index_in_dim(r, arrays[k] if l is None else arrays[k][l], me, 0)
           for r, (k, l) in zip(res, flat)]
    out, t = [], 0
    for L in layers:
        if L is None:
            out.append(res[t])
            t += 1
        else:
            out.append(list(res[t:t + L]))
            t += L
    return out


def add_pair(a, b, out_dtype, name):
    shp = a.shape
    R, C = _shape2d(shp)
    tr = _row_tile(R, C, 1 << 20)

    def body(a_ref, b_ref, o_ref):
        o_ref[...] = (a_ref[...].astype(F32) + b_ref[...].astype(F32)).astype(out_dtype)

    spec = pl.BlockSpec((tr, C), lambda i: (i, 0))
    return pl.pallas_call(
        body, name=name, grid=(R // tr,), in_specs=[spec, spec], out_specs=spec,
        out_shape=jax.ShapeDtypeStruct((R, C), out_dtype), compiler_params=_cparams(("parallel",)),
    )(a.reshape(R, C), b.reshape(R, C)).reshape(shp)


def gather_all_devices(a, name):
    own = plane_allgather([a], [None], name + "_plane")[0]
    sib = sibling_exchange([own], name + "_sibling")[0]
    c = lax.axis_index("c")
    c0 = jnp.where(c == 0, own, sib)
    c1 = jnp.where(c == 0, sib, own)
    return jnp.stack([c0, c1], axis=1).reshape((8,) + a.shape)


def _shape2d(shape):
    if len(shape) >= 2 and shape[-1] % LANE == 0:
        return (math.prod(shape[:-1]), shape[-1])
    return tuple(shape) if len(shape) == 2 else (1, math.prod(shape))


def _as2d(a):
    return a.reshape(_shape2d(a.shape))


def _row_tile(R, C, budget_bytes):
    if R * C * 4 <= budget_bytes or R % SUBLANE:
        return R
    cap = max(SUBLANE, (budget_bytes // (C * 4)) // SUBLANE * SUBLANE)
    return _div_tile(R, cap, SUBLANE)


def sum_slots(x4, name):
    shp = x4.shape[1:]
    R, C = _shape2d(shp)
    v = x4.reshape(4, R, C)
    tr = _row_tile(R, C, 1 << 20)

    def body(x_ref, o_ref):
        f = lambda s: x_ref[s].astype(F32)
        o_ref[...] = ((f(0) + f(1)) + f(2)) + f(3)

    out = pl.pallas_call(
        body, name=name, grid=(R // tr,),
        in_specs=[pl.BlockSpec((4, tr, C), lambda i: (0, i, 0))],
        out_specs=pl.BlockSpec((tr, C), lambda i: (i, 0)),
        out_shape=jax.ShapeDtypeStruct((R, C), F32),
        compiler_params=_cparams(("parallel",)),
    )(v)
    return out.reshape(shp)


def adamw(w, m, v, terms, name):
    shp = w.shape
    w2, m2, v2 = _as2d(w), _as2d(m), _as2d(v)
    flat = [_as2d(t) for inner in terms for t in inner]
    sizes = [len(inner) for inner in terms]
    R, C = w2.shape
    tr = _row_tile(R, C, 1 << 20)

    def body(*refs):
        w_ref, m_ref, v_ref = refs[:3]
        t_refs = refs[3:3 + len(flat)]
        g_ref, d_ref, nm_ref, nv_ref = refs[3 + len(flat):]
        g, t = None, 0
        for sz in sizes:
            inner = t_refs[t][...].astype(F32)
            for q in range(1, sz):
                inner = inner + t_refs[t + q][...].astype(F32)
            t += sz
            g = inner if g is None else g + inner
        nm = ADAM_B1 * m_ref[...] + (1.0 - ADAM_B1) * g
        nv = ADAM_B2 * v_ref[...] + (1.0 - ADAM_B2) * jnp.square(g)
        m_hat = nm / (1.0 - ADAM_B1 ** ADAM_STEP)
        v_hat = nv / (1.0 - ADAM_B2 ** ADAM_STEP)
        g_ref[...] = g
        d_ref[...] = -ADAM_LR * (m_hat / (jnp.sqrt(v_hat) + ADAM_EPS) + ADAM_WD * w_ref[...])
        nm_ref[...] = nm
        nv_ref[...] = nv

    spec = pl.BlockSpec((tr, C), lambda i: (i, 0))
    outs = pl.pallas_call(
        body, name=name, grid=(R // tr,),
        in_specs=[spec] * (3 + len(flat)), out_specs=[spec] * 4,
        out_shape=[jax.ShapeDtypeStruct((R, C), F32)] * 4,
        compiler_params=_cparams(("parallel",)),
    )(w2, m2, v2, *flat)
    return tuple(o.reshape(shp) for o in outs)


def _pack(arrs):
    flat = jnp.concatenate([a.reshape(-1).astype(F32) for a in arrs])
    unit = 32 * LANE
    pad = (-flat.shape[0]) % unit
    return jnp.pad(flat, (0, pad)).reshape(-1, LANE)


def _unpack(pack, shapes, lead=()):
    flat = pack.reshape(tuple(lead) + (-1,))
    out, off = [], 0
    for s in shapes:
        n = math.prod(s)
        out.append(flat[..., off:off + n].reshape(tuple(lead) + tuple(s)))
        off += n
    return out


COL_SHARDED = ("attn_w_qkv", "lru_w_in", "conf_w_in", "ffn_w_up")
ROW_SHARDED = ("attn_w_o", "lru_w_out", "conf_w_out", "ffn_w_down")
SMALL_SHARDED = ("lru_conv_w", "lru_conv_b", "lru_ba", "lru_bx", "lru_lambda", "conf_b_in", "conf_dw_w", "conf_dw_b",
                 "conf_ln_g", "conf_ln_b", "conf_b_out", "ffn_conv_w")
REPLICATED = ("norm_mix_g", "norm_ffn_g", "attn_sink", "lru_wa", "lru_wx", "ffn_conv_b", "final_norm_g")
LOCAL_ONLY = ("c_ctx", "ada_b")
WEIGHTS = ("c_ctx", "ada_w", "ada_b", "norm_mix_g", "norm_ffn_g", "attn_w_qkv", "attn_w_o", "attn_sink", "lru_w_in",
           "lru_conv_w", "lru_conv_b", "lru_wa", "lru_ba", "lru_wx", "lru_bx", "lru_lambda", "lru_w_out", "conf_w_in",
           "conf_b_in", "conf_dw_w", "conf_dw_b", "conf_ln_g", "conf_ln_b", "conf_w_out", "conf_b_out", "ffn_w_up",
           "ffn_conv_w", "ffn_conv_b", "ffn_w_down", "final_norm_g")


def _full_last_axis(g4):
    moved = jnp.moveaxis(g4, 0, -2)
    return moved.reshape(moved.shape[:-2] + (-1,))


def _shards_last_axis(full):
    split = full.reshape(full.shape[:-1] + (4, full.shape[-1] // 4))
    return jnp.moveaxis(split, -2, 0)


def _train_step(W, M, V, x, c, ctx, loss_target):
    n_ctx = ctx.shape[1]
    depth = W["ada_w"].shape[0]
    D = x.shape[-1]
    my_c = lax.axis_index("c")
    my_s = 2 * lax.axis_index("x") + lax.axis_index("y")
    my_dev = 2 * my_s + my_c

    big = COL_SHARDED + ROW_SHARDED
    small_shapes = [W[k].shape for k in SMALL_SHARDED]
    small_pack = _pack([W[k] for k in SMALL_SHARDED])
    gathered = plane_allgather_shared([W[k].astype(BF16) for k in big] + [small_pack],
                                      [W[k].shape[0] for k in big] + [None], "gather_weights")
    P, PX = {}, {}
    for k, per_layer in zip(big, gathered[:len(big)]):
        if k in ROW_SHARDED:
            per_layer = [g.reshape(1, -1, g.shape[-1]) for g in per_layer]
        P[k] = per_layer
        PX[k] = [jnp.zeros(g.shape, BF16) for g in per_layer]
    for k, g4 in zip(SMALL_SHARDED, _unpack(gathered[-1], small_shapes, lead=(4,))):
        P[k] = _full_last_axis(g4)
    for k in REPLICATED:
        P[k] = W[k]

    c8 = gather_all_devices(c, "gather_c").reshape(8, D)
    cond16 = jnp.concatenate([c8, jnp.broadcast_to(W["c_ctx"][None], (8, D))], axis=0)
    act16, act_vjp = jax.vjp(jax.nn.silu, cond16)
    n_ada = W["ada_w"].shape[-1]
    m_shard = matmul_nn(act16, W["ada_w"], name="ada_fwd")
    m4 = plane_allgather([m_shard], [None], "gather_ada")[0].reshape(4, 16, depth, n_ada)
    m_full = m4.transpose(2, 1, 0, 3).reshape(depth, 16, 4 * n_ada) + W["ada_b"][:, None, :]
    m_lat = lax.dynamic_index_in_dim(m_full, my_dev, axis=1, keepdims=False)
    m_ctx = m_full[:, 8]
    mods = [tuple(jnp.stack([a, b])[:, None, :] for a, b in zip(jnp.split(m_ctx[i], 6), jnp.split(m_lat[i], 6)))
            for i in range(depth)]

    x_all = jnp.concatenate([ctx[0], x[0]], axis=0)
    loss, (gx, gmods, gP, gPX) = jax.value_and_grad(
        lambda xa, md, p, px: local_loss(xa, md, p, px, loss_target[0], n_ctx), argnums=(0, 1, 2, 3))(x_all, mods, P, PX)
    loss = lax.psum(loss, ("x", "y", "c"))
    grad_x = gx[n_ctx:][None]

    dm_mine = jnp.stack([jnp.stack([jnp.concatenate([g[r, 0] for g in gmods[i]]) for i in range(depth)])
                         for r in range(2)])
    dm8 = gather_all_devices(dm_mine, "gather_dmod")
    dm16 = jnp.concatenate([dm8[:, 1], dm8[:, 0]], axis=0).transpose(1, 0, 2)
    grad_ada_b = rowwise("ada_b_grad", lambda a: (jnp.sum(a, axis=0, keepdims=True),),
                         [dm16.transpose(1, 0, 2).reshape(16, -1)], [], [], 0, 16)[0].reshape(depth, -1)
    dm_cols = lax.dynamic_slice_in_dim(dm16, my_s * n_ada, n_ada, axis=2)
    dm_cols = dm_cols.transpose(1, 0, 2).reshape(16, depth * n_ada)
    grad_ada_w = matmul_tn(act16, dm_cols, depth, name="ada_dw")
    dact_part = matmul_nt(dm_cols, W["ada_w"], name="ada_dx")
    dact4 = plane_allgather([dact_part], [None], "gather_dact")[0]
    dact = ((dact4[0] + dact4[1]) + dact4[2]) + dact4[3]
    grad_c_ctx = jnp.sum(act_vjp(dact)[0][8:], axis=0)

    out = {}
    out["ada_w"] = adamw(W["ada_w"], M["ada_w"], V["ada_w"], [[grad_ada_w]], "adamw_ada_w")
    local_shapes = [W[k].shape for k in LOCAL_ONLY]
    res = adamw(_pack([W[k] for k in LOCAL_ONLY]), _pack([M[k] for k in LOCAL_ONLY]), _pack([V[k] for k in LOCAL_ONLY]),
                [[_pack([grad_c_ctx, grad_ada_b])]], "adamw_local")
    for k, vals in zip(LOCAL_ONLY, zip(*[_unpack(r, local_shapes) for r in res])):
        out[k] = vals

    groups = []
    for k in big:
        pieces = gPX[k]
        if k in ROW_SHARDED:
            pieces = [g.reshape((4, -1, g.shape[-1])) for g in pieces]
        groups.append(pieces)
    small_grads = _pack_shards([_shards_last_axis(gP[k]) for k in SMALL_SHARDED])
    groups.append([small_grads])
    def rows_half(a, h):
        return lax.dynamic_slice_in_dim(a, h * (a.shape[1] // 2), a.shape[1] // 2, axis=1)

    pieces = [a for grp in groups for a in grp]
    theirs = sibling_exchange([rows_half(a, 1 - my_c) for a in pieces], "swap_grad_halves")
    chip_sums = [add_pair(rows_half(a, my_c), t, a.dtype, "chip_sum") for a, t in zip(pieces, theirs)]
    chip_groups, t = [], 0
    for grp in groups:
        chip_groups.append(chip_sums[t:t + len(grp)])
        t += len(grp)
    received = plane_alltoall(chip_groups, "scatter_grads")
    half_sums = [sum_slots(r, "plane_sum") for r in received]
    other = sibling_exchange(half_sums, "merge_grad_halves")
    full = [jnp.concatenate([jnp.where(my_c == 0, mine, got), jnp.where(my_c == 0, got, mine)], axis=1)
            for mine, got in zip(half_sums, other)]
    for k, g in zip(big, full[:len(big)]):
        out[k] = adamw(W[k], M[k], V[k], [[g]], "adamw_" + k)
    res = adamw(small_pack, _pack([M[k] for k in SMALL_SHARDED]), _pack([V[k] for k in SMALL_SHARDED]),
                [[full[-1][0]]], "adamw_small")
    for k, vals in zip(SMALL_SHARDED, zip(*[_unpack(r, small_shapes) for r in res])):
        out[k] = vals

    rep_shapes = [W[k].shape for k in REPLICATED]
    own = plane_allgather([_pack([gP[k] for k in REPLICATED])], [None], "gather_rep_grads")[0]
    sib = sibling_exchange([own], "sibling_rep_grads")[0]
    res = adamw(_pack([W[k] for k in REPLICATED]), _pack([M[k] for k in REPLICATED]), _pack([V[k] for k in REPLICATED]),
                [[own[s], sib[s]] for s in range(4)], "adamw_rep")
    for k, vals in zip(REPLICATED, zip(*[_unpack(r, rep_shapes) for r in res])):
        out[k] = vals

    return (loss, grad_x) + tuple(out[k][j] for j in range(4) for k in WEIGHTS)


def _pack_shards(arrs4):
    return jnp.stack([_pack([a[s] for a in arrs4]) for s in range(4)])


def kernel(x, c, ctx, c_ctx, ada_w, ada_b, norm_mix_g, norm_ffn_g, attn_w_qkv, attn_w_o, attn_sink, lru_w_in, lru_conv_w, lru_conv_b, lru_wa, lru_ba, lru_wx, lru_bx, lru_lambda, lru_w_out, conf_w_in, conf_b_in, conf_dw_w, conf_dw_b, conf_ln_g, conf_ln_b, conf_w_out, conf_b_out, ffn_w_up, ffn_conv_w, ffn_conv_b, ffn_w_down, final_norm_g, loss_target, m_c_ctx, m_ada_w, m_ada_b, m_norm_mix_g, m_norm_ffn_g, m_attn_w_qkv, m_attn_w_o, m_attn_sink, m_lru_w_in, m_lru_conv_w, m_lru_conv_b, m_lru_wa, m_lru_ba, m_lru_wx, m_lru_bx, m_lru_lambda, m_lru_w_out, m_conf_w_in, m_conf_b_in, m_conf_dw_w, m_conf_dw_b, m_conf_ln_g, m_conf_ln_b, m_conf_w_out, m_conf_b_out, m_ffn_w_up, m_ffn_conv_w, m_ffn_conv_b, m_ffn_w_down, m_final_norm_g, v_c_ctx, v_ada_w, v_ada_b, v_norm_mix_g, v_norm_ffn_g, v_attn_w_qkv, v_attn_w_o, v_attn_sink, v_lru_w_in, v_lru_conv_w, v_lru_conv_b, v_lru_wa, v_lru_ba, v_lru_wx, v_lru_bx, v_lru_lambda, v_lru_w_out, v_conf_w_in, v_conf_b_in, v_conf_dw_w, v_conf_dw_b, v_conf_ln_g, v_conf_ln_b, v_conf_w_out, v_conf_b_out, v_ffn_w_up, v_ffn_conv_w, v_ffn_conv_b, v_ffn_w_down, v_final_norm_g):
    given = dict(locals())
    W = {k: given[k] for k in WEIGHTS}
    M = {k: given["m_" + k] for k in WEIGHTS}
    V = {k: given["v_" + k] for k in WEIGHTS}
    return _train_step(W, M, V, x, c, ctx, loss_target)
```

```python
import functools
import math

import jax
import jax.numpy as jnp
from jax import lax
from jax.experimental import pallas as pl
from jax.experimental.pallas import tpu as pltpu
from jax.experimental.pallas import tpu_sc as plsc

F32 = jnp.float32
BF16 = jnp.bfloat16

VMEM_LIMIT_BYTES = 56 * 1024 * 1024
LANE = 128
SUBLANE = 8

HEAD_DIM = 128
GQA_GROUP = 4
WINDOW_BLOCK = 128
GRID_W = 64
ROPE_THETA = 10000.0
ROPE_FREQS = HEAD_DIM // 4
LRU_BLOCK = 128
LRU_C = 8.0
NORM_EPS = 1e-6
NEG_INF = -1e30
CONV_HALO = 32

ADAM_LR = 0.001
ADAM_B1 = 0.9
ADAM_B2 = 0.999
ADAM_EPS = 1e-08
ADAM_WD = 0.01
ADAM_STEP = 10

MESH = pl.DeviceIdType.MESH
ANY = pl.BlockSpec(memory_space=pl.ANY)


def _cparams(sem):
    return pltpu.CompilerParams(dimension_semantics=sem, vmem_limit_bytes=VMEM_LIMIT_BYTES)


def _div_tile(n, cap, unit):
    if n <= unit:
        return n
    best = None
    t = unit
    while t <= min(n, cap):
        if n % t == 0:
            best = t
        t += unit
    assert best is not None, (n, cap, unit)
    return best


def _dot(a, b, dims):
    return lax.dot_general(a.astype(BF16), b.astype(BF16), (dims, ((), ())), preferred_element_type=F32)


def matmul_nn(a, b, diag=False, name="mm_nn"):
    M = a.shape[0]
    G, K, n = b.shape
    tm = _div_tile(M, 768, SUBLANE)
    tn = _div_tile(n, 1408, LANE)
    tk = K if K <= 2048 else _div_tile(K, 1408, LANE)
    nk, nn = K // tk, n // tn
    assert a.shape[1] == (G * K if diag else K)

    def body(a_ref, b_ref, o_ref, acc_ref):
        r = pl.program_id(3)
        part = _dot(a_ref[...], b_ref[...], ((1,), (0,)))
        if nk == 1:
            o_ref[...] = part
        else:
            @pl.when(r == 0)
            def _():
                acc_ref[...] = part

            @pl.when(r > 0)
            def _():
                acc_ref[...] += part

            @pl.when(r == nk - 1)
            def _():
                o_ref[...] = acc_ref[...]

    a_map = (lambda i, g, j, r: (i, g * nk + r)) if diag else (lambda i, g, j, r: (i, r))
    return pl.pallas_call(
        body, name=name,
        grid=(M // tm, G, nn, nk),
        in_specs=[pl.BlockSpec((tm, tk), a_map),
                  pl.BlockSpec((None, tk, tn), lambda i, g, j, r: (g, r, j))],
        out_specs=pl.BlockSpec((tm, tn), lambda i, g, j, r: (i, g * nn + j)),
        out_shape=jax.ShapeDtypeStruct((M, G * n), F32),
        scratch_shapes=[pltpu.VMEM((tm, tn) if nk > 1 else (SUBLANE, LANE), F32)],
        compiler_params=_cparams(("parallel", "parallel", "parallel", "arbitrary")),
    )(a, b)


def matmul_nt(dy, b, diag=False, out_dtype=F32, name="mm_nt"):
    M = dy.shape[0]
    G, K, n = b.shape
    assert dy.shape[1] == G * n
    tm = _div_tile(M, 768, SUBLANE)
    tko = _div_tile(K, 1408, LANE)
    tr = _div_tile(n, 2048, LANE)
    nr, nko = n // tr, K // tko
    steps = nr if diag else G * nr

    def body(dy_ref, b_ref, o_ref, acc_ref):
        r = pl.program_id(3)
        part = _dot(dy_ref[...], b_ref[...], ((1,), (1,)))
        if steps == 1:
            o_ref[...] = part.astype(out_dtype)
        else:
            @pl.when(r == 0)
            def _():
                acc_ref[...] = part

            @pl.when(r > 0)
            def _():
                acc_ref[...] += part

            @pl.when(r == steps - 1)
            def _():
                o_ref[...] = acc_ref[...].astype(out_dtype)

    if diag:
        grid = (M // tm, G, nko, nr)
        dy_map = lambda i, g, kk, r: (i, g * nr + r)
        b_map = lambda i, g, kk, r: (g, kk, r)
        o_map = lambda i, g, kk, r: (i, g * nko + kk)
        out_cols = G * K
    else:
        grid = (M // tm, 1, nko, G * nr)
        dy_map = lambda i, g, kk, r: (i, r)
        b_map = lambda i, g, kk, r: (r // nr, kk, r % nr)
        o_map = lambda i, g, kk, r: (i, kk)
        out_cols = K
    return pl.pallas_call(
        body, name=name,
        grid=grid,
        in_specs=[pl.BlockSpec((tm, tr), dy_map), pl.BlockSpec((None, tko, tr), b_map)],
        out_specs=pl.BlockSpec((tm, tko), o_map),
        out_shape=jax.ShapeDtypeStruct((M, out_cols), out_dtype),
        scratch_shapes=[pltpu.VMEM((tm, tko) if steps > 1 else (SUBLANE, LANE), F32)],
        compiler_params=_cparams(("parallel", "parallel", "parallel", "arbitrary")),
    )(dy, b)


def matmul_tn(a, dy, G, diag=False, out_dtype=F32, name="mm_tn"):
    M = a.shape[0]
    n = dy.shape[1] // G
    K = a.shape[1] // G if diag else a.shape[1]
    tm = _div_tile(M, 1408, SUBLANE)
    tk = _div_tile(K, 1408, LANE)
    tn = _div_tile(n, 1408, LANE)
    nkb, nn, nm = K // tk, n // tn, M // tm

    def body(a_ref, dy_ref, o_ref, acc_ref):
        r = pl.program_id(3)
        part = _dot(a_ref[...], dy_ref[...], ((0,), (0,)))
        if nm == 1:
            o_ref[...] = part.astype(out_dtype)
        else:
            @pl.when(r == 0)
            def _():
                acc_ref[...] = part

            @pl.when(r > 0)
            def _():
                acc_ref[...] += part

            @pl.when(r == nm - 1)
            def _():
                o_ref[...] = acc_ref[...].astype(out_dtype)

    a_map = (lambda g, kk, j, r: (r, g * nkb + kk)) if diag else (lambda g, kk, j, r: (r, kk))
    return pl.pallas_call(
        body, name=name,
        grid=(G, nkb, nn, nm),
        in_specs=[pl.BlockSpec((tm, tk), a_map),
                  pl.BlockSpec((tm, tn), lambda g, kk, j, r: (r, g * nn + j))],
        out_specs=pl.BlockSpec((None, tk, tn), lambda g, kk, j, r: (g, kk, j)),
        out_shape=jax.ShapeDtypeStruct((G, K, n), out_dtype),
        scratch_shapes=[pltpu.VMEM((tk, tn) if nm > 1 else (SUBLANE, LANE), F32)],
        compiler_params=_cparams(("parallel", "parallel", "parallel", "arbitrary")),
    )(a, dy)


def blockdiag_nn(a, b, transpose_b, name):
    M = a.shape[0]
    G, K, n = b.shape
    kin, kout = (n, K) if transpose_b else (K, n)
    tm = _div_tile(M, 768, SUBLANE)
    dims = ((1,), (1,)) if transpose_b else ((1,), (0,))

    def body(a_ref, b_ref, o_ref):
        for g in range(G):
            o_ref[:, g * kout:(g + 1) * kout] = _dot(a_ref[:, g * kin:(g + 1) * kin], b_ref[g], dims)

    return pl.pallas_call(
        body, name=name, grid=(M // tm,),
        in_specs=[pl.BlockSpec((tm, G * kin), lambda i: (i, 0)), pl.BlockSpec((G, K, n), lambda i: (0, 0, 0))],
        out_specs=pl.BlockSpec((tm, G * kout), lambda i: (i, 0)),
        out_shape=jax.ShapeDtypeStruct((M, G * kout), F32),
        compiler_params=_cparams(("parallel",)),
    )(a, b)


def blockdiag_tn(a, dy, G, name):
    M = a.shape[0]
    K, n = a.shape[1] // G, dy.shape[1] // G
    tm = _div_tile(M, 768, SUBLANE)

    def body(a_ref, dy_ref, o_ref):
        @pl.when(pl.program_id(0) == 0)
        def _():
            o_ref[...] = jnp.zeros_like(o_ref)

        for g in range(G):
            o_ref[g] += _dot(a_ref[:, g * K:(g + 1) * K], dy_ref[:, g * n:(g + 1) * n], ((0,), (0,)))

    return pl.pallas_call(
        body, name=name, grid=(M // tm,),
        in_specs=[pl.BlockSpec((tm, G * K), lambda i: (i, 0)), pl.BlockSpec((tm, G * n), lambda i: (i, 0))],
        out_specs=pl.BlockSpec((G, K, n), lambda i: (0, 0, 0)),
        out_shape=jax.ShapeDtypeStruct((G, K, n), F32),
        compiler_params=_cparams(("arbitrary",)),
    )(a, dy)


def blockdiag_linear(a, w, name):
    G = w.shape[0]

    @jax.custom_vjp
    def op(a, w):
        return blockdiag_nn(a, w, False, name + "_fwd")

    def fwd(a, w):
        return blockdiag_nn(a, w, False, name + "_fwd"), (a, w)

    def bwd(res, dy):
        a, w = res
        return blockdiag_nn(dy, w, True, name + "_dx"), blockdiag_tn(a, dy, G, name + "_dw")

    op.defvjp(fwd, bwd)
    return op(a, w)


def linear(a, w, w_grad_proxy, diag=False, grad_dtype=F32, name="lin"):
    G = w.shape[0]

    @jax.custom_vjp
    def op(a, w, proxy):
        return matmul_nn(a, w, diag, name + "_fwd")

    def fwd(a, w, proxy):
        return matmul_nn(a, w, diag, name + "_fwd"), (a, w)

    def bwd(res, dy):
        a, w = res
        da = matmul_nt(dy, w, diag, a.dtype, name + "_dx")
        dw = matmul_tn(a, dy, G, diag, grad_dtype, name + "_dw")
        return da, None, dw

    op.defvjp(fwd, bwd)
    return op(a, w, w_grad_proxy)


def rowwise(name, f, blocks, params, pkinds, n_out, tb, cb=None, nograd=()):
    rows = blocks[0].shape[0]
    tb = min(tb, rows)
    assert rows % tb == 0
    nrb = rows // tb
    nb, npar = len(blocks), len(params)
    widths = [b.shape[1] for b in blocks]
    if cb is None:
        ncb = 1
        bw = widths
    else:
        assert all(w == widths[0] for w in widths) and widths[0] % cb == 0
        ncb = widths[0] // cb
        bw = [cb] * nb
    pw = []
    for p, kind in zip(params, pkinds):
        full = p.shape[-1]
        pw.append(full if cb is None or full != widths[0] else cb)
    for p, kind in zip(params, pkinds):
        assert p.shape[:-1] == ((1,) if kind == "vec" else (2, 1)), (name, p.shape, kind)

    shapes = jax.eval_shape(
        f, *[jax.ShapeDtypeStruct((tb, w), F32) for w in bw],
        *[jax.ShapeDtypeStruct((1, w), p.dtype) for w, p in zip(pw, params)])
    shapes = tuple(shapes)
    out_sh, acc_sh = shapes[:n_out], shapes[n_out:]
    n_acc = len(acc_sh)
    for s in out_sh:
        assert s.shape[0] == tb
    for s in acc_sh:
        assert s.shape[0] == 1

    def blk_spec(w):
        return pl.BlockSpec((tb, w), lambda c, r: (r, c))

    def par_spec(w, kind, full):
        col = (lambda c: c) if (cb is not None and full == widths[0]) else (lambda c: 0)
        if kind == "vec":
            return pl.BlockSpec((1, w), lambda c, r: (0, col(c)))
        return pl.BlockSpec((None, 1, w), lambda c, r: (jnp.minimum(r, 1), 0, col(c)))

    blk_specs = [blk_spec(w) for w in bw]
    par_specs = [par_spec(w, k, p.shape[-1]) for w, k, p in zip(pw, pkinds, params)]
    sem = _cparams(("parallel", "arbitrary"))

    def run_fwd(blocks, params):
        def body(*refs):
            ins = [r[...].astype(F32) for r in refs[:nb + npar]]
            outs = refs[nb + npar:]
            res = f(*ins)
            r_id = pl.program_id(1)
            for o_ref, val in zip(outs[:n_out], res[:n_out]):
                o_ref[...] = val
            for o_ref, val in zip(outs[n_out:], res[n_out:]):
                @pl.when(r_id == 0)
                def _(o_ref=o_ref, val=val):
                    o_ref[...] = val

                @pl.when(r_id > 0)
                def _(o_ref=o_ref, val=val):
                    o_ref[...] += val

        out_specs = [blk_spec(s.shape[1]) for s in out_sh] + \
                    [pl.BlockSpec((1, s.shape[1]), lambda c, r: (0, c)) for s in acc_sh]
        out_shape = [jax.ShapeDtypeStruct((rows, s.shape[1] * ncb), s.dtype) for s in out_sh] + \
                    [jax.ShapeDtypeStruct((1, s.shape[1] * ncb), s.dtype) for s in acc_sh]
        return tuple(pl.pallas_call(
            body, name=name + "_fwd", grid=(ncb, nrb),
            in_specs=blk_specs + par_specs, out_specs=out_specs, out_shape=out_shape,
            compiler_params=sem)(*blocks, *params))

    def run_bwd(blocks, params, cts):
        d_outs, d_accs = list(cts[:n_out]), list(cts[n_out:])
        want = [i for i in range(nb) if i not in nograd]

        def body(*refs):
            k = nb + npar
            ins = [r[...].astype(F32) for r in refs[:k]]
            ct = tuple(r[...] for r in refs[k:k + n_out + n_acc])
            outs = refs[k + n_out + n_acc:]
            _, vjp = jax.vjp(lambda *a: tuple(f(*a)), *ins)
            grads = vjp(ct)
            r_id = pl.program_id(1)
            for o_ref, i in zip(outs[:len(want)], want):
                o_ref[...] = grads[i].astype(o_ref.dtype)
            for o_ref, g, kind in zip(outs[len(want):], grads[nb:], pkinds):
                first = (r_id == 0) if kind == "vec" else (r_id <= 1)

                @pl.when(first)
                def _(o_ref=o_ref, g=g):
                    o_ref[...] = g.astype(o_ref.dtype)

                @pl.when(jnp.logical_not(first))
                def _(o_ref=o_ref, g=g):
                    o_ref[...] += g.astype(o_ref.dtype)

        ct_specs = [blk_spec(s.shape[1]) for s in out_sh] + \
                   [pl.BlockSpec((1, s.shape[1]), lambda c, r: (0, c)) for s in acc_sh]
        out_specs = [blk_specs[i] for i in want] + par_specs
        out_shape = [jax.ShapeDtypeStruct(blocks[i].shape, blocks[i].dtype) for i in want] + \
                    [jax.ShapeDtypeStruct(p.shape, p.dtype) for p in params]
        res = pl.pallas_call(
            body, name=name + "_bwd", grid=(ncb, nrb),
            in_specs=blk_specs + par_specs + ct_specs, out_specs=out_specs, out_shape=out_shape,
            compiler_params=sem)(*blocks, *params, *d_outs, *d_accs)
        d_blocks = [None] * nb
        for i, g in zip(want, res[:len(want)]):
            d_blocks[i] = g
        return tuple(d_blocks), tuple(res[len(want):])

    @jax.custom_vjp
    def op(blocks, params):
        return run_fwd(blocks, params)

    def op_fwd(blocks, params):
        return run_fwd(blocks, params), (blocks, params)

    def op_bwd(res, cts):
        return run_bwd(res[0], res[1], cts)

    op.defvjp(op_fwd, op_bwd)
    return op(tuple(blocks), tuple(params))


def _conv_window(prev_ref, cur_ref, next_ref, win_ref, i, tb, rows, n_ctx):
    starts = (i == 0) | (i * tb == n_ctx)
    ends = ((i + 1) * tb == rows) | ((i + 1) * tb == n_ctx)
    win_ref[pl.ds(0, CONV_HALO), :] = jnp.where(starts, 0.0, prev_ref[...].astype(F32))
    win_ref[pl.ds(CONV_HALO, tb), :] = cur_ref[...].astype(F32)
    win_ref[pl.ds(CONV_HALO + tb, CONV_HALO), :] = jnp.where(ends, 0.0, next_ref[...].astype(F32))


def _conv_specs(rows, tb, cb):
    hb = tb // CONV_HALO
    last = rows // CONV_HALO - 1
    prev = pl.BlockSpec((CONV_HALO, cb), lambda c, i: (jnp.maximum(i * hb - 1, 0), c))
    cur = pl.BlockSpec((tb, cb), lambda c, i: (i, c))
    nxt = pl.BlockSpec((CONV_HALO, cb), lambda c, i: (jnp.minimum((i + 1) * hb, last), c))
    return prev, cur, nxt


CONV_CHUNK = 32


def _conv_taps(win_ref, phase_ref, K, pad_left, tb):
    phased = K > SUBLANE
    if phased:
        span = tb + 2 * CONV_HALO - SUBLANE
        for p in range(SUBLANE):
            phase_ref[p] = win_ref[pl.ds(p, span), :]

    def tap(k, r0):
        off = CONV_HALO + k - pad_left
        if phased:
            return phase_ref[off % SUBLANE, pl.ds(r0 + off - off % SUBLANE, CONV_CHUNK), :]
        return win_ref[pl.ds(r0 + off, CONV_CHUNK), :]

    return tap


def _conv_scratch(K, tb, cb):
    phases = (SUBLANE, tb + 2 * CONV_HALO - SUBLANE, cb) if K > SUBLANE else (1, SUBLANE, LANE)
    return [pltpu.VMEM((tb + 2 * CONV_HALO, cb), F32), pltpu.VMEM(phases, F32)]


def _dwconv_apply(x, w, b, pad_left, n_ctx, out_dtype, name):
    rows, C = x.shape
    K = w.shape[0]
    tb = 256
    cb = _div_tile(C, 512, LANE)
    assert rows % tb == 0 and n_ctx % tb == 0 and K - 1 <= CONV_HALO and tb % CONV_CHUNK == 0

    def body(prev_ref, cur_ref, next_ref, w_ref, b_ref, o_ref, win_ref, phase_ref):
        i = pl.program_id(1)
        _conv_window(prev_ref, cur_ref, next_ref, win_ref, i, tb, rows, n_ctx)
        tap = _conv_taps(win_ref, phase_ref, K, pad_left, tb)
        wk = [w_ref[pl.ds(k, 1), :] for k in range(K)]
        for r0 in range(0, tb, CONV_CHUNK):
            acc = b_ref[...] + tap(0, r0) * wk[0]
            for k in range(1, K):
                acc = acc + tap(k, r0) * wk[k]
            o_ref[pl.ds(r0, CONV_CHUNK), :] = acc.astype(out_dtype)

    prev, cur, nxt = _conv_specs(rows, tb, cb)
    return pl.pallas_call(
        body, name=name, grid=(C // cb, rows // tb),
        in_specs=[prev, cur, nxt, pl.BlockSpec((K, cb), lambda c, i: (0, c)),
                  pl.BlockSpec((1, cb), lambda c, i: (0, c))],
        out_specs=pl.BlockSpec((tb, cb), lambda c, i: (i, c)),
        out_shape=jax.ShapeDtypeStruct((rows, C), out_dtype),
        scratch_shapes=_conv_scratch(K, tb, cb),
        compiler_params=_cparams(("parallel", "arbitrary")),
    )(x, x, x, w, b)


def _dwconv_wgrad(x, dy, K, pad_left, n_ctx, name):
    rows, C = x.shape
    tb = 256
    cb = _div_tile(C, 512, LANE)

    def body(prev_ref, cur_ref, next_ref, dy_ref, dw_ref, db_ref, win_ref, phase_ref):
        i = pl.program_id(1)
        _conv_window(prev_ref, cur_ref, next_ref, win_ref, i, tb, rows, n_ctx)
        tap = _conv_taps(win_ref, phase_ref, K, pad_left, tb)

        @pl.when(i == 0)
        def _():
            dw_ref[...] = jnp.zeros_like(dw_ref)
            db_ref[...] = jnp.zeros_like(db_ref)

        chunks = range(0, tb, CONV_CHUNK)
        dy = lambda r0: dy_ref[pl.ds(r0, CONV_CHUNK), :].astype(F32)
        db_ref[...] += jnp.sum(sum(dy(r0) for r0 in chunks), axis=0, keepdims=True)
        for k in range(K):
            acc = sum(tap(k, r0) * dy(r0) for r0 in chunks)
            dw_ref[pl.ds(k, 1), :] += jnp.sum(acc, axis=0, keepdims=True)

    prev, cur, nxt = _conv_specs(rows, tb, cb)
    return pl.pallas_call(
        body, name=name, grid=(C // cb, rows // tb),
        in_specs=[prev, cur, nxt, pl.BlockSpec((tb, cb), lambda c, i: (i, c))],
        out_specs=[pl.BlockSpec((K, cb), lambda c, i: (0, c)), pl.BlockSpec((1, cb), lambda c, i: (0, c))],
        out_shape=[jax.ShapeDtypeStruct((K, C), F32), jax.ShapeDtypeStruct((1, C), F32)],
        scratch_shapes=_conv_scratch(K, tb, cb),
        compiler_params=_cparams(("parallel", "arbitrary")),
    )(x, x, x, dy)


def dwconv(x, w, b, pad_left, n_ctx, name):
    K = w.shape[0]

    @jax.custom_vjp
    def op(x, w, b):
        return _dwconv_apply(x, w, b, pad_left, n_ctx, F32, name + "_fwd")

    def fwd(x, w, b):
        return _dwconv_apply(x, w, b, pad_left, n_ctx, F32, name + "_fwd"), (x, w)

    def bwd(res, dy):
        x, w = res
        dx = _dwconv_apply(dy, w[::-1], jnp.zeros((1, w.shape[1]), F32), K - 1 - pad_left, n_ctx, x.dtype,
                           name + "_dx")
        dw, db = _dwconv_wgrad(x, dy, K, pad_left, n_ctx, name + "_dw")
        return dx, dw, db

    op.defvjp(fwd, bwd)
    return op(x, w, b)


GLU_HALO = 16


def _glu_specs(rows, tb, cb):
    hb = tb // GLU_HALO
    last = rows // GLU_HALO - 1
    prev = pl.BlockSpec((GLU_HALO, cb), lambda c, i: (jnp.maximum(i * hb - 1, 0), c))
    cur = pl.BlockSpec((tb, cb), lambda c, i: (i, c))
    nxt = pl.BlockSpec((GLU_HALO, cb), lambda c, i: (jnp.minimum((i + 1) * hb, last), c))
    return [prev, cur, nxt]


GLU_CHUNK = 16
GLU_PAD = 8


def _glu_window(prev_ref, cur_ref, next_ref, win_ref, i, tb, rows, n_ctx, pad):
    starts = (i == 0) | (i * tb == n_ctx)
    ends = ((i + 1) * tb == rows) | ((i + 1) * tb == n_ctx)
    win_ref[pl.ds(pad, GLU_HALO), :] = jnp.where(starts, 0.0, prev_ref[...].astype(F32))
    win_ref[pl.ds(pad + GLU_HALO, tb), :] = cur_ref[...].astype(F32)
    win_ref[pl.ds(pad + GLU_HALO + tb, GLU_HALO), :] = jnp.where(ends, 0.0, next_ref[...].astype(F32))
    return starts, ends


def _glu_taps(gwin_ref, row):
    return [gwin_ref[pl.ds(row - 1 + k, GLU_CHUNK), :] for k in range(3)]


def _ffn_glu_fwd(g, v, w, b, n_ctx, name):
    rows, C = g.shape
    tb = ROW_BLOCK
    cb = _div_tile(C, 1408, LANE)
    assert rows % tb == 0 and n_ctx % tb == 0 and w.shape[0] == 3 and tb % GLU_CHUNK == 0

    def body(gp_ref, gc_ref, gn_ref, v_ref, w_ref, b_ref, o_ref, gwin_ref):
        i = pl.program_id(1)
        _glu_window(gp_ref, gc_ref, gn_ref, gwin_ref, i, tb, rows, n_ctx, 0)
        w0, w1, w2, bias = w_ref[pl.ds(0, 1), :], w_ref[pl.ds(1, 1), :], w_ref[pl.ds(2, 1), :], b_ref[...]
        for j in range(tb // GLU_CHUNK):
            own = pl.ds(j * GLU_CHUNK, GLU_CHUNK)
            t = _glu_taps(gwin_ref, GLU_HALO + j * GLU_CHUNK)
            gate = bias + t[0] * w0 + t[1] * w1 + t[2] * w2
            o_ref[own, :] = (gate * jax.nn.sigmoid(gate) * v_ref[own, :].astype(F32)).astype(BF16)

    blk = pl.BlockSpec((tb, cb), lambda c, i: (i, c))
    return pl.pallas_call(
        body, name=name, grid=(C // cb, rows // tb),
        in_specs=_glu_specs(rows, tb, cb) + [blk, pl.BlockSpec((3, cb), lambda c, i: (0, c)),
                                             pl.BlockSpec((1, cb), lambda c, i: (0, c))],
        out_specs=blk, out_shape=jax.ShapeDtypeStruct((rows, C), BF16),
        scratch_shapes=[pltpu.VMEM((tb + 2 * GLU_HALO, cb), F32)],
        compiler_params=_cparams(("parallel", "arbitrary")),
    )(g, g, g, v, w, b)


def _ffn_glu_bwd(g, v, w, b, da, n_ctx, name):
    rows, C = g.shape
    tb = ROW_BLOCK
    cb = _div_tile(C, 1408, LANE)
    n_chunks = (tb + 2 * GLU_HALO) // GLU_CHUNK
    assert GLU_HALO == GLU_CHUNK

    def body(gp_ref, gc_ref, gn_ref, vp_ref, vc_ref, vn_ref, dp_ref, dc_ref, dn_ref, w_ref, b_ref,
             dg_ref, dv_ref, dw_ref, db_ref, gwin_ref, dgate_ref):
        i = pl.program_id(1)
        gwin_ref[pl.ds(0, GLU_PAD), :] = jnp.zeros((GLU_PAD, cb), F32)
        gwin_ref[pl.ds(GLU_PAD + tb + 2 * GLU_HALO, GLU_PAD), :] = jnp.zeros((GLU_PAD, cb), F32)
        starts, ends = _glu_window(gp_ref, gc_ref, gn_ref, gwin_ref, i, tb, rows, n_ctx, GLU_PAD)
        w0, w1, w2, bias = w_ref[pl.ds(0, 1), :], w_ref[pl.ds(1, 1), :], w_ref[pl.ds(2, 1), :], b_ref[...]
        db_acc = jnp.zeros((GLU_CHUNK, cb), F32)
        dw_acc = [jnp.zeros((GLU_CHUNK, cb), F32) for _ in range(3)]
        for j in range(n_chunks):
            t = _glu_taps(gwin_ref, GLU_PAD + j * GLU_CHUNK)
            gate = bias + t[0] * w0 + t[1] * w1 + t[2] * w2
            sig = jax.nn.sigmoid(gate)
            own = pl.ds((j - 1) * GLU_CHUNK, GLU_CHUNK)
            if j == 0:
                dout, val = jnp.where(starts, 0.0, dp_ref[...].astype(F32)), vp_ref[...].astype(F32)
            elif j == n_chunks - 1:
                dout, val = jnp.where(ends, 0.0, dn_ref[...].astype(F32)), vn_ref[...].astype(F32)
            else:
                dout, val = dc_ref[own, :].astype(F32), vc_ref[own, :].astype(F32)
            dgate = dout * val * (sig * (1.0 + gate * (1.0 - sig)))
            dgate_ref[pl.ds(j * GLU_CHUNK, GLU_CHUNK), :] = dgate
            if 0 < j < n_chunks - 1:
                dv_ref[own, :] = (dout * (gate * sig)).astype(dv_ref.dtype)
                db_acc = db_acc + dgate
                dw_acc = [acc + dgate * tap for acc, tap in zip(dw_acc, t)]
        for j in range(tb // GLU_CHUNK):
            r = GLU_HALO + j * GLU_CHUNK
            dg = (dgate_ref[pl.ds(r + 1, GLU_CHUNK), :] * w0 + dgate_ref[pl.ds(r, GLU_CHUNK), :] * w1
                  + dgate_ref[pl.ds(r - 1, GLU_CHUNK), :] * w2)
            dg_ref[pl.ds(j * GLU_CHUNK, GLU_CHUNK), :] = dg.astype(dg_ref.dtype)

        @pl.when(i == 0)
        def _():
            dw_ref[...] = jnp.zeros_like(dw_ref)
            db_ref[...] = jnp.zeros_like(db_ref)

        db_ref[...] += jnp.sum(db_acc, axis=0, keepdims=True)
        for k in range(3):
            dw_ref[pl.ds(k, 1), :] += jnp.sum(dw_acc[k], axis=0, keepdims=True)

    blk = pl.BlockSpec((tb, cb), lambda c, i: (i, c))
    specs = _glu_specs(rows, tb, cb)
    return pl.pallas_call(
        body, name=name, grid=(C // cb, rows // tb),
        in_specs=specs * 3 + [pl.BlockSpec((3, cb), lambda c, i: (0, c)), pl.BlockSpec((1, cb), lambda c, i: (0, c))],
        out_specs=[blk, blk, pl.BlockSpec((3, cb), lambda c, i: (0, c)), pl.BlockSpec((1, cb), lambda c, i: (0, c))],
        out_shape=[jax.ShapeDtypeStruct((rows, C), g.dtype), jax.ShapeDtypeStruct((rows, C), v.dtype),
                   jax.ShapeDtypeStruct((3, C), F32), jax.ShapeDtypeStruct((1, C), F32)],
        scratch_shapes=[pltpu.VMEM((tb + 2 * GLU_HALO + 2 * GLU_PAD, cb), F32),
                        pltpu.VMEM((tb + 2 * GLU_HALO, cb), F32)],
        compiler_params=_cparams(("parallel", "arbitrary")),
    )(g, g, g, v, v, v, da, da, da, w, b)


def ffn_glu(g, v, w, b, n_ctx, name):
    @jax.custom_vjp
    def op(g, v, w, b):
        return _ffn_glu_fwd(g, v, w, b, n_ctx, name + "_fwd")

    def fwd(g, v, w, b):
        return _ffn_glu_fwd(g, v, w, b, n_ctx, name + "_fwd"), (g, v, w, b)

    def bwd(res, da):
        g, v, w, b = res
        return tuple(_ffn_glu_bwd(g, v, w, b, da, n_ctx, name + "_bwd"))

    op.defvjp(fwd, bwd)
    return op(g, v, w, b)


def _block_scan(a, b, reverse):
    tb = a.shape[0]
    row = lax.broadcasted_iota(jnp.int32, (tb, 1), 0)
    s = 1
    while s < tb:
        if reverse:
            keep = row < tb - s
            a_sh = pltpu.roll(a, tb - s, 0)
            b_sh = pltpu.roll(b, tb - s, 0)
        else:
            keep = row >= s
            a_sh = pltpu.roll(a, s, 0)
            b_sh = pltpu.roll(b, s, 0)
        b = jnp.where(keep, a * b_sh + b, b)
        a = jnp.where(keep, a * a_sh, a)
        s *= 2
    return a, b


def _shift_in(h, carry, reverse):
    tb = h.shape[0]
    row = lax.broadcasted_iota(jnp.int32, (tb, 1), 0)
    if reverse:
        return jnp.where(row == tb - 1, carry, pltpu.roll(h, tb - 1, 0))
    return jnp.where(row == 0, carry, pltpu.roll(h, 1, 0))


def _scan_maps(nrb, rot, reverse):
    def phys(c, i):
        q = (nrb - 1 - i) if reverse else i
        return (lax.rem(q + rot, nrb), c)
    return phys


def _scan_fwd(a, b, rot, reverse, name):
    rows, C = a.shape
    tb = 256
    cb = _div_tile(C, 512, LANE)
    nrb = rows // tb
    last_row = 0 if reverse else tb - 1

    def body(a_ref, b_ref, h_ref, hp_ref, carry_ref):
        i = pl.program_id(1)

        @pl.when(i == 0)
        def _():
            carry_ref[...] = jnp.zeros_like(carry_ref)

        carry = carry_ref[pl.ds(0, 1), :]
        A, B = _block_scan(a_ref[...], b_ref[...], reverse)
        h = A * carry + B
        h_ref[...] = h
        hp_ref[...] = _shift_in(h, carry, reverse)
        carry_ref[pl.ds(0, 1), :] = h[last_row:last_row + 1, :]

    spec = pl.BlockSpec((tb, cb), _scan_maps(nrb, rot, reverse))
    return pl.pallas_call(
        body, name=name, grid=(C // cb, nrb),
        in_specs=[spec, spec], out_specs=[spec, spec],
        out_shape=[jax.ShapeDtypeStruct((rows, C), F32)] * 2,
        scratch_shapes=[pltpu.VMEM((SUBLANE, cb), F32)],
        compiler_params=_cparams(("parallel", "arbitrary")),
    )(a, b)


def _scan_bwd(a, dh, h_prev, rot, reverse, name):
    rows, C = a.shape
    tb = 256
    cb = _div_tile(C, 512, LANE)
    nrb = rows // tb
    adj = not reverse
    last_row = 0 if adj else tb - 1

    def body(a_ref, dh_ref, hp_ref, da_ref, db_ref, carry_ref):
        i = pl.program_id(1)

        @pl.when(i == 0)
        def _():
            carry_ref[...] = jnp.zeros_like(carry_ref)

        carry = carry_ref[pl.ds(0, 1), :]
        a = a_ref[...]
        dh = dh_ref[...]
        A, B = _block_scan(a, a * dh, adj)
        u = A * carry + B
        g = dh + _shift_in(u, carry, adj)
        db_ref[...] = g
        da_ref[...] = g * hp_ref[...]
        carry_ref[pl.ds(0, 1), :] = u[last_row:last_row + 1, :]

    spec = pl.BlockSpec((tb, cb), _scan_maps(nrb, rot, adj))
    return pl.pallas_call(
        body, name=name, grid=(C // cb, nrb),
        in_specs=[spec, spec, spec], out_specs=[spec, spec],
        out_shape=[jax.ShapeDtypeStruct((rows, C), F32)] * 2,
        scratch_shapes=[pltpu.VMEM((SUBLANE, cb), F32)],
        compiler_params=_cparams(("parallel", "arbitrary")),
    )(a, dh, h_prev)


def linear_scan(a, b, rot, reverse, name):
    @jax.custom_vjp
    def op(a, b):
        return _scan_fwd(a, b, rot, reverse, name + "_fwd")[0]

    def fwd(a, b):
        h, hp = _scan_fwd(a, b, rot, reverse, name + "_fwd")
        return h, (a, hp)

    def bwd(res, dh):
        a, hp = res
        da, db = _scan_bwd(a, dh, hp, rot, reverse, name + "_bwd")
        return da, db

    op.defvjp(fwd, bwd)
    return op(a, b)


def rope_tables(n_ctx, n_lat):
    t = jnp.arange(n_lat)
    pos = jnp.stack([t // GRID_W, t % GRID_W], axis=-1).astype(F32)
    freq = ROPE_THETA ** (-jnp.arange(ROPE_FREQS, dtype=F32) / ROPE_FREQS)
    ang = pos[:, :, None] * freq
    cos, sin = jnp.cos(ang), jnp.sin(ang)
    c = jnp.concatenate([cos[:, 0], cos[:, 0], cos[:, 1], cos[:, 1]], axis=-1)
    s = jnp.concatenate([-sin[:, 0], sin[:, 0], -sin[:, 1], sin[:, 1]], axis=-1)
    c = jnp.concatenate([jnp.ones((n_ctx, HEAD_DIM), F32), c], axis=0)
    s = jnp.concatenate([jnp.zeros((n_ctx, HEAD_DIM), F32), s], axis=0)
    return c, s


def _rope_apply(qkv, c_tab, s_tab, n_rot_heads, out_dtype, name):
    rows, cols = qkv.shape
    tb = _div_tile(rows, 768, SUBLANE)
    heads = cols // HEAD_DIM
    hb = max(h for h in (4, 2, 1) if heads % h == 0 and n_rot_heads % h == 0)
    wb = hb * HEAD_DIM

    def body(x_ref, c_ref, s_ref, o_ref):
        x = x_ref[...].astype(F32)
        lane = lax.broadcasted_iota(jnp.int32, x.shape, 1)
        swapped = jnp.where((lane & 63) < 32, pltpu.roll(x, wb - 32, 1), pltpu.roll(x, 32, 1))
        roped = x * jnp.tile(c_ref[...], (1, hb)) + swapped * jnp.tile(s_ref[...], (1, hb))
        o_ref[...] = jnp.where(pl.program_id(1) * hb < n_rot_heads, roped, x).astype(out_dtype)

    tab = pl.BlockSpec((tb, HEAD_DIM), lambda i, j: (i, 0))
    blk = pl.BlockSpec((tb, wb), lambda i, j: (i, j))
    return pl.pallas_call(
        body, name=name, grid=(rows // tb, cols // wb),
        in_specs=[blk, tab, tab], out_specs=blk,
        out_shape=jax.ShapeDtypeStruct((rows, cols), out_dtype),
        compiler_params=_cparams(("parallel", "arbitrary")),
    )(qkv, c_tab, s_tab)


def rope(qkv, c_tab, s_tab, n_rot_heads, name):
    @jax.custom_vjp
    def op(qkv):
        return _rope_apply(qkv, c_tab, s_tab, n_rot_heads, BF16, name + "_fwd")

    def fwd(qkv):
        return _rope_apply(qkv, c_tab, s_tab, n_rot_heads, BF16, name + "_fwd"), None

    def bwd(_, d):
        return (_rope_apply(d, c_tab, -s_tab, n_rot_heads, F32, name + "_bwd"),)

    op.defvjp(fwd, bwd)
    return op(qkv)


def _attn_in_specs(H, KV, n_ctx, nqb):
    ncb = n_ctx // WINDOW_BLOCK
    G = GQA_GROUP

    def loc(delta, col0):
        return pl.BlockSpec((WINDOW_BLOCK, HEAD_DIM),
                            lambda g, i: (jnp.clip(i + delta, ncb, nqb - 1), col0 + g))

    q = pl.BlockSpec((WINDOW_BLOCK, G * HEAD_DIM), lambda g, i: (i, g))
    kc = pl.BlockSpec((n_ctx, HEAD_DIM), lambda g, i: (0, H + g))
    vc = pl.BlockSpec((n_ctx, HEAD_DIM), lambda g, i: (0, H + KV + g))
    sink = pl.BlockSpec((None, G * WINDOW_BLOCK, 1), lambda g, i: (g, 0, 0))
    bias = pl.BlockSpec((None, G * WINDOW_BLOCK, n_ctx + 3 * WINDOW_BLOCK),
                        lambda g, i: (jnp.where(i < ncb, 3, jnp.where(i == ncb, 1, jnp.where(i == nqb - 1, 2, 0))),
                                      0, 0))
    return [q, kc, loc(-1, H), loc(0, H), loc(1, H), vc, loc(-1, H + KV), loc(0, H + KV), loc(1, H + KV), sink, bias]


def _attn_bias(n_ctx):
    nq, nk = GQA_GROUP * WINDOW_BLOCK, n_ctx + 3 * WINDOW_BLOCK
    r = (jnp.arange(nq) % WINDOW_BLOCK)[:, None]
    col = jnp.arange(nk)[None, :]
    blk = (col - n_ctx) // WINDOW_BLOCK
    rk = (col - n_ctx) % WINDOW_BLOCK
    is_ctx = jnp.broadcast_to(col < n_ctx, (nq, nk))
    prev = (blk == 0) & (rk >= r)
    cur = jnp.broadcast_to(blk == 1, (nq, nk))
    nxt = (blk == 2) & (rk <= r)
    valid = jnp.stack([is_ctx | prev | cur | nxt, is_ctx | cur | nxt, is_ctx | prev | cur, is_ctx])
    return jnp.where(valid, 0.0, NEG_INF).astype(F32)


def _stack_heads(x):
    return jnp.concatenate([x[:, h * HEAD_DIM:(h + 1) * HEAD_DIM] for h in range(GQA_GROUP)], axis=0)


def _attn_probs(q_ref, kc_ref, kp_ref, kcur_ref, kn_ref, sink_ref, bias_ref):
    qs = _stack_heads(q_ref[...]).astype(BF16)
    k = jnp.concatenate([kc_ref[...], kp_ref[...], kcur_ref[...], kn_ref[...]], axis=0).astype(BF16)
    s = _dot(qs, k, ((1,), (1,))) * (HEAD_DIM ** -0.5) + bias_ref[...]
    sk = sink_ref[...]
    m = jnp.maximum(jnp.max(s, axis=1, keepdims=True), sk)
    e = jnp.exp(s - m)
    es = jnp.exp(sk - m)
    inv = 1.0 / (jnp.sum(e, axis=1, keepdims=True) + es)
    return qs, k, e * inv, es * inv


def _attn_fwd(qkv, sink_col, H, KV, n_ctx, name):
    rows = qkv.shape[0]
    nqb = rows // WINDOW_BLOCK
    assert (rows - n_ctx) // WINDOW_BLOCK >= 2
    G = GQA_GROUP

    def body(q_ref, kc_ref, kp_ref, kcur_ref, kn_ref, vc_ref, vp_ref, vcur_ref, vn_ref, sink_ref, bias_ref, o_ref):
        _, _, p, _ = _attn_probs(q_ref, kc_ref, kp_ref, kcur_ref, kn_ref, sink_ref, bias_ref)
        v = jnp.concatenate([vc_ref[...], vp_ref[...], vcur_ref[...], vn_ref[...]], axis=0).astype(BF16)
        o = _dot(p, v, ((1,), (0,)))
        for h in range(G):
            o_ref[:, h * HEAD_DIM:(h + 1) * HEAD_DIM] = o[h * WINDOW_BLOCK:(h + 1) * WINDOW_BLOCK, :].astype(BF16)

    return pl.pallas_call(
        body, name=name, grid=(KV, nqb),
        in_specs=_attn_in_specs(H, KV, n_ctx, nqb),
        out_specs=pl.BlockSpec((WINDOW_BLOCK, G * HEAD_DIM), lambda g, i: (i, g)),
        out_shape=jax.ShapeDtypeStruct((rows, H * HEAD_DIM), BF16),
        compiler_params=_cparams(("parallel", "arbitrary")),
    )(*([qkv] * 9), sink_col, _attn_bias(n_ctx))


def _attn_bwd(qkv, sink_col, o, do, H, KV, n_ctx, name):
    rows = qkv.shape[0]
    nqb = rows // WINDOW_BLOCK
    G = GQA_GROUP
    WB = WINDOW_BLOCK

    def body(q_ref, kc_ref, kp_ref, kcur_ref, kn_ref, vc_ref, vp_ref, vcur_ref, vn_ref, sink_ref, bias_ref, o_ref,
             do_ref, dq_ref, dkc_ref, dvc_ref, dkp_ref, dkcur_ref, dkn_ref, dvp_ref, dvcur_ref, dvn_ref, dsink_ref):
        i = pl.program_id(1)
        qs, k, p, ps = _attn_probs(q_ref, kc_ref, kp_ref, kcur_ref, kn_ref, sink_ref, bias_ref)
        v = jnp.concatenate([vc_ref[...], vp_ref[...], vcur_ref[...], vn_ref[...]], axis=0).astype(BF16)
        do_s = _stack_heads(do_ref[...]).astype(F32)
        o_s = _stack_heads(o_ref[...]).astype(F32)
        delta = jnp.sum(do_s * o_s, axis=1, keepdims=True)
        dp = _dot(do_s, v, ((1,), (1,)))
        ds = p * (dp - delta) * (HEAD_DIM ** -0.5)
        dq = _dot(ds, k, ((1,), (0,)))
        dk = _dot(ds, qs, ((0,), (0,)))
        dv = _dot(p, do_s, ((0,), (0,)))
        for h in range(G):
            dq_ref[:, h * HEAD_DIM:(h + 1) * HEAD_DIM] = dq[h * WB:(h + 1) * WB, :]

        @pl.when(i == 0)
        def _():
            dkc_ref[...] = jnp.zeros_like(dkc_ref)
            dvc_ref[...] = jnp.zeros_like(dvc_ref)
            dsink_ref[...] = jnp.zeros_like(dsink_ref)

        dkc_ref[...] += dk[:n_ctx]
        dvc_ref[...] += dv[:n_ctx]
        dsink_ref[...] += -ps * delta
        for j, (dk_ref, dv_ref) in enumerate(((dkp_ref, dvp_ref), (dkcur_ref, dvcur_ref), (dkn_ref, dvn_ref))):
            dk_ref[...] = dk[n_ctx + j * WB:n_ctx + (j + 1) * WB]
            dv_ref[...] = dv[n_ctx + j * WB:n_ctx + (j + 1) * WB]

    qblk = pl.BlockSpec((WB, G * HEAD_DIM), lambda g, i: (i, g))
    ctx = pl.BlockSpec((n_ctx, HEAD_DIM), lambda g, i: (0, g))
    piece = pl.BlockSpec((WB, HEAD_DIM), lambda g, i: (i, g))
    sink = pl.BlockSpec((None, G * WB, 1), lambda g, i: (g, 0, 0))
    kv_shape = jax.ShapeDtypeStruct((rows, KV * HEAD_DIM), F32)
    ctx_shape = jax.ShapeDtypeStruct((n_ctx, KV * HEAD_DIM), F32)
    return pl.pallas_call(
        body, name=name, grid=(KV, nqb),
        in_specs=_attn_in_specs(H, KV, n_ctx, nqb) + [qblk, qblk],
        out_specs=[qblk, ctx, ctx] + [piece] * 6 + [sink],
        out_shape=[jax.ShapeDtypeStruct((rows, H * HEAD_DIM), F32), ctx_shape, ctx_shape] + [kv_shape] * 6 +
                  [jax.ShapeDtypeStruct(sink_col.shape, F32)],
        compiler_params=_cparams(("parallel", "arbitrary")),
    )(*([qkv] * 9), sink_col, _attn_bias(n_ctx), o, do)


def _shift_blocks(x, n_ctx, delta):
    lat = x[n_ctx:]
    z = jnp.zeros((WINDOW_BLOCK, x.shape[1]), x.dtype)
    if delta == 1:
        lat = jnp.concatenate([z, lat[:-WINDOW_BLOCK]], axis=0)
    elif delta == -1:
        lat = jnp.concatenate([lat[WINDOW_BLOCK:], z], axis=0)
    return jnp.concatenate([jnp.zeros((n_ctx, x.shape[1]), x.dtype), lat], axis=0)


def attention(qkv, sink_col, H, KV, n_ctx, name):
    @jax.custom_vjp
    def op(qkv, sink_col):
        return _attn_fwd(qkv, sink_col, H, KV, n_ctx, name + "_fwd")

    def fwd(qkv, sink_col):
        o = _attn_fwd(qkv, sink_col, H, KV, n_ctx, name + "_fwd")
        return o, (qkv, sink_col, o)

    def bwd(res, do):
        qkv, sink_col, o = res
        dq, dkc, dvc, dkp, dkcur, dkn, dvp, dvcur, dvn, dsink = _attn_bwd(qkv, sink_col, o, do, H, KV, n_ctx,
                                                                          name + "_bwd")

        def gather_pieces(prev, cur, nxt, ctx):
            pad = jnp.concatenate([ctx, jnp.zeros((qkv.shape[0] - n_ctx, ctx.shape[1]), F32)], axis=0)
            return rowwise(name + "_kvsum", lambda a, b, c, d: (a + b + c + d,),
                           [cur, _shift_blocks(prev, n_ctx, -1), _shift_blocks(nxt, n_ctx, 1), pad], [], [], 1, 256)[0]

        dk = gather_pieces(dkp, dkcur, dkn, dkc)
        dv = gather_pieces(dvp, dvcur, dvn, dvc)
        return jnp.concatenate([dq, dk, dv], axis=1).astype(qkv.dtype), dsink

    op.defvjp(fwd, bwd)
    return op(qkv, sink_col)


ROW_BLOCK = 256


def _rms_norm(x, g):
    return (x * lax.rsqrt(jnp.mean(x * x, axis=-1, keepdims=True) + NORM_EPS)) * g


def _modulate_f(x, g, shift, scale):
    return ((_rms_norm(x, g) * (1.0 + scale) + shift).astype(BF16),)


def _expm1(x):
    series = x * (1 + x / 2 * (1 + x / 3 * (1 + x / 4 * (1 + x / 5 * (1 + x / 6)))))
    return jnp.where(jnp.abs(x) < 0.1, series, jnp.exp(x) - 1.0)


def _lru_gates_f(ra, rx, uc, ba, bx, sp):
    r = jax.nn.sigmoid(ra + ba)
    ig = jax.nn.sigmoid(rx + bx)
    log_a = -LRU_C * r * sp
    return jnp.exp(log_a), jnp.sqrt(-_expm1(2.0 * log_a)) * (ig * uc)


def _ln_silu_f(z, g, b):
    mu = jnp.mean(z, axis=-1, keepdims=True)
    var = jnp.mean(jnp.square(z - mu), axis=-1, keepdims=True)
    return (jax.nn.silu((z - mu) * lax.rsqrt(var + NORM_EPS) * g + b).astype(BF16),)


def _loss_f(x, target, g):
    err = _rms_norm(x, g) - target
    return (jnp.sum(0.5 * err * err, axis=0, keepdims=True) / x.shape[1],)


def _modulate(x, g, shift, scale):
    return rowwise("modulate", _modulate_f, [x], [g, shift, scale], ["vec", "seg", "seg"], 1, ROW_BLOCK)[0]


def _residual_modulate(x, y, bias, gate, g, shift, scale):
    def f(x, y, *p):
        x_new = x + p[-4] * (y if bias is None else y + p[0])
        return x_new, _modulate_f(x_new, *p[-3:])[0]

    params = ([] if bias is None else [bias]) + [gate, g, shift, scale]
    kinds = ([] if bias is None else ["vec"]) + ["seg", "vec", "seg", "seg"]
    return rowwise("residual_modulate", f, [x, y], params, kinds, 2, ROW_BLOCK)


def _residual(x, y, gate):
    return rowwise("residual", lambda x, y, g: (x + g * y,), [x, y], [gate], ["seg"], 1, ROW_BLOCK)[0]


def _split_nn(a, b, g0, name):
    M, K = a.shape
    n = b.shape[2]
    tm = _div_tile(M, 768, SUBLANE)
    tn = _div_tile(n, 1408, LANE)
    nn = n // tn

    def body(a_ref, b_ref, o_ref):
        o_ref[...] = _dot(a_ref[...], b_ref[...], ((1,), (0,))).astype(BF16)

    return pl.pallas_call(
        body, name=name, grid=(M // tm, 2, nn),
        in_specs=[pl.BlockSpec((tm, K), lambda i, g, j: (i, 0)),
                  pl.BlockSpec((None, K, tn), lambda i, g, j: (g + g0, 0, j))],
        out_specs=pl.BlockSpec((tm, tn), lambda i, g, j: (i, g * nn + j)),
        out_shape=jax.ShapeDtypeStruct((M, 2 * n), BF16),
        compiler_params=_cparams(("parallel", "parallel", "parallel")),
    )(a, b)


def _split_nt(dy_a, dy_b, b, out_dtype, name):
    M = dy_a.shape[0]
    _, K, n = b.shape
    tm = _div_tile(M, 768, SUBLANE)
    tko = _div_tile(K, 1408, LANE)
    tr = _div_tile(n, 2816, LANE)
    nr = n // tr
    half = 2 * nr

    def body(dya_ref, dyb_ref, b_ref, o_ref, acc_ref):
        r = pl.program_id(2)

        @pl.when(r == 0)
        def _():
            acc_ref[...] = _dot(dya_ref[...], b_ref[...], ((1,), (1,)))

        @pl.when((r > 0) & (r < half))
        def _():
            acc_ref[...] += _dot(dya_ref[...], b_ref[...], ((1,), (1,)))

        @pl.when(r >= half)
        def _():
            acc_ref[...] += _dot(dyb_ref[...], b_ref[...], ((1,), (1,)))

        @pl.when(r == 2 * half - 1)
        def _():
            o_ref[...] = acc_ref[...].astype(out_dtype)

    return pl.pallas_call(
        body, name=name, grid=(M // tm, K // tko, 2 * half),
        in_specs=[pl.BlockSpec((tm, tr), lambda i, kk, r: (i, jnp.minimum(r, half - 1))),
                  pl.BlockSpec((tm, tr), lambda i, kk, r: (i, jnp.maximum(r - half, 0))),
                  pl.BlockSpec((None, tko, tr), lambda i, kk, r: (r // nr, kk, lax.rem(r, nr)))],
        out_specs=pl.BlockSpec((tm, tko), lambda i, kk, r: (i, kk)),
        out_shape=jax.ShapeDtypeStruct((M, K), out_dtype),
        scratch_shapes=[pltpu.VMEM((tm, tko), F32)],
        compiler_params=_cparams(("parallel", "parallel", "arbitrary")),
    )(dy_a, dy_b, b)


def _split_tn(a, dy_a, dy_b, out_dtype, name):
    M, K = a.shape
    n = dy_a.shape[1] // 2
    tm = _div_tile(M, 1408, SUBLANE)
    tk = _div_tile(K, 1408, LANE)
    tn = _div_tile(n, 1408, LANE)
    nkb, nn, nm = K // tk, n // tn, M // tm

    def body(a_ref, dya_ref, dyb_ref, o_ref, acc_ref):
        g, r = pl.program_id(0), pl.program_id(3)

        @pl.when(r == 0)
        def _():
            acc_ref[...] = jnp.zeros_like(acc_ref)

        @pl.when(g < 2)
        def _():
            acc_ref[...] += _dot(a_ref[...], dya_ref[...], ((0,), (0,)))

        @pl.when(g >= 2)
        def _():
            acc_ref[...] += _dot(a_ref[...], dyb_ref[...], ((0,), (0,)))

        @pl.when(r == nm - 1)
        def _():
            o_ref[...] = acc_ref[...].astype(out_dtype)

    return pl.pallas_call(
        body, name=name, grid=(4, nkb, nn, nm),
        in_specs=[pl.BlockSpec((tm, tk), lambda g, kk, j, r: (r, kk)),
                  pl.BlockSpec((tm, tn), lambda g, kk, j, r: (jnp.where(g < 2, r, 0), jnp.minimum(g, 1) * nn + j)),
                  pl.BlockSpec((tm, tn), lambda g, kk, j, r: (jnp.where(g >= 2, r, 0),
                                                              jnp.maximum(g - 2, 0) * nn + j))],
        out_specs=pl.BlockSpec((None, tk, tn), lambda g, kk, j, r: (g, kk, j)),
        out_shape=jax.ShapeDtypeStruct((4, K, n), out_dtype),
        scratch_shapes=[pltpu.VMEM((tk, tn), F32)],
        compiler_params=_cparams(("parallel", "parallel", "parallel", "arbitrary")),
    )(a, dy_a, dy_b)


def _split_linear(h, w, px, name):
    @jax.custom_vjp
    def op(h, w, px):
        return _split_nn(h, w, 0, name + "_a_fwd"), _split_nn(h, w, 2, name + "_b_fwd")

    def fwd(h, w, px):
        return (_split_nn(h, w, 0, name + "_a_fwd"), _split_nn(h, w, 2, name + "_b_fwd")), (h, w)

    def bwd(res, cts):
        h, w = res
        dh = _split_nt(cts[0], cts[1], w, h.dtype, name + "_dx")
        dw = _split_tn(h, cts[0], cts[1], BF16, name + "_dw")
        return dh, None, dw

    op.defvjp(fwd, bwd)
    return op(h, w, px)


def _attention_mixer(h, P, PX, j, n_ctx, tabs):
    H = P["attn_w_o"][j].shape[1] // HEAD_DIM
    KV = H // GQA_GROUP
    qkv = linear(h, P["attn_w_qkv"][j], PX["attn_w_qkv"][j], grad_dtype=BF16, name="attn_qkv")
    qkv = rope(qkv, tabs[0], tabs[1], H + KV, "rope")
    sink_col = jnp.repeat(P["attn_sink"][j].reshape(KV, GQA_GROUP), WINDOW_BLOCK, axis=1)[..., None]
    o = attention(qkv, sink_col, H, KV, n_ctx, "attn")
    return linear(o, P["attn_w_o"][j], PX["attn_w_o"][j], grad_dtype=BF16, name="attn_o"), None


def _rglru_mixer(h, P, PX, j, n_ctx):
    gate, xb = _split_linear(h, P["lru_w_in"][j], PX["lru_w_in"][j], "lru_in")
    R = xb.shape[1]
    cb = _div_tile(R, 512, LANE)
    sp = jax.nn.softplus(-P["lru_lambda"][j])
    hs = []
    for d in range(2):
        K = P["lru_conv_w"][j].shape[1]
        uc = dwconv(xb, P["lru_conv_w"][j][d], P["lru_conv_b"][j][d][None], 0 if d == 1 else K - 1, n_ctx,
                    "lru_conv")
        ra = blockdiag_linear(uc, P["lru_wa"][j][d], "lru_wa")
        rx = blockdiag_linear(uc, P["lru_wx"][j][d], "lru_wx")
        a, bt = rowwise("lru_gates", _lru_gates_f, [ra, rx, uc],
                        [P["lru_ba"][j][d][None], P["lru_bx"][j][d][None], sp[d][None]], ["vec"] * 3, 2,
                        ROW_BLOCK, cb=cb)
        hs.append(linear_scan(a, bt, d, d == 1, "lru_scan"))
    y_in = rowwise("lru_gelu", lambda g, h0, h1: ((jax.nn.gelu(g) * (h0 + h1)).astype(BF16),), [gate, hs[0], hs[1]], [], [], 1,
                   ROW_BLOCK, cb=cb)[0]
    return linear(y_in, P["lru_w_out"][j], PX["lru_w_out"][j], grad_dtype=BF16, name="lru_out"), None


def _conformer_mixer(h, P, PX, j, n_ctx):
    z1, z2 = _split_linear(h, P["conf_w_in"][j], PX["conf_w_in"][j], "conf_in")
    Dm = z1.shape[1]
    b_in = P["conf_b_in"][j]
    z = rowwise("conf_glu", lambda a, b, ba, bb: ((a + ba) * jax.nn.sigmoid(b + bb),), [z1, z2],
                [b_in[None, :Dm], b_in[None, Dm:]], ["vec", "vec"], 1, ROW_BLOCK, cb=_div_tile(Dm, 512, LANE))[0]
    K = P["conf_dw_w"][j].shape[0]
    zc = dwconv(z, P["conf_dw_w"][j], P["conf_dw_b"][j][None], K // 2, n_ctx, "conf_conv")
    zs = rowwise("conf_ln_silu", _ln_silu_f, [zc], [P["conf_ln_g"][j][None], P["conf_ln_b"][j][None]],
                 ["vec", "vec"], 1, ROW_BLOCK)[0]
    y = linear(zs, P["conf_w_out"][j], PX["conf_w_out"][j], grad_dtype=BF16, name="conf_out")
    return y, P["conf_b_out"][j][None]


def _conv_ffn(u, P, PX, i, n_ctx):
    g, v = _split_linear(u, P["ffn_w_up"][i], PX["ffn_w_up"][i], "ffn_up")
    a = ffn_glu(g, v, P["ffn_conv_w"][i], P["ffn_conv_b"][i][None], n_ctx, "ffn_glu")
    return linear(a, P["ffn_w_down"][i], PX["ffn_w_down"][i], grad_dtype=BF16, name="ffn_down")


def local_loss(x_all, mods, P, PX, target, n_ctx):
    assert n_ctx == ROW_BLOCK
    depth = len(mods)
    tabs = rope_tables(n_ctx, x_all.shape[0] - n_ctx)
    x = x_all
    h = _modulate(x, P["norm_mix_g"][0][None], mods[0][0], mods[0][1])
    for i in range(depth):
        kind, j = i % 3, i // 3
        _, _, g1, sh2, sc2, g2 = mods[i]
        if i == 1:
            late = [(k, l) for k in COL_SHARDED + ROW_SHARDED for l in range(len(P[k]))
                    if not (l == 0 and k.startswith(("attn", "ffn")))]
            x, h, *tied = lax.optimization_barrier((x, h, *[P[k][l] for k, l in late]))
            P = {k: list(v) if isinstance(v, list) else v for k, v in P.items()}
            for (k, l), w in zip(late, tied):
                P[k][l] = w
        if kind == 0:
            y, bias = _attention_mixer(h, P, PX, j, n_ctx, tabs)
        elif kind == 1:
            y, bias = _rglru_mixer(h, P, PX, j, n_ctx)
        else:
            y, bias = _conformer_mixer(h, P, PX, j, n_ctx)
        x, u = _residual_modulate(x, y, bias, g1, P["norm_ffn_g"][i][None], sh2, sc2)
        f = _conv_ffn(u, P, PX, i, n_ctx)
        if i + 1 < depth:
            x, h = _residual_modulate(x, f, None, g2, P["norm_mix_g"][i + 1][None], mods[i + 1][0], mods[i + 1][1])
        else:
            x = _residual(x, f, g2)
    per_feature = rowwise("loss_head", _loss_f, [x[n_ctx:], target], [P["final_norm_g"][None]], ["vec"], 0,
                          ROW_BLOCK, nograd=(1,))[0]
    return jnp.sum(per_feature)


def _plane_peers():
    x, y, c = lax.axis_index("x"), lax.axis_index("y"), lax.axis_index("c")
    me = 2 * x + y
    peers = [((1 - x, y, c), 2 * (1 - x) + y),
             ((x, 1 - y, c), 2 * x + (1 - y)),
             ((1 - x, 1 - y, c), 2 * (1 - x) + (1 - y))]
    return me, peers


def plane_allgather(arrays, layers, name):
    flat = []
    for k, L in enumerate(layers):
        flat += [(k, None)] if L is None else [(k, l) for l in range(L)]
    n_in, n = len(arrays), len(flat)

    def body(*refs):
        ins, outs = refs[:n_in], refs[n_in:n_in + n]
        lsem, ssem, rsem = refs[n_in + n:]
        me, peers = _plane_peers()

        def src(t):
            k, l = flat[t]
            return ins[k] if l is None else ins[k].at[l]

        def remote(t, p, slot):
            return pltpu.make_async_remote_copy(src(t), outs[t].at[slot], ssem.at[3 * t + p], rsem.at[3 * t + p],
                                                device_id=peers[p][0], device_id_type=MESH)

        local = [pltpu.make_async_copy(src(t), outs[t].at[me], lsem.at[t]) for t in range(n)]
        for t in range(n):
            local[t].start()
            for p in range(3):
                remote(t, p, me).start()
        for t in range(n):
            local[t].wait()
            for p in range(3):
                remote(t, p, peers[p][1]).wait()

    out_shape = []
    for k, l in flat:
        shp = arrays[k].shape if l is None else arrays[k].shape[1:]
        out_shape.append(jax.ShapeDtypeStruct((4,) + tuple(shp), arrays[k].dtype))
    res = pl.pallas_call(
        body, name=name, in_specs=[ANY] * n_in, out_specs=[ANY] * n, out_shape=out_shape,
        scratch_shapes=[pltpu.SemaphoreType.DMA((n,)), pltpu.SemaphoreType.DMA((3 * n,)),
                        pltpu.SemaphoreType.DMA((3 * n,))],
    )(*arrays)
    out, t = [], 0
    for L in layers:
        if L is None:
            out.append(res[t])
            t += 1
        else:
            out.append(list(res[t:t + L]))
            t += L
    return out


def plane_alltoall(groups, name):
    flat = [(k, l) for k, grp in enumerate(groups) for l in range(len(grp))]
    arrays = [a for grp in groups for a in grp]
    n, ng = len(flat), len(groups)

    def body(*refs):
        ins, outs = refs[:n], refs[n:n + ng]
        lsem, ssem, rsem = refs[n + ng:]
        me, peers = _plane_peers()

        def remote(t, p, src_slot, dst_slot):
            k, l = flat[t]
            return pltpu.make_async_remote_copy(ins[t].at[src_slot], outs[k].at[dst_slot, l], ssem.at[3 * t + p],
                                                rsem.at[3 * t + p], device_id=peers[p][0], device_id_type=MESH)

        local = [pltpu.make_async_copy(ins[t].at[me], outs[flat[t][0]].at[me, flat[t][1]], lsem.at[t])
                 for t in range(n)]
        for t in range(n):
            local[t].start()
            for p in range(3):
                remote(t, p, peers[p][1], me).start()
        for t in range(n):
            local[t].wait()
            for p in range(3):
                remote(t, p, peers[p][1], peers[p][1]).wait()

    out_shape = [jax.ShapeDtypeStruct((4, len(grp)) + tuple(grp[0].shape[1:]), grp[0].dtype) for grp in groups]
    return pl.pallas_call(
        body, name=name, in_specs=[ANY] * n, out_specs=[ANY] * ng, out_shape=out_shape,
        scratch_shapes=[pltpu.SemaphoreType.DMA((n,)), pltpu.SemaphoreType.DMA((3 * n,)),
                        pltpu.SemaphoreType.DMA((3 * n,))],
    )(*arrays)


def sibling_exchange(arrays, name):
    n = len(arrays)

    def body(*refs):
        ins, outs = refs[:n], refs[n:2 * n]
        ssem, rsem = refs[2 * n:]
        sibling = (lax.axis_index("x"), lax.axis_index("y"), 1 - lax.axis_index("c"))
        copies = [pltpu.make_async_remote_copy(ins[t], outs[t], ssem.at[t], rsem.at[t], device_id=sibling,
                                               device_id_type=MESH) for t in range(n)]
        for cp in copies:
            cp.start()
        for cp in copies:
            cp.wait()

    return pl.pallas_call(
        body, name=name, in_specs=[ANY] * n, out_specs=[ANY] * n,
        out_shape=[jax.ShapeDtypeStruct(a.shape, a.dtype) for a in arrays],
        scratch_shapes=[pltpu.SemaphoreType.DMA((n,)), pltpu.SemaphoreType.DMA((n,))],
    )(*arrays)


def _sibling():
    return (lax.axis_index("x"), lax.axis_index("y"), 1 - lax.axis_index("c"))


def _rows_half(ref, h, rows, axis):
    idx = (slice(None),) * axis + (pl.ds(h * (rows // 2), rows // 2),)
    return ref.at[idx]


def plane_allgather_shared(arrays, layers, name):
    flat = []
    for k, L in enumerate(layers):
        flat += [(k, None)] if L is None else [(k, l) for l in range(L)]
    n_in, n = len(arrays), len(flat)
    shard_rows = [arrays[k].shape[0] if l is None else arrays[k].shape[1] for k, l in flat]
    assert all(r % 32 == 0 for r in shard_rows)

    def body(*refs):
        ins, outs = refs[:n_in], refs[n_in:n_in + n]
        isend, irecv, dsend, drecv = refs[n_in + n:]
        me, peers = _plane_peers()
        c = lax.axis_index("c")

        def src(t):
            k, l = flat[t]
            return ins[k] if l is None else ins[k].at[l]

        def over_ici(t, p, slot):
            return pltpu.make_async_remote_copy(
                _rows_half(src(t), c, shard_rows[t], 0), _rows_half(outs[t].at[slot], c, shard_rows[t], 0),
                isend.at[3 * t + p], irecv.at[3 * t + p], device_id=peers[p][0], device_id_type=MESH)

        def over_d2d(t, p, h):
            piece = _rows_half(outs[t].at[peers[p][1]], h, shard_rows[t], 0)
            return pltpu.make_async_remote_copy(piece, piece, dsend.at[3 * t + p], drecv.at[3 * t + p],
                                                device_id=_sibling(), device_id_type=MESH)

        for t in range(n):
            for p in range(3):
                over_ici(t, p, me).start()
        for t in range(n):
            for p in range(3):
                over_ici(t, p, peers[p][1]).wait_recv()
                over_d2d(t, p, c).start()
        for t in range(n):
            for p in range(3):
                over_d2d(t, p, 1 - c).wait_recv()
                over_ici(t, p, me).wait_send()
                over_d2d(t, p, c).wait_send()

    out_shape = []
    for k, l in flat:
        shp = arrays[k].shape if l is None else arrays[k].shape[1:]
        out_shape.append(jax.ShapeDtypeStruct((4,) + tuple(shp), arrays[k].dtype))
    res = pl.pallas_call(
        body, name=name, in_specs=[ANY] * n_in, out_specs=[ANY] * n, out_shape=out_shape,
        scratch_shapes=[pltpu.SemaphoreType.DMA((3 * n,))] * 4,
    )(*arrays)
    me = 2 * lax.axis_index("x") + lax.axis_index("y")
    res = [lax.dynamic_update_index_in_dim(r, arrays[k] if l is None else arrays[k][l], me, 0)
           for r, (k, l) in zip(res, flat)]
    out, t = [], 0
    for L in layers:
        if L is None:
            out.append(res[t])
            t += 1
        else:
            out.append(list(res[t:t + L]))
            t += L
    return out


def plane_allgather_async(arrays, layers, name, collective_id):
    flat = []
    for k, L in enumerate(layers):
        flat += [(k, None)] if L is None else [(k, l) for l in range(L)]
    n = len(flat)
    shard_rows = [arrays[k].shape[0] if l is None else arrays[k].shape[1] for k, l in flat]
    assert all(r % 32 == 0 for r in shard_rows)
    in_refs = [jax.new_ref(a, memory_space=pltpu.MemorySpace.HBM) for a in arrays]
    out_refs = []
    for k, l in flat:
        shp = arrays[k].shape if l is None else arrays[k].shape[1:]
        out_refs.append(jax.empty_ref(jax.ShapeDtypeStruct((4,) + tuple(shp), arrays[k].dtype),
                                      memory_space=pltpu.MemorySpace.HBM))

    @pl.kernel(mesh=plsc.ScalarSubcoreMesh(axis_name="sequencer", num_cores=1), name=name,
               scratch_types=(pltpu.SemaphoreType.DMA((3 * n,)),) * 4,
               compiler_params=pltpu.CompilerParams(collective_id=collective_id))
    def launch(isend, irecv, dsend, drecv):
        me, peers = _plane_peers()
        c = lax.axis_index("c")
        barrier = pltpu.get_barrier_semaphore()
        for dev in [dev for dev, _ in peers] + [_sibling()]:
            pl.semaphore_signal(barrier, inc=1, device_id=dev, device_id_type=MESH)
        pl.semaphore_wait(barrier, 4)

        def src(t):
            k, l = flat[t]
            return in_refs[k] if l is None else in_refs[k].at[l]

        def over_ici(t, p, slot):
            return pltpu.make_async_remote_copy(
                _rows_half(src(t), c, shard_rows[t], 0), _rows_half(out_refs[t].at[slot], c, shard_rows[t], 0),
                isend.at[3 * t + p], irecv.at[3 * t + p], device_id=peers[p][0], device_id_type=MESH)

        def over_d2d(t, p, h):
            piece = _rows_half(out_refs[t].at[peers[p][1]], h, shard_rows[t], 0)
            return pltpu.make_async_remote_copy(piece, piece, dsend.at[3 * t + p], drecv.at[3 * t + p],
                                                device_id=_sibling(), device_id_type=MESH)

        for t in range(n):
            for p in range(3):
                over_ici(t, p, me).start()
        for t in range(n):
            for p in range(3):
                over_ici(t, p, peers[p][1]).wait_recv()
                over_d2d(t, p, c).start()
        for t in range(n):
            for p in range(3):
                over_d2d(t, p, 1 - c).wait_recv()
                over_ici(t, p, me).wait_send()
                over_d2d(t, p, c).wait_send()

    launch()
    me = 2 * lax.axis_index("x") + lax.axis_index("y")
    res = [lax.dynamic_update_index_in_dim(r[...], arrays[k] if l is None else arrays[k][l], me, 0)
           for r, (k, l) in zip(out_refs, flat)]
    out, t = [], 0
    for L in layers:
        if L is None:
            out.append(res[t])
            t += 1
        else:
            out.append(list(res[t:t + L]))
            t += L
    return out


def plane_alltoall_async(groups, name, collective_id):
    flat = [(k, l) for k, grp in enumerate(groups) for l in range(len(grp))]
    arrays = [a for grp in groups for a in grp]
    n = len(flat)
    in_refs = [jax.new_ref(a, memory_space=pltpu.MemorySpace.HBM) for a in arrays]
    out_refs = [jax.empty_ref(jax.ShapeDtypeStruct((4, len(grp)) + tuple(grp[0].shape[1:]), grp[0].dtype),
                              memory_space=pltpu.MemorySpace.HBM) for grp in groups]

    @pl.kernel(mesh=plsc.ScalarSubcoreMesh(axis_name="sequencer", num_cores=1), name=name,
               scratch_types=(pltpu.SemaphoreType.DMA((3 * n,)),) * 2,
               compiler_params=pltpu.CompilerParams(collective_id=collective_id))
    def launch(ssem, rsem):
        me, peers = _plane_peers()
        barrier = pltpu.get_barrier_semaphore()
        for dev, _ in peers:
            pl.semaphore_signal(barrier, inc=1, device_id=dev, device_id_type=MESH)
        pl.semaphore_wait(barrier, 3)

        def remote(t, p, src_slot, dst_slot):
            k, l = flat[t]
            return pltpu.make_async_remote_copy(in_refs[t].at[src_slot], out_refs[k].at[dst_slot, l], ssem.at[3 * t + p],
                                                rsem.at[3 * t + p], device_id=peers[p][0], device_id_type=MESH)

        for t in range(n):
            for p in range(3):
                remote(t, p, peers[p][1], me).start()
        for t in range(n):
            for p in range(3):
                remote(t, p, peers[p][1], peers[p][1]).wait()

    launch()
    me = 2 * lax.axis_index("x") + lax.axis_index("y")
    res = [r[...] for r in out_refs]
    for a, (k, l) in zip(arrays, flat):
        own = lax.dynamic_index_in_dim(a, me, 0, keepdims=True)[:, None]
        res[k] = lax.dynamic_update_slice(res[k], own, (me, l) + (0,) * (a.ndim - 1))
    return res


def add_pair(a, b, out_dtype, name):
    shp = a.shape
    R, C = _shape2d(shp)
    tr = _row_tile(R, C, 1 << 20)

    def body(a_ref, b_ref, o_ref):
        o_ref[...] = (a_ref[...].astype(F32) + b_ref[...].astype(F32)).astype(out_dtype)

    spec = pl.BlockSpec((tr, C), lambda i: (i, 0))
    return pl.pallas_call(
        body, name=name, grid=(R // tr,), in_specs=[spec, spec], out_specs=spec,
        out_shape=jax.ShapeDtypeStruct((R, C), out_dtype), compiler_params=_cparams(("parallel",)),
    )(a.reshape(R, C), b.reshape(R, C)).reshape(shp)


def gather_all_devices(a, name):
    own = plane_allgather([a], [None], name + "_plane")[0]
    sib = sibling_exchange([own], name + "_sibling")[0]
    c = lax.axis_index("c")
    c0 = jnp.where(c == 0, own, sib)
    c1 = jnp.where(c == 0, sib, own)
    return jnp.stack([c0, c1], axis=1).reshape((8,) + a.shape)


def _shape2d(shape):
    if len(shape) >= 2 and shape[-1] % LANE == 0:
        return (math.prod(shape[:-1]), shape[-1])
    return tuple(shape) if len(shape) == 2 else (1, math.prod(shape))


def _as2d(a):
    return a.reshape(_shape2d(a.shape))


def _row_tile(R, C, budget_bytes):
    if R * C * 4 <= budget_bytes or R % SUBLANE:
        return R
    cap = max(SUBLANE, (budget_bytes // (C * 4)) // SUBLANE * SUBLANE)
    return _div_tile(R, cap, SUBLANE)


def sum_slots(x4, name):
    shp = x4.shape[1:]
    R, C = _shape2d(shp)
    v = x4.reshape(4, R, C)
    tr = _row_tile(R, C, 1 << 20)

    def body(x_ref, o_ref):
        f = lambda s: x_ref[s].astype(F32)
        o_ref[...] = ((f(0) + f(1)) + f(2)) + f(3)

    out = pl.pallas_call(
        body, name=name, grid=(R // tr,),
        in_specs=[pl.BlockSpec((4, tr, C), lambda i: (0, i, 0))],
        out_specs=pl.BlockSpec((tr, C), lambda i: (i, 0)),
        out_shape=jax.ShapeDtypeStruct((R, C), F32),
        compiler_params=_cparams(("parallel",)),
    )(v)
    return out.reshape(shp)


def adamw(w, m, v, terms, name):
    shp = w.shape
    w2, m2, v2 = _as2d(w), _as2d(m), _as2d(v)
    flat = [_as2d(t) for inner in terms for t in inner]
    sizes = [len(inner) for inner in terms]
    R, C = w2.shape
    tr = _row_tile(R, C, 1 << 20)

    def body(*refs):
        w_ref, m_ref, v_ref = refs[:3]
        t_refs = refs[3:3 + len(flat)]
        g_ref, d_ref, nm_ref, nv_ref = refs[3 + len(flat):]
        g, t = None, 0
        for sz in sizes:
            inner = t_refs[t][...].astype(F32)
            for q in range(1, sz):
                inner = inner + t_refs[t + q][...].astype(F32)
            t += sz
            g = inner if g is None else g + inner
        nm = ADAM_B1 * m_ref[...] + (1.0 - ADAM_B1) * g
        nv = ADAM_B2 * v_ref[...] + (1.0 - ADAM_B2) * jnp.square(g)
        m_hat = nm / (1.0 - ADAM_B1 ** ADAM_STEP)
        v_hat = nv / (1.0 - ADAM_B2 ** ADAM_STEP)
        g_ref[...] = g
        d_ref[...] = -ADAM_LR * (m_hat / (jnp.sqrt(v_hat) + ADAM_EPS) + ADAM_WD * w_ref[...])
        nm_ref[...] = nm
        nv_ref[...] = nv

    spec = pl.BlockSpec((tr, C), lambda i: (i, 0))
    outs = pl.pallas_call(
        body, name=name, grid=(R // tr,),
        in_specs=[spec] * (3 + len(flat)), out_specs=[spec] * 4,
        out_shape=[jax.ShapeDtypeStruct((R, C), F32)] * 4,
        compiler_params=_cparams(("parallel",)),
    )(w2, m2, v2, *flat)
    return tuple(o.reshape(shp) for o in outs)


def _pack(arrs):
    flat = jnp.concatenate([a.reshape(-1).astype(F32) for a in arrs])
    unit = 32 * LANE
    pad = (-flat.shape[0]) % unit
    return jnp.pad(flat, (0, pad)).reshape(-1, LANE)


def _unpack(pack, shapes, lead=()):
    flat = pack.reshape(tuple(lead) + (-1,))
    out, off = [], 0
    for s in shapes:
        n = math.prod(s)
        out.append(flat[..., off:off + n].reshape(tuple(lead) + tuple(s)))
        off += n
    return out


COL_SHARDED = ("attn_w_qkv", "lru_w_in", "conf_w_in", "ffn_w_up")
ROW_SHARDED = ("attn_w_o", "lru_w_out", "conf_w_out", "ffn_w_down")
SMALL_SHARDED = ("lru_conv_w", "lru_conv_b", "lru_ba", "lru_bx", "lru_lambda", "conf_b_in", "conf_dw_w", "conf_dw_b",
                 "conf_ln_g", "conf_ln_b", "conf_b_out", "ffn_conv_w")
REPLICATED = ("norm_mix_g", "norm_ffn_g", "attn_sink", "lru_wa", "lru_wx", "ffn_conv_b", "final_norm_g")
LOCAL_ONLY = ("c_ctx", "ada_b")
WEIGHTS = ("c_ctx", "ada_w", "ada_b", "norm_mix_g", "norm_ffn_g", "attn_w_qkv", "attn_w_o", "attn_sink", "lru_w_in",
           "lru_conv_w", "lru_conv_b", "lru_wa", "lru_ba", "lru_wx", "lru_bx", "lru_lambda", "lru_w_out", "conf_w_in",
           "conf_b_in", "conf_dw_w", "conf_dw_b", "conf_ln_g", "conf_ln_b", "conf_w_out", "conf_b_out", "ffn_w_up",
           "ffn_conv_w", "ffn_conv_b", "ffn_w_down", "final_norm_g")


def _full_last_axis(g4):
    moved = jnp.moveaxis(g4, 0, -2)
    return moved.reshape(moved.shape[:-2] + (-1,))


def _shards_last_axis(full):
    split = full.reshape(full.shape[:-1] + (4, full.shape[-1] // 4))
    return jnp.moveaxis(split, -2, 0)


def _train_step(W, M, V, x, c, ctx, loss_target):
    n_ctx = ctx.shape[1]
    depth = W["ada_w"].shape[0]
    D = x.shape[-1]
    my_c = lax.axis_index("c")
    my_s = 2 * lax.axis_index("x") + lax.axis_index("y")
    my_dev = 2 * my_s + my_c

    c8 = gather_all_devices(c, "gather_c").reshape(8, D)
    cond16 = jnp.concatenate([c8, jnp.broadcast_to(W["c_ctx"][None], (8, D))], axis=0)
    act16, act_vjp = jax.vjp(jax.nn.silu, cond16)
    n_ada = W["ada_w"].shape[-1]
    m_shard = matmul_nn(act16, W["ada_w"], name="ada_fwd")
    m4 = plane_allgather([m_shard], [None], "gather_ada")[0].reshape(4, 16, depth, n_ada)
    m_full = m4.transpose(2, 1, 0, 3).reshape(depth, 16, 4 * n_ada) + W["ada_b"][:, None, :]

    big = COL_SHARDED + ROW_SHARDED
    small_shapes = [W[k].shape for k in SMALL_SHARDED]
    small_pack = _pack([W[k] for k in SMALL_SHARDED])
    first = [k for k in big if k.startswith(("attn", "ffn"))]
    bf16 = {k: W[k].astype(BF16) for k in big}
    now = plane_allgather_shared([bf16[k][:1] for k in first] + [small_pack], [1] * len(first) + [None],
                                 "gather_weights")
    later_keys = [k for k in big if k not in first or W[k].shape[0] > 1]
    later_shards = [bf16[k][1:] if k in first else bf16[k] for k in later_keys]
    m_full, now, later_shards = lax.optimization_barrier((m_full, now, later_shards))
    later = plane_allgather_async(later_shards,
                                  [W[k].shape[0] - 1 if k in first else W[k].shape[0] for k in later_keys],
                                  "gather_weights_later", 1)
    gathered = [(now[first.index(k)] if k in first else []) + (later[later_keys.index(k)] if k in later_keys else [])
                for k in big] + [now[-1]]
    P, PX = {}, {}
    for k, per_layer in zip(big, gathered[:len(big)]):
        if k in ROW_SHARDED:
            per_layer = [g.reshape(1, -1, g.shape[-1]) for g in per_layer]
        P[k] = per_layer
        PX[k] = [jnp.zeros(g.shape, BF16) for g in per_layer]
    for k, g4 in zip(SMALL_SHARDED, _unpack(gathered[-1], small_shapes, lead=(4,))):
        P[k] = _full_last_axis(g4)
    for k in REPLICATED:
        P[k] = W[k]

    m_lat = lax.dynamic_index_in_dim(m_full, my_dev, axis=1, keepdims=False)
    m_ctx = m_full[:, 8]
    mods = [tuple(jnp.stack([a, b])[:, None, :] for a, b in zip(jnp.split(m_ctx[i], 6), jnp.split(m_lat[i], 6)))
            for i in range(depth)]

    x_all = jnp.concatenate([ctx[0], x[0]], axis=0)
    loss, (gx, gmods, gP, gPX) = jax.value_and_grad(
        lambda xa, md, p, px: local_loss(xa, md, p, px, loss_target[0], n_ctx), argnums=(0, 1, 2, 3))(x_all, mods, P, PX)
    loss = lax.psum(loss, ("x", "y", "c"))
    grad_x = gx[n_ctx:][None]

    grad_pieces = {k: [g.reshape((4, -1, g.shape[-1])) for g in gPX[k]] if k in ROW_SHARDED else gPX[k] for k in big}
    received_later = plane_alltoall_async([grad_pieces[k][1:] if k in first else grad_pieces[k] for k in later_keys],
                                          "scatter_grads_later", 2)

    dm_mine = jnp.stack([jnp.stack([jnp.concatenate([g[r, 0] for g in gmods[i]]) for i in range(depth)])
                         for r in range(2)])
    dm8 = gather_all_devices(dm_mine, "gather_dmod")
    dm16 = jnp.concatenate([dm8[:, 1], dm8[:, 0]], axis=0).transpose(1, 0, 2)
    grad_ada_b = rowwise("ada_b_grad", lambda a: (jnp.sum(a, axis=0, keepdims=True),),
                         [dm16.transpose(1, 0, 2).reshape(16, -1)], [], [], 0, 16)[0].reshape(depth, -1)
    dm_cols = lax.dynamic_slice_in_dim(dm16, my_s * n_ada, n_ada, axis=2)
    dm_cols = dm_cols.transpose(1, 0, 2).reshape(16, depth * n_ada)
    grad_ada_w = matmul_tn(act16, dm_cols, depth, name="ada_dw")
    dact_part = matmul_nt(dm_cols, W["ada_w"], name="ada_dx")
    dact4 = plane_allgather([dact_part], [None], "gather_dact")[0]
    dact = ((dact4[0] + dact4[1]) + dact4[2]) + dact4[3]
    grad_c_ctx = jnp.sum(act_vjp(dact)[0][8:], axis=0)

    out = {}
    out["ada_w"] = adamw(W["ada_w"], M["ada_w"], V["ada_w"], [[grad_ada_w]], "adamw_ada_w")
    local_shapes = [W[k].shape for k in LOCAL_ONLY]
    res = adamw(_pack([W[k] for k in LOCAL_ONLY]), _pack([M[k] for k in LOCAL_ONLY]), _pack([V[k] for k in LOCAL_ONLY]),
                [[_pack([grad_c_ctx, grad_ada_b])]], "adamw_local")
    for k, vals in zip(LOCAL_ONLY, zip(*[_unpack(r, local_shapes) for r in res])):
        out[k] = vals

    groups = [grad_pieces[k][:1] for k in first]
    small_grads = _pack_shards([_shards_last_axis(gP[k]) for k in SMALL_SHARDED])
    groups.append([small_grads])
    def rows_half(a, h):
        return lax.dynamic_slice_in_dim(a, h * (a.shape[1] // 2), a.shape[1] // 2, axis=1)

    pieces = [a for grp in groups for a in grp]
    theirs = sibling_exchange([rows_half(a, 1 - my_c) for a in pieces], "swap_grad_halves")
    chip_sums = [add_pair(rows_half(a, my_c), t, a.dtype, "chip_sum") for a, t in zip(pieces, theirs)]
    chip_groups, t = [], 0
    for grp in groups:
        chip_groups.append(chip_sums[t:t + len(grp)])
        t += len(grp)
    received = plane_alltoall(chip_groups, "scatter_grads")
    half_sums = [sum_slots(r, "plane_sum") for r in received]
    other = sibling_exchange(half_sums, "merge_grad_halves")
    full = [jnp.concatenate([jnp.where(my_c == 0, mine, got), jnp.where(my_c == 0, got, mine)], axis=1)
            for mine, got in zip(half_sums, other)]
    own_later = [sum_slots(r, "plane_sum_later") for r in received_later]
    sib_later = sibling_exchange(own_later, "sibling_grads_later")
    grad_later = [add_pair(o, s, F32, "pair_sum_later") for o, s in zip(own_later, sib_later)]
    for k in big:
        parts = ([full[first.index(k)]] if k in first else []) + \
                ([grad_later[later_keys.index(k)]] if k in later_keys else [])
        g = parts[0] if len(parts) == 1 else jnp.concatenate(parts, axis=0)
        out[k] = adamw(W[k], M[k], V[k], [[g]], "adamw_" + k)
    res = adamw(small_pack, _pack([M[k] for k in SMALL_SHARDED]), _pack([V[k] for k in SMALL_SHARDED]),
                [[full[-1][0]]], "adamw_small")
    for k, vals in zip(SMALL_SHARDED, zip(*[_unpack(r, small_shapes) for r in res])):
        out[k] = vals

    rep_shapes = [W[k].shape for k in REPLICATED]
    own = plane_allgather([_pack([gP[k] for k in REPLICATED])], [None], "gather_rep_grads")[0]
    sib = sibling_exchange([own], "sibling_rep_grads")[0]
    res = adamw(_pack([W[k] for k in REPLICATED]), _pack([M[k] for k in REPLICATED]), _pack([V[k] for k in REPLICATED]),
                [[own[s], sib[s]] for s in range(4)], "adamw_rep")
    for k, vals in zip(REPLICATED, zip(*[_unpack(r, rep_shapes) for r in res])):
        out[k] = vals

    return (loss, grad_x) + tuple(out[k][j] for j in range(4) for k in WEIGHTS)


def _pack_shards(arrs4):
    return jnp.stack([_pack([a[s] for a in arrs4]) for s in range(4)])


def kernel(x, c, ctx, c_ctx, ada_w, ada_b, norm_mix_g, norm_ffn_g, attn_w_qkv, attn_w_o, attn_sink, lru_w_in, lru_conv_w, lru_conv_b, lru_wa, lru_ba, lru_wx, lru_bx, lru_lambda, lru_w_out, conf_w_in, conf_b_in, conf_dw_w, conf_dw_b, conf_ln_g, conf_ln_b, conf_w_out, conf_b_out, ffn_w_up, ffn_conv_w, ffn_conv_b, ffn_w_down, final_norm_g, loss_target, m_c_ctx, m_ada_w, m_ada_b, m_norm_mix_g, m_norm_ffn_g, m_attn_w_qkv, m_attn_w_o, m_attn_sink, m_lru_w_in, m_lru_conv_w, m_lru_conv_b, m_lru_wa, m_lru_ba, m_lru_wx, m_lru_bx, m_lru_lambda, m_lru_w_out, m_conf_w_in, m_conf_b_in, m_conf_dw_w, m_conf_dw_b, m_conf_ln_g, m_conf_ln_b, m_conf_w_out, m_conf_b_out, m_ffn_w_up, m_ffn_conv_w, m_ffn_conv_b, m_ffn_w_down, m_final_norm_g, v_c_ctx, v_ada_w, v_ada_b, v_norm_mix_g, v_norm_ffn_g, v_attn_w_qkv, v_attn_w_o, v_attn_sink, v_lru_w_in, v_lru_conv_w, v_lru_conv_b, v_lru_wa, v_lru_ba, v_lru_wx, v_lru_bx, v_lru_lambda, v_lru_w_out, v_conf_w_in, v_conf_b_in, v_conf_dw_w, v_conf_dw_b, v_conf_ln_g, v_conf_ln_b, v_conf_w_out, v_conf_b_out, v_ffn_w_up, v_ffn_conv_w, v_ffn_conv_b, v_ffn_w_down, v_final_norm_g):
    given = dict(locals())
    W = {k: given[k] for k in WEIGHTS}
    M = {k: given["m_" + k] for k in WEIGHTS}
    V = {k: given["v_" + k] for k in WEIGHTS}
    return _train_step(W, M, V, x, c, ctx, loss_target)
```
